```python
import math
import jax
import jax.numpy as jnp
from jax import lax
import numpy as np

D_MODEL = 2048
BATCH = 8
SEQ = 4096
DEPTH = 1

HEAD_DIM = 128
D_MIX = D_MODEL
SB_HEADS = D_MIX // (2 * HEAD_DIM)
NSA_HEADS = D_MIX // (2 * HEAD_DIM)
NSA_KV_GROUPS = 2
NSA_GROUP_SIZE = NSA_HEADS // NSA_KV_GROUPS
SB_WIDTH = SB_HEADS * HEAD_DIM
NSA_WIDTH = NSA_HEADS * HEAD_DIM
NSA_KV_WIDTH = NSA_KV_GROUPS * HEAD_DIM
N_GATES = 3
CMP_STRIDE = 16
CMP_BLOCK = 2 * CMP_STRIDE
CMP_HIDDEN = 256
SEL_BLOCK = 64
SEL_TOP_N = 16
WINDOW = 512
SB_Q_BLOCK = 128
NSA_Q_BLOCK = 32
REL_BUCKETS = 32
REL_MAX_EXACT = 16
REL_MAX_DISTANCE = 1024
RMS_EPS = 1e-6
SEL_FORCE = 1e9
MASK_VALUE = -1e30
SPLIT_SIZES = (SB_WIDTH,) * 4 + (NSA_WIDTH,) + (NSA_KV_WIDTH,) * 6 + (N_GATES * NSA_HEADS, NSA_WIDTH)
D_IN_PROJ = sum(SPLIT_SIZES)

kernel_name = 'hybrid_stickbreaking_nsa_layer'


def rms_norm(x, g):
    xf = x.astype(jnp.float32)
    y = xf * lax.rsqrt(jnp.mean(xf * xf, axis=-1, keepdims=True) + RMS_EPS)
    return (y * g.astype(jnp.float32)).astype(x.dtype)


def rel_bucket(dist):
    n = jnp.maximum(dist, 0)
    nf = jnp.maximum(n, 1).astype(jnp.float32)
    large = REL_MAX_EXACT + (jnp.log(nf / REL_MAX_EXACT)
                             / math.log(REL_MAX_DISTANCE / REL_MAX_EXACT)
                             * (REL_BUCKETS - REL_MAX_EXACT)).astype(jnp.int32)
    large = jnp.minimum(large, REL_BUCKETS - 1)
    return jnp.where(n < REL_MAX_EXACT, n, large)


def masked_softmax(scores, mask):
    s = jnp.where(mask, scores.astype(jnp.float32), MASK_VALUE)
    return jnp.where(mask, jax.nn.softmax(s, axis=-1), 0.0)


def stick_breaking_attention(q, k, v):
    b, s, h, d = q.shape
    nqb = s // SB_Q_BLOCK
    q_blocks = q.reshape(b, nqb, SB_Q_BLOCK, h, d).transpose(1, 0, 2, 3, 4)
    vf = v.astype(jnp.float32)
    key_pos = jnp.arange(s)
    scale = 1.0 / math.sqrt(d)

    def block(args):
        q_blk, i = args
        t = i * SB_Q_BLOCK + jnp.arange(SB_Q_BLOCK)
        z = jnp.einsum('bthd,bshd->bhts', q_blk, k).astype(jnp.float32) * scale
        mask = key_pos[None, :] < t[:, None]
        log_beta = jax.nn.log_sigmoid(z)
        log_rest = jnp.where(mask, jax.nn.log_sigmoid(-z), 0.0)
        between = lax.cumsum(log_rest, axis=3, reverse=True) - log_rest
        a = jnp.where(mask, jnp.exp(log_beta + between), 0.0)
        return jnp.einsum('bhts,bshd->bthd', a, vf)

    out = lax.map(block, (q_blocks, jnp.arange(nqb)))
    return out.transpose(1, 0, 2, 3, 4).reshape(b, s, h * d).astype(q.dtype)


def compress_kv(kv, pos, w1, w2):
    b, s, g, d = kv.shape
    chunks = kv.reshape(b, s // CMP_STRIDE, CMP_STRIDE, g, d)
    blocks = jnp.concatenate([chunks[:, :-1], chunks[:, 1:]], axis=2)
    blocks = blocks + pos[None, None, :, None, :]
    hid = jax.nn.gelu(jnp.einsum('bnlgd,lde->bnge', blocks, w1))
    return jnp.einsum('bnge,ef->bngf', hid, w2)


def native_sparse_attention(q, k_cmp, v_cmp, k_sel, v_sel, k_win, v_win, gate_logits,
                            cmp_k_pos, cmp_k_w1, cmp_k_w2, cmp_v_pos, cmp_v_w1, cmp_v_w2,
                            rel_bias):
    b, s, g, r, d = q.shape
    scale = 1.0 / math.sqrt(d)
    kc = compress_kv(k_cmp, cmp_k_pos, cmp_k_w1, cmp_k_w2)
    vc = compress_kv(v_cmp, cmp_v_pos, cmp_v_w1, cmp_v_w2)
    nb = kc.shape[1]
    cmp_end = jnp.arange(nb) * CMP_STRIDE + CMP_BLOCK - 1
    nsel = s // SEL_BLOCK
    n_top = min(SEL_TOP_N, nsel)
    ci = np.arange(nb)[:, None] * CMP_STRIDE
    sj = np.arange(nsel)[None, :] * SEL_BLOCK
    overlap = jnp.asarray(((ci < sj + SEL_BLOCK) & (ci + CMP_BLOCK > sj)).astype(np.float32))
    ks_blocks = k_sel.reshape(b, nsel, SEL_BLOCK, g, d).transpose(0, 3, 1, 2, 4)
    vs_blocks = v_sel.reshape(b, nsel, SEL_BLOCK, g, d).transpose(0, 3, 1, 2, 4)
    kw_pad = jnp.pad(k_win, ((0, 0), (WINDOW, 0), (0, 0), (0, 0)))
    vw_pad = jnp.pad(v_win, ((0, 0), (WINDOW, 0), (0, 0), (0, 0)))
    win_len = NSA_Q_BLOCK + WINDOW
    bias_tbl = rel_bias.astype(jnp.float32).reshape(REL_BUCKETS, g, r).transpose(1, 2, 0)
    gather_blocks = jax.vmap(jax.vmap(lambda blk, idx: blk[idx]))

    nqb = s // NSA_Q_BLOCK
    q_blocks = q.reshape(b, nqb, NSA_Q_BLOCK, g, r, d).transpose(1, 0, 2, 3, 4, 5)
    g_blocks = gate_logits.reshape(b, nqb, NSA_Q_BLOCK, g, r, N_GATES).transpose(1, 0, 2, 3, 4, 5)

    def step(args):
        q_blk, g_blk, i = args
        start = i * NSA_Q_BLOCK
        t = start + jnp.arange(NSA_Q_BLOCK)
        bias_c = rel_bias[rel_bucket(t[:, None] - cmp_end[None, :])]
        bias_c = bias_c.transpose(2, 0, 1).reshape(g, r, NSA_Q_BLOCK, nb)
        sc = jnp.einsum('btgrd,bngd->bgrtn', q_blk, kc) * scale + bias_c
        p_c = masked_softmax(sc, cmp_end[None, :] <= t[:, None])
        o_c = jnp.einsum('bgrtn,bngd->btgrd', p_c, vc)
        imp = jnp.einsum('bgrtn,nj->bgtj', p_c, overlap)
        j = jnp.arange(nsel)[None, :]
        cur = (t // SEL_BLOCK)[:, None]
        valid = j * SEL_BLOCK <= t[:, None]
        forced = (j == 0) | (j == cur) | (j == cur - 1)
        score = jnp.where(valid, jnp.where(forced, SEL_FORCE, imp), -SEL_FORCE)
        _, idx = lax.top_k(score, n_top)
        ks_g = gather_blocks(ks_blocks, idx)
        vs_g = gather_blocks(vs_blocks, idx)
        key_pos = idx[..., None] * SEL_BLOCK + jnp.arange(SEL_BLOCK)
        dist = t[None, None, :, None, None] - key_pos
        bias_s = jnp.einsum('bgtnlk,grk->bgtrnl',
                            jax.nn.one_hot(rel_bucket(dist), REL_BUCKETS, dtype=jnp.float32), bias_tbl)
        ss = jnp.einsum('bgtrd,bgtnld->bgtrnl', q_blk.transpose(0, 2, 1, 3, 4), ks_g) * scale + bias_s
        n_keys = n_top * SEL_BLOCK
        p_s = masked_softmax(ss.reshape(b, g, NSA_Q_BLOCK, r, n_keys),
                             (dist >= 0).reshape(b, g, NSA_Q_BLOCK, 1, n_keys))
        o_s = jnp.einsum('bgtrm,bgtmd->btgrd', p_s, vs_g.reshape(b, g, NSA_Q_BLOCK, n_keys, d))
        kw = lax.dynamic_slice_in_dim(kw_pad, start, win_len, axis=1)
        vw = lax.dynamic_slice_in_dim(vw_pad, start, win_len, axis=1)
        win_pos = start - WINDOW + jnp.arange(win_len)
        diff = t[:, None] - win_pos[None, :]
        mask_w = (win_pos[None, :] >= 0) & (diff >= 0) & (diff < WINDOW)
        bias_w = rel_bias[rel_bucket(diff)].transpose(2, 0, 1).reshape(g, r, NSA_Q_BLOCK, win_len)
        sw = jnp.einsum('btgrd,bsgd->bgrts', q_blk, kw) * scale + bias_w
        p_w = masked_softmax(sw, mask_w)
        o_w = jnp.einsum('bgrts,bsgd->btgrd', p_w, vw)
        gate = jax.nn.sigmoid(g_blk.astype(jnp.float32))
        return gate[..., 0:1] * o_c + gate[..., 1:2] * o_s + gate[..., 2:3] * o_w

    out = lax.map(step, (q_blocks, g_blocks, jnp.arange(nqb)))
    return out.transpose(1, 0, 2, 3, 4, 5).reshape(b, s, g * r * d).astype(q.dtype)


def setup_inputs(seed: int = 0) -> dict:
    key = jax.random.key(seed)
    ks = jax.random.split(key, 16)
    f32 = jnp.float32
    nrm = lambda k, shape, sc: jax.random.normal(k, shape, f32) * sc
    return {
        'x': nrm(ks[0], (BATCH, SEQ, D_MODEL), 1.0),
        'norm_in': 1.0 + nrm(ks[1], (DEPTH, D_MODEL), 0.05),
        'w_in': nrm(ks[2], (DEPTH, D_MODEL, D_IN_PROJ), D_MODEL ** -0.5),
        'cmp_k_pos': nrm(ks[3], (DEPTH, CMP_BLOCK, HEAD_DIM), 0.5),
        'cmp_k_w1': nrm(ks[4], (DEPTH, CMP_BLOCK, HEAD_DIM, CMP_HIDDEN), (CMP_BLOCK * HEAD_DIM) ** -0.5),
        'cmp_k_w2': nrm(ks[5], (DEPTH, CMP_HIDDEN, HEAD_DIM), CMP_HIDDEN ** -0.5),
        'cmp_v_pos': nrm(ks[6], (DEPTH, CMP_BLOCK, HEAD_DIM), 0.5),
        'cmp_v_w1': nrm(ks[7], (DEPTH, CMP_BLOCK, HEAD_DIM, CMP_HIDDEN), (CMP_BLOCK * HEAD_DIM) ** -0.5),
        'cmp_v_w2': nrm(ks[8], (DEPTH, CMP_HIDDEN, HEAD_DIM), CMP_HIDDEN ** -0.5),
        'rel_bias': nrm(ks[9], (REL_BUCKETS, NSA_HEADS), 0.5),
        'norm_sb': 1.0 + nrm(ks[10], (DEPTH, SB_WIDTH), 0.05),
        'norm_nsa': 1.0 + nrm(ks[11], (DEPTH, NSA_WIDTH), 0.05),
        'w_out': nrm(ks[12], (DEPTH, D_MIX, D_MODEL), D_MIX ** -0.5),
        'norm_final': 1.0 + nrm(ks[13], (D_MODEL,), 0.05),
    }


def reference(x, norm_in, w_in, cmp_k_pos, cmp_k_w1, cmp_k_w2, cmp_v_pos, cmp_v_w1, cmp_v_w2,
              rel_bias, norm_sb, norm_nsa, w_out, norm_final):
    b, s, _ = x.shape
    split_points = [int(p) for p in np.cumsum(SPLIT_SIZES)[:-1]]
    g, r = NSA_KV_GROUPS, NSA_GROUP_SIZE
    h = x
    for layer in range(DEPTH):
        xn = rms_norm(h, norm_in[layer])
        proj = xn @ w_in[layer]
        (sb_q, sb_k, sb_v, sb_z, n_q, n_kc, n_vc, n_ks, n_vs, n_kw, n_vw,
         n_gate, n_z) = jnp.split(proj, split_points, axis=-1)
        sb_heads = lambda a: a.reshape(b, s, SB_HEADS, HEAD_DIM)
        kv_heads = lambda a: a.reshape(b, s, g, HEAD_DIM)
        o_sb = stick_breaking_attention(sb_heads(sb_q), sb_heads(sb_k), sb_heads(sb_v))
        o_nsa = native_sparse_attention(
            n_q.reshape(b, s, g, r, HEAD_DIM), kv_heads(n_kc), kv_heads(n_vc), kv_heads(n_ks),
            kv_heads(n_vs), kv_heads(n_kw), kv_heads(n_vw), n_gate.reshape(b, s, g, r, N_GATES),
            cmp_k_pos[layer], cmp_k_w1[layer], cmp_k_w2[layer],
            cmp_v_pos[layer], cmp_v_w1[layer], cmp_v_w2[layer], rel_bias)
        y_sb = rms_norm(o_sb, norm_sb[layer]) * jax.nn.silu(sb_z)
        y_nsa = rms_norm(o_nsa, norm_nsa[layer]) * jax.nn.silu(n_z)
        h = h + jnp.concatenate([y_sb, y_nsa], axis=-1) @ w_out[layer]
    return rms_norm(h, norm_final)
```

```python
import functools
import math

import numpy as np
import jax
import jax.numpy as jnp
from jax import lax
from jax.experimental import pallas as pl
from jax.experimental.pallas import tpu as pltpu

F32 = jnp.float32
BF16 = jnp.bfloat16

HEAD_DIM = 128
SB_HEADS = 8
NSA_HEADS = 8
NSA_KV_GROUPS = 2
NSA_GROUP_SIZE = NSA_HEADS // NSA_KV_GROUPS
SB_WIDTH = SB_HEADS * HEAD_DIM
NSA_WIDTH = NSA_HEADS * HEAD_DIM
NSA_KV_WIDTH = NSA_KV_GROUPS * HEAD_DIM
N_GATES = 3
CMP_STRIDE = 16
CMP_BLOCK = 2 * CMP_STRIDE
CMP_HIDDEN = 256
SEL_BLOCK = 64
SEL_TOP_N = 16
WINDOW = 512
REL_BUCKETS = 32
REL_MAX_EXACT = 16
REL_MAX_DISTANCE = 1024
RMS_EPS = 1e-6
SEL_FORCE = 1e9
MASK_VALUE = -1e30
SCALE = 1.0 / math.sqrt(HEAD_DIM)

V7X_LANES = 128
V7X_MXU_DIM = 256
V7X_VMEM_BYTES = 64 * 1024 * 1024

SB_TILE = V7X_MXU_DIM
NSA_TQ = V7X_LANES
NSA_TK = V7X_MXU_DIM
NSA_ROWS = NSA_GROUP_SIZE * NSA_TQ

_NT = (((1,), (1,)), ((), ()))


def _bucket_thresholds():
    n = np.arange(0, 4 * REL_MAX_DISTANCE)
    nf = np.maximum(n, 1).astype(np.float64)
    large = REL_MAX_EXACT + (np.log(nf / REL_MAX_EXACT) / math.log(REL_MAX_DISTANCE / REL_MAX_EXACT)
                             * (REL_BUCKETS - REL_MAX_EXACT)).astype(np.int64)
    bucket = np.where(n < REL_MAX_EXACT, n, np.minimum(large, REL_BUCKETS - 1))
    assert np.all(np.diff(bucket) >= 0) and bucket[-1] == REL_BUCKETS - 1
    return [int(n[bucket >= k][0]) for k in range(1, REL_BUCKETS)]


BUCKET_THR = _bucket_thresholds()
LAST_THR = BUCKET_THR[-1]
BIAS_CONST_IDX = -(-(LAST_THR + NSA_TK - 1) // NSA_TQ)
BIAS_FAR = 1 << 20


def _cparams(sem, vmem_bytes):
    return pltpu.CompilerParams(dimension_semantics=sem, vmem_limit_bytes=int(min(vmem_bytes, V7X_VMEM_BYTES)))


def _bias_table_kernel(rb_ref, o_ref, *, stride, offset, const_idx):
    h = pl.program_id(0)
    tile = pl.program_id(1)
    rows, cols = o_ref.shape[2], o_ref.shape[3]
    base = tile * NSA_TQ + offset
    if const_idx is not None:
        base = jnp.where(tile >= const_idx, BIAS_FAR, base)
    ti = lax.broadcasted_iota(jnp.int32, (rows, cols), 0)
    cj = lax.broadcasted_iota(jnp.int32, (rows, cols), 1)
    dist = base + ti - stride * cj
    val = jnp.full((rows, cols), rb_ref[0, h], F32)
    for k in range(1, REL_BUCKETS):
        val = jnp.where(dist >= BUCKET_THR[k - 1], rb_ref[k, h], val)
    o_ref[0, 0] = val


def _bias_table(rel_bias, n_tiles, cols, stride, offset, const_idx):
    kern = functools.partial(_bias_table_kernel, stride=stride, offset=offset, const_idx=const_idx)
    return pl.pallas_call(
        kern,
        grid=(NSA_HEADS, n_tiles),
        in_specs=[pl.BlockSpec(memory_space=pltpu.SMEM)],
        out_specs=pl.BlockSpec((1, 1, NSA_TQ, cols),
                               lambda h, t: (h // NSA_GROUP_SIZE, t, h % NSA_GROUP_SIZE, 0)),
        out_shape=jax.ShapeDtypeStruct((NSA_KV_GROUPS, n_tiles, NSA_ROWS, cols), F32),
        compiler_params=_cparams(("parallel", "parallel"), 8 << 20),
        name="bias_table",
    )(rel_bias)


def _proj_kernel(x_ref, g_ref, w_ref, o_ref, xn_ref, *, chunk):
    @pl.when(pl.program_id(1) == 0)
    def _normalise():
        def body(c, carry):
            rows = pl.ds(pl.multiple_of(c * chunk, chunk), chunk)
            x = x_ref[rows, :]
            ms = jnp.mean(x * x, axis=-1, keepdims=True)
            xn_ref[rows, :] = (x * lax.rsqrt(ms + RMS_EPS) * g_ref[...]).astype(BF16)
            return carry
        lax.fori_loop(0, x_ref.shape[0] // chunk, body, 0)

    o_ref[...] = jnp.dot(xn_ref[...], w_ref[...], preferred_element_type=F32).astype(o_ref.dtype)


def _rms_proj(x2d, g, w, out_dtype, tm, tn, name):
    m, d = x2d.shape
    n = w.shape[1]
    assert m % tm == 0 and n % tn == 0
    osz = jnp.dtype(out_dtype).itemsize
    vmem = 2 * tm * d * 4 + tm * d * 2 + 2 * d * tn * 2 + 2 * tm * tn * osz + tm * tn * 4 + (4 << 20)
    return pl.pallas_call(
        functools.partial(_proj_kernel, chunk=min(tm, 128)),
        grid=(m // tm, n // tn),
        in_specs=[pl.BlockSpec((tm, d), lambda i, j: (i, 0)),
                  pl.BlockSpec((1, d), lambda i, j: (0, 0)),
                  pl.BlockSpec((d, tn), lambda i, j: (0, j))],
        out_specs=pl.BlockSpec((tm, tn), lambda i, j: (i, j)),
        out_shape=jax.ShapeDtypeStruct((m, n), out_dtype),
        scratch_shapes=[pltpu.VMEM((tm, d), BF16)],
        compiler_params=_cparams(("parallel", "arbitrary"), vmem),
        name=name,
    )(x2d, g, w)


def _sb_kernel(q_ref, k_ref, v_ref, o_ref, acc_ref, run_ref):
    i = pl.program_id(2)
    t = SB_TILE
    q = q_ref[0]
    row = lax.broadcasted_iota(jnp.int32, (t, t), 0)
    col = lax.broadcasted_iota(jnp.int32, (t, t), 1)
    below = row > col
    suffix = jnp.where(below, 1.0, 0.0).astype(BF16)

    def tile(j, diag):
        start = pl.multiple_of(j * t, t)
        k = k_ref[0, pl.ds(start, t), :]
        v = v_ref[0, pl.ds(start, t), :]
        z = lax.dot_general(q, k, _NT, preferred_element_type=F32) * SCALE
        sp = jnp.log(1.0 + jnp.exp(-jnp.abs(z)))
        log_beta = jnp.minimum(z, 0.0) - sp
        log_rest = -jnp.maximum(z, 0.0) - sp
        if diag:
            log_rest = jnp.where(below, log_rest, 0.0)
        hi = log_rest.astype(BF16)
        lo = (log_rest - hi.astype(F32)).astype(BF16)
        between = (jnp.dot(hi, suffix, preferred_element_type=F32)
                   + jnp.dot(lo, suffix, preferred_element_type=F32)) + run_ref[...]
        a = jnp.exp(log_beta + between)
        if diag:
            a = jnp.where(below, a, 0.0)
        acc_ref[...] += jnp.dot(a.astype(BF16), v, preferred_element_type=F32)
        run_ref[...] += jnp.sum(log_rest, axis=-1, keepdims=True)

    acc_ref[...] = jnp.zeros_like(acc_ref)
    run_ref[...] = jnp.zeros_like(run_ref)
    tile(i, True)

    def body(jj, carry):
        tile(i - 1 - jj, False)
        return carry
    lax.fori_loop(0, i, body, 0)
    o_ref[0] = acc_ref[...]


def _sb_attention(proj_a, batch, seq):
    t = SB_TILE
    assert seq % t == 0
    kblk = SB_WIDTH // HEAD_DIM
    vmem = 2 * 2 * seq * HEAD_DIM * 2 + 4 * t * HEAD_DIM * 4 + 24 * t * t * 4 + (4 << 20)
    return pl.pallas_call(
        _sb_kernel,
        grid=(batch, SB_HEADS, seq // t),
        in_specs=[pl.BlockSpec((1, t, HEAD_DIM), lambda b, h, i: (b, i, h)),
                  pl.BlockSpec((1, seq, HEAD_DIM), lambda b, h, i: (b, 0, kblk + h)),
                  pl.BlockSpec((1, seq, HEAD_DIM), lambda b, h, i: (b, 0, 2 * kblk + h))],
        out_specs=pl.BlockSpec((1, t, HEAD_DIM), lambda b, h, i: (b, i, h)),
        out_shape=jax.ShapeDtypeStruct((batch, seq, SB_WIDTH), F32),
        scratch_shapes=[pltpu.VMEM((t, HEAD_DIM), F32), pltpu.VMEM((t, 1), F32)],
        compiler_params=_cparams(("parallel", "parallel", "arbitrary"), vmem),
        name="sb_attention",
    )(proj_a, proj_a, proj_a)


def _compress_kernel(cv_ref, pos_ref, w1_ref, w2_ref, ko_ref, vo_ref):
    nbp = cv_ref.shape[1]
    tok = 2 * NSA_KV_WIDTH
    for which, out_ref in ((0, ko_ref), (1, vo_ref)):
        for g in range(NSA_KV_GROUPS):
            c0 = (which * NSA_KV_GROUPS + g) * HEAD_DIM
            chunks = jnp.concatenate(
                [cv_ref[0, :, l * tok + c0:l * tok + c0 + HEAD_DIM] for l in range(CMP_STRIDE)],
                axis=1).astype(F32)
            first = jnp.dot((chunks + pos_ref[which, 0]).astype(BF16), w1_ref[which, 0],
                            preferred_element_type=F32)
            second = jnp.dot((chunks + pos_ref[which, 1]).astype(BF16), w1_ref[which, 1],
                             preferred_element_type=F32)
            hid = jax.nn.gelu(first + pltpu.roll(second, nbp - 1, 0))
            out = jnp.dot(hid.astype(BF16), w2_ref[which], preferred_element_type=F32)
            out_ref[0, g] = out.astype(BF16)


def _compress(cv, pos, w1, w2, batch, nbp):
    tokw = CMP_STRIDE * 2 * NSA_KV_WIDTH
    half = CMP_STRIDE * HEAD_DIM
    vmem = 2 * nbp * tokw * 2 + 2 * 4 * half * CMP_HIDDEN * 2 + 8 * nbp * half * 4 + (8 << 20)
    out_sds = jax.ShapeDtypeStruct((batch, NSA_KV_GROUPS, nbp, HEAD_DIM), BF16)
    out_spec = pl.BlockSpec((1, NSA_KV_GROUPS, nbp, HEAD_DIM), lambda b: (b, 0, 0, 0))
    return pl.pallas_call(
        _compress_kernel,
        grid=(batch,),
        in_specs=[pl.BlockSpec((1, nbp, tokw), lambda b: (b, 0, 0)),
                  pl.BlockSpec((2, 2, 1, half), lambda b: (0, 0, 0, 0)),
                  pl.BlockSpec((2, 2, half, CMP_HIDDEN), lambda b: (0, 0, 0, 0)),
                  pl.BlockSpec((2, CMP_HIDDEN, HEAD_DIM), lambda b: (0, 0, 0))],
        out_specs=[out_spec, out_spec],
        out_shape=[out_sds, out_sds],
        compiler_params=_cparams(("parallel",), vmem),
        name="kv_compress",
    )(cv, pos, w1, w2)


def _nsa_kernel(q_ref, kc_ref, vc_ref, ks_ref, vs_ref, kw_ref, vw_ref, gate_ref, tbl_ref, tblc_ref,
                ovt_ref, expand_ref, o_ref, acc_s, m_s, acc_w, m_w, keymask):
    i = pl.program_id(2)
    tq, tk, rows = NSA_TQ, NSA_TK, NSA_ROWS
    nbp = kc_ref.shape[2]
    seq = expand_ref.shape[1]
    nsel = seq // SEL_BLOCK
    t0 = i * tq
    qb = q_ref[0]
    q4 = jnp.concatenate([qb[:, r * HEAD_DIM:(r + 1) * HEAD_DIM] for r in range(NSA_GROUP_SIZE)], axis=0)

    def per_head(x, m):
        n = x.shape[-1]
        return jnp.where(m[None], x.reshape(NSA_GROUP_SIZE, tq, n), MASK_VALUE).reshape(rows, n)

    ti_c = lax.broadcasted_iota(jnp.int32, (tq, nbp), 0)
    n_c = lax.broadcasted_iota(jnp.int32, (tq, nbp), 1)
    mask_c = (n_c * CMP_STRIDE + (CMP_BLOCK - 1)) <= (t0 + ti_c)
    s_c = lax.dot_general(q4, kc_ref[0, 0], _NT, preferred_element_type=F32) * SCALE + tblc_ref[0, 0]
    s_c = per_head(s_c, mask_c)
    e_c = jnp.exp(s_c - jnp.max(s_c, axis=-1, keepdims=True))
    e_c = jnp.where(mask_c[None], e_c.reshape(NSA_GROUP_SIZE, tq, nbp), 0.0).reshape(rows, nbp)
    l_c = jnp.sum(e_c, axis=-1, keepdims=True)
    p_c = e_c / jnp.where(l_c > 0.0, l_c, 1.0)
    o_c = jnp.dot(p_c.astype(BF16), vc_ref[0, 0], preferred_element_type=F32)

    p_sum = p_c[0:tq]
    for r in range(1, NSA_GROUP_SIZE):
        p_sum = p_sum + p_c[r * tq:(r + 1) * tq]
    p_hi = p_sum.astype(BF16)
    p_lo = (p_sum - p_hi.astype(F32)).astype(BF16)
    ovt = ovt_ref[...]
    imp_t = (lax.dot_general(ovt, p_hi, _NT, preferred_element_type=F32)
             + lax.dot_general(ovt, p_lo, _NT, preferred_element_type=F32))
    jrow = lax.broadcasted_iota(jnp.int32, (V7X_LANES, tq), 0)
    tcol = t0 + lax.broadcasted_iota(jnp.int32, (V7X_LANES, tq), 1)
    cur = tcol // SEL_BLOCK
    valid = jrow * SEL_BLOCK <= tcol
    forced = (jrow == 0) | (jrow == cur) | (jrow == cur - 1)
    score = jnp.where(valid, jnp.where(forced, SEL_FORCE, imp_t), -SEL_FORCE)
    rank = jnp.zeros((V7X_LANES, tq), F32)
    for other in range(nsel):
        srow = score[other:other + 1, :]
        before = (srow > score) | ((srow >= score) & (jrow > other))
        rank = rank + jnp.where(before, 1.0, 0.0)
    sel_t = jnp.where((rank < float(min(SEL_TOP_N, nsel))) & (jrow < nsel), 1.0, 0.0)
    sel = sel_t.T.astype(BF16)
    key_sel = jnp.dot(sel, expand_ref[...], preferred_element_type=F32)
    for jt in range(seq // tk):
        keymask[jt] = key_sel[:, jt * tk:(jt + 1) * tk]

    diff = (lax.broadcasted_iota(jnp.int32, (tq, tk), 0) - lax.broadcasted_iota(jnp.int32, (tq, tk), 1))
    ones = jnp.ones((tk, HEAD_DIM), BF16)
    jd = t0 // tk

    def flash_tile(jt, k_ref, v_ref, m_ref, acc_ref, windowed):
        start = pl.multiple_of(jt * tk, tk)
        d0i = i - jt * (tk // tq)
        dist = d0i * tq + diff
        if windowed:
            mask = (dist >= 0) & (dist < WINDOW)
        else:
            mask = (dist >= 0) & (keymask[jt] > 0.5)
        bias = tbl_ref[0, jnp.minimum(d0i, BIAS_CONST_IDX)]
        s = lax.dot_general(q4, k_ref[0, pl.ds(start, tk), :], _NT, preferred_element_type=F32) * SCALE + bias
        s = per_head(s, mask)
        m_old = m_ref[...]
        m_new = jnp.maximum(m_old, jnp.max(s, axis=-1, keepdims=True))
        p = jnp.exp(s - m_new)
        v_aug = jnp.concatenate([v_ref[0, pl.ds(start, tk), :], ones], axis=1)
        acc_ref[...] = jnp.exp(m_old - m_new) * acc_ref[...] + jnp.dot(p.astype(BF16), v_aug,
                                                                        preferred_element_type=F32)
        m_ref[...] = m_new

    for m_ref, acc_ref in ((m_s, acc_s), (m_w, acc_w)):
        m_ref[...] = jnp.full(m_ref.shape, MASK_VALUE, F32)
        acc_ref[...] = jnp.zeros_like(acc_ref)

    def sel_body(it, carry):
        flash_tile(jd - it, ks_ref, vs_ref, m_s, acc_s, False)
        return carry
    lax.fori_loop(0, jd + 1, sel_body, 0)

    n_win = (WINDOW + tq - 1) // tk + 1

    def win_body(it, carry):
        flash_tile(jd - it, kw_ref, vw_ref, m_w, acc_w, True)
        return carry
    lax.fori_loop(0, jnp.minimum(jd + 1, n_win), win_body, 0)

    gate = 1.0 / (1.0 + jnp.exp(-gate_ref[0]))
    for r in range(NSA_GROUP_SIZE):
        rs = slice(r * tq, (r + 1) * tq)
        o_s = acc_s[rs, 0:HEAD_DIM] / acc_s[rs, HEAD_DIM:HEAD_DIM + 1]
        o_w = acc_w[rs, 0:HEAD_DIM] / acc_w[rs, HEAD_DIM:HEAD_DIM + 1]
        c = r * N_GATES
        out = gate[:, c:c + 1] * o_c[rs] + gate[:, c + 1:c + 2] * o_s + gate[:, c + 2:c + 3] * o_w
        o_ref[0, :, r * HEAD_DIM:(r + 1) * HEAD_DIM] = out


def _nsa_attention(proj_a, proj_b, kcmp, vcmp, tbl, tblc, ovt, expand, batch, seq, cols):
    tq, tk, rows = NSA_TQ, NSA_TK, NSA_ROWS
    nbp = kcmp.shape[2]
    n_tiles = tbl.shape[1]
    gw = NSA_GROUP_SIZE * HEAD_DIM
    qblk = cols["n_q"] // gw
    kvblk = {name: cols[name] // HEAD_DIM for name in ("ks", "vs", "kw", "vw")}
    gblk = cols["gates"] // V7X_LANES

    def kv_spec(name):
        blk = kvblk[name]
        return pl.BlockSpec((1, seq, HEAD_DIM), lambda b, g, i: (b, 0, blk + g))

    cmp_spec = pl.BlockSpec((1, 1, nbp, HEAD_DIM), lambda b, g, i: (b, g, 0, 0))
    vmem = (2 * 4 * seq * HEAD_DIM * 2 + 2 * n_tiles * rows * tk * 4 + 2 * rows * nbp * 4
            + 2 * V7X_LANES * seq * 2 + tq * seq * 4 + 2 * rows * (tk + V7X_LANES) * 4
            + 16 * rows * tk * 4 + (8 << 20))
    return pl.pallas_call(
        _nsa_kernel,
        grid=(batch, NSA_KV_GROUPS, seq // tq),
        in_specs=[pl.BlockSpec((1, tq, gw), lambda b, g, i: (b, i, qblk + g)),
                  cmp_spec, cmp_spec,
                  kv_spec("ks"), kv_spec("vs"), kv_spec("kw"), kv_spec("vw"),
                  pl.BlockSpec((1, tq, V7X_LANES), lambda b, g, i: (b, i, gblk + g)),
                  pl.BlockSpec((1, n_tiles, rows, tk), lambda b, g, i: (g, 0, 0, 0)),
                  pl.BlockSpec((1, 1, rows, nbp), lambda b, g, i: (g, i, 0, 0)),
                  pl.BlockSpec((V7X_LANES, nbp), lambda b, g, i: (0, 0)),
                  pl.BlockSpec((V7X_LANES, seq), lambda b, g, i: (0, 0))],
        out_specs=pl.BlockSpec((1, tq, gw), lambda b, g, i: (b, i, g)),
        out_shape=jax.ShapeDtypeStruct((batch, seq, NSA_WIDTH), F32),
        scratch_shapes=[pltpu.VMEM((rows, 2 * HEAD_DIM), F32), pltpu.VMEM((rows, 1), F32),
                        pltpu.VMEM((rows, 2 * HEAD_DIM), F32), pltpu.VMEM((rows, 1), F32),
                        pltpu.VMEM((seq // tk, tq, tk), F32)],
        compiler_params=_cparams(("parallel", "parallel", "arbitrary"), vmem),
        name="nsa_attention",
    )(proj_a, kcmp, vcmp, proj_a, proj_a, proj_a, proj_a, proj_b, tbl, tblc, ovt, expand)


def _out_kernel(osb_ref, zsb_ref, onsa_ref, znsa_ref, x_ref, gsb_ref, gnsa_ref, gfin_ref, wsb_ref, wnsa_ref,
                o_ref):
    def gated(o_r, z_r, g_r):
        o = o_r[...]
        y = o * lax.rsqrt(jnp.mean(o * o, axis=-1, keepdims=True) + RMS_EPS) * g_r[...]
        z = z_r[...]
        return (y * (z * (1.0 / (1.0 + jnp.exp(-z))))).astype(BF16)

    mixed = (jnp.dot(gated(osb_ref, zsb_ref, gsb_ref), wsb_ref[...], preferred_element_type=F32)
             + jnp.dot(gated(onsa_ref, znsa_ref, gnsa_ref), wnsa_ref[...], preferred_element_type=F32))
    h = x_ref[...] + mixed
    o_ref[...] = h * lax.rsqrt(jnp.mean(h * h, axis=-1, keepdims=True) + RMS_EPS) * gfin_ref[...]


def _output_stage(o_sb, o_nsa, proj_b, x2d, g_sb, g_nsa, g_fin, w_out, tm):
    m, d = x2d.shape
    half = SB_WIDTH
    assert m % tm == 0
    vmem = 2 * (4 * tm * half * 4 + 2 * tm * d * 4) + 2 * 2 * half * d * 2 + 6 * tm * d * 4 + (4 << 20)
    row = lambda c: pl.BlockSpec((tm, half), lambda i: (i, c))
    vec = lambda n: pl.BlockSpec((1, n), lambda i: (0, 0))
    return pl.pallas_call(
        _out_kernel,
        grid=(m // tm,),
        in_specs=[row(0), row(0), row(0), row(1),
                  pl.BlockSpec((tm, d), lambda i: (i, 0)),
                  vec(half), vec(half), vec(d),
                  pl.BlockSpec((half, d), lambda i: (0, 0)),
                  pl.BlockSpec((half, d), lambda i: (1, 0))],
        out_specs=pl.BlockSpec((tm, d), lambda i: (i, 0)),
        out_shape=jax.ShapeDtypeStruct((m, d), F32),
        compiler_params=_cparams(("parallel",), vmem),
        name="output_stage",
    )(o_sb, proj_b, o_nsa, proj_b, x2d, g_sb, g_nsa, g_fin, w_out, w_out)


def _selection_constants(seq, nbp):
    nb = seq // CMP_STRIDE - 1
    nsel = seq // SEL_BLOCK
    assert nsel <= V7X_LANES and nbp >= nb
    ci = np.arange(nbp)[None, :] * CMP_STRIDE
    sj = np.arange(V7X_LANES)[:, None] * SEL_BLOCK
    ovt = ((ci < sj + SEL_BLOCK) & (ci + CMP_BLOCK > sj) & (np.arange(nbp)[None, :] < nb)
           & (np.arange(V7X_LANES)[:, None] < nsel))
    expand = (np.arange(seq)[None, :] // SEL_BLOCK) == np.arange(V7X_LANES)[:, None]
    return jnp.asarray(ovt, BF16), jnp.asarray(expand, BF16)


def kernel(x, norm_in, w_in, cmp_k_pos, cmp_k_w1, cmp_k_w2, cmp_v_pos, cmp_v_w1, cmp_v_w2,
           rel_bias, norm_sb, norm_nsa, w_out, norm_final):
    batch, seq, d_model = x.shape
    assert w_in.shape[0] == 1, "single-layer trunk: the final norm is fused into the output stage"
    m = batch * seq
    nbp = seq // CMP_STRIDE
    assert seq % NSA_TK == 0 and seq % SB_TILE == 0 and nbp % V7X_LANES == 0

    sizes = (SB_WIDTH,) * 4 + (NSA_WIDTH,) + (NSA_KV_WIDTH,) * 6 + (N_GATES * NSA_HEADS, NSA_WIDTH)
    off = np.concatenate([[0], np.cumsum(sizes)])
    (c_sbq, c_sbk, c_sbv, c_sbz, c_nq, c_kc, c_vc, c_ks, c_vs, c_kw, c_vw, c_gate, c_nz, c_end) = [int(o) for o in off]
    cols = {"n_q": 3 * SB_WIDTH, "ks": 3 * SB_WIDTH + NSA_WIDTH}
    cols["vs"] = cols["ks"] + NSA_KV_WIDTH
    cols["kw"] = cols["vs"] + NSA_KV_WIDTH
    cols["vw"] = cols["kw"] + NSA_KV_WIDTH
    cols["gates"] = 2 * SB_WIDTH
    gates_per_group = N_GATES * NSA_GROUP_SIZE

    tbl = _bias_table(rel_bias, BIAS_CONST_IDX + 1, NSA_TK, 1, 0, BIAS_CONST_IDX)
    tblc = _bias_table(rel_bias, seq // NSA_TQ, nbp, CMP_STRIDE, -(CMP_BLOCK - 1), None)
    ovt, expand = _selection_constants(seq, nbp)

    h2d = x.reshape(m, d_model)
    for layer in range(1):
        w = w_in[layer]
        w_a = jnp.concatenate([w[:, c_sbq:c_sbz], w[:, c_nq:c_kc], w[:, c_ks:c_gate]], axis=1).astype(BF16)
        w_cv = w[:, c_kc:c_ks].astype(BF16)
        gate_pad = jnp.zeros((d_model, V7X_LANES - gates_per_group), w.dtype)
        w_b = jnp.concatenate(
            [w[:, c_sbz:c_nq], w[:, c_nz:c_end]]
            + [blk for g in range(NSA_KV_GROUPS)
               for blk in (w[:, c_gate + g * gates_per_group:c_gate + (g + 1) * gates_per_group], gate_pad)],
            axis=1).astype(BF16)
        g_in = norm_in[layer].reshape(1, d_model)

        proj_a = _rms_proj(h2d, g_in, w_a, BF16, 1024, w_a.shape[1] // 4, "proj_attn")
        proj_cv = _rms_proj(h2d, g_in, w_cv, BF16, 1024, w_cv.shape[1], "proj_cmp")
        proj_b = _rms_proj(h2d, g_in, w_b, F32, 512, w_b.shape[1], "proj_gate")

        proj_a3 = proj_a.reshape(batch, seq, -1)
        o_sb = _sb_attention(proj_a3, batch, seq)

        half = CMP_STRIDE * HEAD_DIM
        pos = jnp.stack([cmp_k_pos[layer], cmp_v_pos[layer]]).reshape(2, 2, 1, half)
        w1 = jnp.stack([cmp_k_w1[layer], cmp_v_w1[layer]]).reshape(2, 2, half, CMP_HIDDEN).astype(BF16)
        w2 = jnp.stack([cmp_k_w2[layer], cmp_v_w2[layer]]).astype(BF16)
        kcmp, vcmp = _compress(proj_cv.reshape(batch, nbp, -1), pos, w1, w2, batch, nbp)

        o_nsa = _nsa_attention(proj_a3, proj_b.reshape(batch, seq, -1), kcmp, vcmp, tbl, tblc, ovt, expand,
                               batch, seq, cols)

        h2d = _output_stage(o_sb.reshape(m, SB_WIDTH), o_nsa.reshape(m, NSA_WIDTH), proj_b, h2d,
                            norm_sb[layer].reshape(1, SB_WIDTH), norm_nsa[layer].reshape(1, NSA_WIDTH),
                            norm_final.reshape(1, d_model),
                            w_out[layer].astype(BF16), 256)
    return h2d.reshape(batch, seq, d_model)
```

```python
import functools
import math

import numpy as np
import jax
import jax.numpy as jnp
from jax import lax
from jax.experimental import pallas as pl
from jax.experimental.pallas import tpu as pltpu

F32 = jnp.float32
BF16 = jnp.bfloat16

HEAD_DIM = 128
SB_HEADS = 8
NSA_HEADS = 8
NSA_KV_GROUPS = 2
NSA_GROUP_SIZE = NSA_HEADS // NSA_KV_GROUPS
SB_WIDTH = SB_HEADS * HEAD_DIM
NSA_WIDTH = NSA_HEADS * HEAD_DIM
NSA_KV_WIDTH = NSA_KV_GROUPS * HEAD_DIM
N_GATES = 3
CMP_STRIDE = 16
CMP_BLOCK = 2 * CMP_STRIDE
CMP_HIDDEN = 256
SEL_BLOCK = 64
SEL_TOP_N = 16
WINDOW = 512
REL_BUCKETS = 32
REL_MAX_EXACT = 16
REL_MAX_DISTANCE = 1024
RMS_EPS = 1e-6
SEL_FORCE = 1e9
MASK_VALUE = -1e30
SCALE = 1.0 / math.sqrt(HEAD_DIM)
LOG2E = math.log2(math.e)

V7X_LANES = 128
V7X_MXU_DIM = 256
V7X_BF16_SUBLANES = 16
V7X_VMEM_BYTES = 64 * 1024 * 1024

SB_TQ = 4 * V7X_MXU_DIM
SB_TK = V7X_MXU_DIM
SB_HEADS_PER_STEP = 4
NSA_TQ = V7X_MXU_DIM
NSA_TK = V7X_MXU_DIM
NSA_COLS = NSA_GROUP_SIZE * NSA_TQ

_NT = (((1,), (1,)), ((), ()))


def _bucket_thresholds():
    n = np.arange(0, 4 * REL_MAX_DISTANCE)
    nf = np.maximum(n, 1).astype(np.float64)
    large = REL_MAX_EXACT + (np.log(nf / REL_MAX_EXACT) / math.log(REL_MAX_DISTANCE / REL_MAX_EXACT)
                             * (REL_BUCKETS - REL_MAX_EXACT)).astype(np.int64)
    bucket = np.where(n < REL_MAX_EXACT, n, np.minimum(large, REL_BUCKETS - 1))
    assert np.all(np.diff(bucket) >= 0) and bucket[-1] == REL_BUCKETS - 1
    return [int(n[bucket >= k][0]) for k in range(1, REL_BUCKETS)]


BUCKET_THR = _bucket_thresholds()
LAST_THR = BUCKET_THR[-1]
BIAS_CONST_IDX = -(-(LAST_THR + NSA_TK - 1) // NSA_TQ)
BIAS_FAR = 1 << 20


def _cparams(sem, vmem_bytes):
    return pltpu.CompilerParams(dimension_semantics=sem, vmem_limit_bytes=int(min(vmem_bytes, V7X_VMEM_BYTES)))


def _bias_table_kernel(rb_ref, o_ref, *, stride, offset, const_idx):
    h = pl.program_id(0)
    tile = pl.program_id(1)
    rows, cols = o_ref.shape[2], o_ref.shape[3]
    base = tile * NSA_TQ + offset
    if const_idx is not None:
        base = jnp.where(tile >= const_idx, BIAS_FAR, base)
    cj = lax.broadcasted_iota(jnp.int32, (rows, cols), 0)
    ti = lax.broadcasted_iota(jnp.int32, (rows, cols), 1)
    dist = base + ti - stride * cj
    val = jnp.full((rows, cols), rb_ref[0, h], F32)
    for k in range(1, REL_BUCKETS):
        val = jnp.where(dist >= BUCKET_THR[k - 1], rb_ref[k, h], val)
    o_ref[0, 0] = val


def _bias_table(rel_bias, n_tiles, rows, stride, offset, const_idx):
    kern = functools.partial(_bias_table_kernel, stride=stride, offset=offset, const_idx=const_idx)
    return pl.pallas_call(
        kern,
        grid=(NSA_HEADS, n_tiles),
        in_specs=[pl.BlockSpec(memory_space=pltpu.SMEM)],
        out_specs=pl.BlockSpec((1, 1, rows, NSA_TQ),
                               lambda h, t: (h // NSA_GROUP_SIZE, t, 0, h % NSA_GROUP_SIZE)),
        out_shape=jax.ShapeDtypeStruct((NSA_KV_GROUPS, n_tiles, rows, NSA_COLS), F32),
        compiler_params=_cparams(("parallel", "parallel"), 8 << 20),
        name="bias_table",
    )(rel_bias)


def _proj_kernel(x_ref, g_ref, w_ref, o_ref, xn_ref, *, chunk):
    @pl.when(pl.program_id(1) == 0)
    def _normalise():
        def body(c, carry):
            rows = pl.ds(pl.multiple_of(c * chunk, chunk), chunk)
            x = x_ref[rows, :]
            ms = jnp.mean(x * x, axis=-1, keepdims=True)
            xn_ref[rows, :] = (x * lax.rsqrt(ms + RMS_EPS) * g_ref[...]).astype(BF16)
            return carry
        lax.fori_loop(0, x_ref.shape[0] // chunk, body, 0)

    o_ref[...] = jnp.dot(xn_ref[...], w_ref[...], preferred_element_type=F32).astype(o_ref.dtype)


def _rms_proj(x2d, g, w, out_dtype, tm, tn, name):
    m, d = x2d.shape
    n = w.shape[1]
    assert m % tm == 0 and n % tn == 0
    osz = jnp.dtype(out_dtype).itemsize
    vmem = 2 * tm * d * 4 + tm * d * 2 + 2 * d * tn * 2 + 2 * tm * tn * osz + tm * tn * 4 + (4 << 20)
    return pl.pallas_call(
        functools.partial(_proj_kernel, chunk=min(tm, 128)),
        grid=(m // tm, n // tn),
        in_specs=[pl.BlockSpec((tm, d), lambda i, j: (i, 0)),
                  pl.BlockSpec((1, d), lambda i, j: (0, 0)),
                  pl.BlockSpec((d, tn), lambda i, j: (0, j))],
        out_specs=pl.BlockSpec((tm, tn), lambda i, j: (i, j)),
        out_shape=jax.ShapeDtypeStruct((m, n), out_dtype),
        scratch_shapes=[pltpu.VMEM((tm, d), BF16)],
        compiler_params=_cparams(("parallel", "arbitrary"), vmem),
        name=name,
    )(x2d, g, w)


def _sb_kernel(q_ref, k_ref, v_ref, o_ref, acc_ref, run_ref):
    i = pl.program_id(2)
    tq, tk = q_ref.shape[1], SB_TK
    n_diag = tq // tk
    row = lax.broadcasted_iota(jnp.int32, (tq, tk), 0)
    col = lax.broadcasted_iota(jnp.int32, (tq, tk), 1)
    below = row > col
    suffix = jnp.where(below[:tk], 1.0, 0.0).astype(BF16)
    suffix2 = jnp.concatenate([suffix, suffix], axis=0)

    def tile(j, row0, diag):
        start = pl.multiple_of(j * tk, tk)
        rows = slice(row0, tq)
        causal = below[:tq - row0]
        for hh in range(SB_HEADS_PER_STEP):
            lanes = slice(hh * HEAD_DIM, (hh + 1) * HEAD_DIM)
            q = q_ref[0, rows, lanes]
            k = k_ref[0, pl.ds(start, tk), lanes]
            v = v_ref[0, pl.ds(start, tk), lanes]
            z = lax.dot_general(q, k, _NT, preferred_element_type=F32) * SCALE
            sp = jnp.log(1.0 + jnp.exp2(jnp.abs(z) * (-LOG2E)))
            neg_rest = jnp.maximum(z, 0.0) + sp
            if diag:
                neg_rest = jnp.where(causal, neg_rest, 0.0)
            hi = neg_rest.astype(BF16)
            lo = (neg_rest - hi.astype(F32)).astype(BF16)
            later = jnp.dot(jnp.concatenate([hi, lo], axis=1), suffix2, preferred_element_type=F32)
            a = jnp.exp((z - neg_rest) - (later + run_ref[hh, rows]))
            if diag:
                a = jnp.where(causal, a, 0.0)
            acc_ref[hh, rows] += jnp.dot(a.astype(BF16), v, preferred_element_type=F32)
            run_ref[hh, rows] += jnp.sum(neg_rest, axis=-1, keepdims=True)

    acc_ref[...] = jnp.zeros_like(acc_ref)
    run_ref[...] = jnp.zeros_like(run_ref)
    for c in reversed(range(n_diag)):
        tile(i * n_diag + c, c * tk, True)

    def body(jj, carry):
        tile(i * n_diag - 1 - jj, 0, False)
        return carry
    lax.fori_loop(0, i * n_diag, body, 0)
    for hh in range(SB_HEADS_PER_STEP):
        o_ref[0, :, hh * HEAD_DIM:(hh + 1) * HEAD_DIM] = acc_ref[hh]


def _sb_attention(proj_a, batch, seq):
    tq = min(SB_TQ, seq)
    hps = SB_HEADS_PER_STEP
    width = hps * HEAD_DIM
    assert seq % tq == 0 and tq % SB_TK == 0 and SB_HEADS % hps == 0
    kblk = SB_WIDTH // width
    vmem = 2 * 2 * seq * width * 2 + 6 * tq * width * 4 + hps * 12 * tq * SB_TK * 4 + (4 << 20)
    return pl.pallas_call(
        _sb_kernel,
        grid=(batch, SB_HEADS // hps, seq // tq),
        in_specs=[pl.BlockSpec((1, tq, width), lambda b, h, i: (b, i, h)),
                  pl.BlockSpec((1, seq, width), lambda b, h, i: (b, 0, kblk + h)),
                  pl.BlockSpec((1, seq, width), lambda b, h, i: (b, 0, 2 * kblk + h))],
        out_specs=pl.BlockSpec((1, tq, width), lambda b, h, i: (b, i, h)),
        out_shape=jax.ShapeDtypeStruct((batch, seq, SB_WIDTH), F32),
        scratch_shapes=[pltpu.VMEM((hps, tq, HEAD_DIM), F32), pltpu.VMEM((hps, tq, 1), F32)],
        compiler_params=_cparams(("parallel", "parallel", "arbitrary"), vmem),
        name="sb_attention",
    )(proj_a, proj_a, proj_a)


def _compress_kernel(cv_ref, pos_ref, w1_ref, w2_ref, w2t_ref, ko_ref, vo_ref):
    nbp = cv_ref.shape[1]
    tok = 2 * NSA_KV_WIDTH
    for which, out_ref in ((0, ko_ref), (1, vo_ref)):
        for g in range(NSA_KV_GROUPS):
            c0 = (which * NSA_KV_GROUPS + g) * HEAD_DIM
            chunks = jnp.concatenate(
                [cv_ref[0, :, l * tok + c0:l * tok + c0 + HEAD_DIM] for l in range(CMP_STRIDE)],
                axis=1).astype(F32)
            first = jnp.dot((chunks + pos_ref[which, 0]).astype(BF16), w1_ref[which, 0],
                            preferred_element_type=F32)
            second = jnp.dot((chunks + pos_ref[which, 1]).astype(BF16), w1_ref[which, 1],
                             preferred_element_type=F32)
            hid = jax.nn.gelu(first + pltpu.roll(second, nbp - 1, 0)).astype(BF16)
            if which == 0:
                out = jnp.dot(hid, w2_ref[which], preferred_element_type=F32)
            else:
                out = lax.dot_general(w2t_ref[which], hid, _NT, preferred_element_type=F32)
            out_ref[0, g] = out.astype(BF16)


def _compress(cv, pos, w1, w2, batch, nbp):
    tokw = CMP_STRIDE * 2 * NSA_KV_WIDTH
    half = CMP_STRIDE * HEAD_DIM
    vmem = 2 * nbp * tokw * 2 + 2 * 4 * half * CMP_HIDDEN * 2 + 8 * nbp * half * 4 + (8 << 20)
    out_sds = [jax.ShapeDtypeStruct((batch, NSA_KV_GROUPS, nbp, HEAD_DIM), BF16),
               jax.ShapeDtypeStruct((batch, NSA_KV_GROUPS, HEAD_DIM, nbp), BF16)]
    out_spec = [pl.BlockSpec((1, NSA_KV_GROUPS, nbp, HEAD_DIM), lambda b: (b, 0, 0, 0)),
                pl.BlockSpec((1, NSA_KV_GROUPS, HEAD_DIM, nbp), lambda b: (b, 0, 0, 0))]
    return pl.pallas_call(
        _compress_kernel,
        grid=(batch,),
        in_specs=[pl.BlockSpec((1, nbp, tokw), lambda b: (b, 0, 0)),
                  pl.BlockSpec((2, 2, 1, half), lambda b: (0, 0, 0, 0)),
                  pl.BlockSpec((2, 2, half, CMP_HIDDEN), lambda b: (0, 0, 0, 0)),
                  pl.BlockSpec((2, CMP_HIDDEN, HEAD_DIM), lambda b: (0, 0, 0)),
                  pl.BlockSpec((2, HEAD_DIM, CMP_HIDDEN), lambda b: (0, 0, 0))],
        out_specs=out_spec,
        out_shape=out_sds,
        compiler_params=_cparams(("parallel",), vmem),
        name="kv_compress",
    )(cv, pos, w1, w2, jnp.swapaxes(w2, 1, 2))


def _nsa_kernel(q_ref, kc_ref, vct_ref, ks_ref, vst_ref, kw_ref, vwt_ref, gate_ref, tbl_ref, tblc_ref,
                ovt_ref, expt_ref, o_ref, acc_s, m_s, acc_w, m_w, keymask):
    i = pl.program_id(2)
    tq, tk, cols = NSA_TQ, NSA_TK, NSA_COLS
    nbp = kc_ref.shape[2]
    seq = expt_ref.shape[0]
    nsel = seq // SEL_BLOCK
    t0 = i * tq
    qb = q_ref[0]
    q4 = jnp.concatenate([qb[:, r * HEAD_DIM:(r + 1) * HEAD_DIM] for r in range(NSA_GROUP_SIZE)], axis=0)

    def lane_query(shape):
        return lax.broadcasted_iota(jnp.int32, shape, 1) & (tq - 1)

    n_c = lax.broadcasted_iota(jnp.int32, (nbp, cols), 0)
    mask_c = (n_c * CMP_STRIDE + (CMP_BLOCK - 1)) <= (t0 + lane_query((nbp, cols)))
    s_c = lax.dot_general(kc_ref[0, 0], q4, _NT, preferred_element_type=F32) * SCALE + tblc_ref[0, 0]
    s_c = jnp.where(mask_c, s_c, MASK_VALUE)
    e_c = jnp.where(mask_c, jnp.exp(s_c - jnp.max(s_c, axis=0, keepdims=True)), 0.0)
    l_c = jnp.sum(e_c, axis=0, keepdims=True)
    p_c = e_c / jnp.where(l_c > 0.0, l_c, 1.0)
    o_c = jnp.dot(vct_ref[0, 0], p_c.astype(BF16), preferred_element_type=F32)

    p_sum = p_c[:, 0:tq]
    for r in range(1, NSA_GROUP_SIZE):
        p_sum = p_sum + p_c[:, r * tq:(r + 1) * tq]
    p_hi = p_sum.astype(BF16)
    p_lo = (p_sum - p_hi.astype(F32)).astype(BF16)
    ovt = ovt_ref[...]
    imp = (jnp.dot(ovt, p_hi, preferred_element_type=F32)
           + jnp.dot(ovt, p_lo, preferred_element_type=F32))[:nsel]
    jrow = lax.broadcasted_iota(jnp.int32, (nsel, tq), 0)
    tcol = t0 + lax.broadcasted_iota(jnp.int32, (nsel, tq), 1)
    cur = tcol // SEL_BLOCK
    valid = jrow * SEL_BLOCK <= tcol
    forced = (jrow == 0) | (jrow == cur) | (jrow == cur - 1)
    score = jnp.where(valid, jnp.where(forced, SEL_FORCE, imp), -SEL_FORCE)
    rank = jnp.zeros((nsel, tq), F32)
    for other in range(nsel):
        srow = score[other:other + 1, :]
        before = (srow > score) | ((srow >= score) & (jrow > other))
        rank = rank + jnp.where(before, 1.0, 0.0)
    sel = jnp.where(rank < float(min(SEL_TOP_N, nsel)), 1.0, 0.0)
    if nsel < V7X_LANES:
        sel = jnp.concatenate([sel, jnp.zeros((V7X_LANES - nsel, tq), F32)], axis=0)
    key_sel = jnp.dot(expt_ref[...], sel.astype(BF16), preferred_element_type=F32)
    for jt in range(seq // tk):
        keymask[jt] = key_sel[jt * tk:(jt + 1) * tk]

    diff = lane_query((tk, cols)) - lax.broadcasted_iota(jnp.int32, (tk, cols), 0)
    ones = jnp.ones((V7X_BF16_SUBLANES, tk), BF16)
    jd = t0 // tk
    selected = (ks_ref, vst_ref, m_s, acc_s, False)
    window = (kw_ref, vwt_ref, m_w, acc_w, True)

    def flash_tile(jt, chains):
        start = pl.multiple_of(jt * tk, tk)
        d0i = i - jt * (tk // tq)
        dist = d0i * tq + diff
        causal = dist >= 0
        bias = tbl_ref[0, jnp.minimum(d0i, BIAS_CONST_IDX)]
        for k_ref, vt_ref, m_ref, acc_ref, windowed in chains:
            if windowed:
                mask = causal & (dist < WINDOW)
            else:
                km = keymask[jt]
                mask = causal & (jnp.concatenate([km] * NSA_GROUP_SIZE, axis=1) > 0.5)
            s = lax.dot_general(k_ref[0, pl.ds(start, tk), :], q4, _NT, preferred_element_type=F32) * SCALE + bias
            s = jnp.where(mask, s, MASK_VALUE)
            m_old = m_ref[...]
            m_new = jnp.maximum(m_old, jnp.max(s, axis=0, keepdims=True))
            p = jnp.exp(s - m_new)
            v_aug = jnp.concatenate([vt_ref[0, 0, jt], ones], axis=0)
            acc_ref[...] = jnp.exp(m_old - m_new) * acc_ref[...] + jnp.dot(v_aug, p.astype(BF16),
                                                                            preferred_element_type=F32)
            m_ref[...] = m_new

    for m_ref, acc_ref in ((m_s, acc_s), (m_w, acc_w)):
        m_ref[...] = jnp.full(m_ref.shape, MASK_VALUE, F32)
        acc_ref[...] = jnp.zeros_like(acc_ref)

    n_win = (WINDOW + tq - 1) // tk + 1

    def both_body(it, carry):
        flash_tile(jd - it, (selected, window))
        return carry
    lax.fori_loop(0, jnp.minimum(jd + 1, n_win), both_body, 0)

    def sel_body(it, carry):
        flash_tile(jd - it, (selected,))
        return carry
    lax.fori_loop(n_win, jd + 1, sel_body, 0)

    gate = (1.0 / (1.0 + jnp.exp(-gate_ref[0]))).T
    for r in range(NSA_GROUP_SIZE):
        cs = slice(r * tq, (r + 1) * tq)
        o_s = acc_s[0:HEAD_DIM, cs] / acc_s[HEAD_DIM:HEAD_DIM + 1, cs]
        o_w = acc_w[0:HEAD_DIM, cs] / acc_w[HEAD_DIM:HEAD_DIM + 1, cs]
        c = r * N_GATES
        out = gate[c:c + 1] * o_c[:, cs] + gate[c + 1:c + 2] * o_s + gate[c + 2:c + 3] * o_w
        o_ref[0, :, r * HEAD_DIM:(r + 1) * HEAD_DIM] = out.T


def _nsa_attention(proj_a, proj_b, v_t, kcmp, vcmp_t, tbl, tblc, ovt, expt, batch, seq, cols):
    tq, tk, ncols = NSA_TQ, NSA_TK, NSA_COLS
    nbp = kcmp.shape[2]
    n_tiles = tbl.shape[1]
    gw = NSA_GROUP_SIZE * HEAD_DIM
    qblk = cols["n_q"] // gw
    ksblk, kwblk = cols["ks"] // HEAD_DIM, cols["kw"] // HEAD_DIM
    gblk = cols["gates"] // V7X_LANES
    acc_rows = HEAD_DIM + V7X_BF16_SUBLANES

    def k_spec(blk):
        return pl.BlockSpec((1, seq, HEAD_DIM), lambda b, g, i: (b, 0, blk + g))

    def vt_spec(first):
        return pl.BlockSpec((1, 1, seq // tk, HEAD_DIM, tk), lambda b, g, i: (b, first + g, 0, 0, 0))

    vmem = (2 * 4 * seq * HEAD_DIM * 2 + 2 * n_tiles * tk * ncols * 4 + 2 * nbp * ncols * 4
            + 2 * seq * V7X_LANES * 2 + seq * tq * 4 + 2 * acc_rows * ncols * 4
            + 20 * tk * ncols * 4 + (8 << 20))
    return pl.pallas_call(
        _nsa_kernel,
        grid=(batch, NSA_KV_GROUPS, seq // tq),
        in_specs=[pl.BlockSpec((1, tq, gw), lambda b, g, i: (b, i, qblk + g)),
                  pl.BlockSpec((1, 1, nbp, HEAD_DIM), lambda b, g, i: (b, g, 0, 0)),
                  pl.BlockSpec((1, 1, HEAD_DIM, nbp), lambda b, g, i: (b, g, 0, 0)),
                  k_spec(ksblk), vt_spec(0), k_spec(kwblk), vt_spec(NSA_KV_GROUPS),
                  pl.BlockSpec((1, tq, V7X_LANES), lambda b, g, i: (b, i, gblk + g)),
                  pl.BlockSpec((1, n_tiles, tk, ncols), lambda b, g, i: (g, 0, 0, 0)),
                  pl.BlockSpec((1, 1, nbp, ncols), lambda b, g, i: (g, i, 0, 0)),
                  pl.BlockSpec((V7X_LANES, nbp), lambda b, g, i: (0, 0)),
                  pl.BlockSpec((seq, V7X_LANES), lambda b, g, i: (0, 0))],
        out_specs=pl.BlockSpec((1, tq, gw), lambda b, g, i: (b, i, g)),
        out_shape=jax.ShapeDtypeStruct((batch, seq, NSA_WIDTH), F32),
        scratch_shapes=[pltpu.VMEM((acc_rows, ncols), F32), pltpu.VMEM((1, ncols), F32),
                        pltpu.VMEM((acc_rows, ncols), F32), pltpu.VMEM((1, ncols), F32),
                        pltpu.VMEM((seq // tk, tk, tq), F32)],
        compiler_params=_cparams(("parallel", "parallel", "arbitrary"), vmem),
        name="nsa_attention",
    )(proj_a, kcmp, vcmp_t, proj_a, v_t, proj_a, v_t, proj_b, tbl, tblc, ovt, expt)


def _selection_constants(seq, nbp):
    nb = seq // CMP_STRIDE - 1
    nsel = seq // SEL_BLOCK
    assert nsel <= V7X_LANES and nbp >= nb
    ci = np.arange(nbp)[None, :] * CMP_STRIDE
    sj = np.arange(V7X_LANES)[:, None] * SEL_BLOCK
    ovt = ((ci < sj + SEL_BLOCK) & (ci + CMP_BLOCK > sj) & (np.arange(nbp)[None, :] < nb)
           & (np.arange(V7X_LANES)[:, None] < nsel))
    expt = (np.arange(seq)[:, None] // SEL_BLOCK) == np.arange(V7X_LANES)[None, :]
    return jnp.asarray(ovt, BF16), jnp.asarray(expt, BF16)


def _nsa_branch(proj_a3, proj_b3, proj_cv3, cmp_pos, cmp_w1, cmp_w2, rel_bias, cols):
    batch, seq, _ = proj_a3.shape
    nbp = seq // CMP_STRIDE
    half = CMP_STRIDE * HEAD_DIM
    pos = jnp.stack(cmp_pos).reshape(2, 2, 1, half)
    w1 = jnp.stack(cmp_w1).reshape(2, 2, half, CMP_HIDDEN).astype(BF16)
    w2 = jnp.stack(cmp_w2).astype(BF16)
    kcmp, vcmp_t = _compress(proj_cv3, pos, w1, w2, batch, nbp)

    tbl = _bias_table(rel_bias, BIAS_CONST_IDX + 1, NSA_TK, 1, 0, BIAS_CONST_IDX)
    tblc = _bias_table(rel_bias, seq // NSA_TQ, nbp, CMP_STRIDE, -(CMP_BLOCK - 1), None)
    ovt, expt = _selection_constants(seq, nbp)

    vals = jnp.concatenate([proj_a3[:, :, cols["vs"]:cols["vs"] + NSA_KV_WIDTH],
                            proj_a3[:, :, cols["vw"]:cols["vw"] + NSA_KV_WIDTH]], axis=-1)
    v_t = vals.reshape(batch, seq // NSA_TK, NSA_TK, 2 * NSA_KV_GROUPS, HEAD_DIM).transpose(0, 3, 1, 4, 2)
    return _nsa_attention(proj_a3, proj_b3, v_t, kcmp, vcmp_t, tbl, tblc, ovt, expt, batch, seq, cols)


def _out_kernel(osb_ref, zsb_ref, onsa_ref, znsa_ref, x_ref, gsb_ref, gnsa_ref, gfin_ref, wsb_ref, wnsa_ref,
                o_ref):
    def gated(o_r, z_r, g_r):
        o = o_r[...]
        y = o * lax.rsqrt(jnp.mean(o * o, axis=-1, keepdims=True) + RMS_EPS) * g_r[...]
        z = z_r[...]
        return (y * (z * (1.0 / (1.0 + jnp.exp(-z))))).astype(BF16)

    mixed = (jnp.dot(gated(osb_ref, zsb_ref, gsb_ref), wsb_ref[...], preferred_element_type=F32)
             + jnp.dot(gated(onsa_ref, znsa_ref, gnsa_ref), wnsa_ref[...], preferred_element_type=F32))
    h = x_ref[...] + mixed
    o_ref[...] = h * lax.rsqrt(jnp.mean(h * h, axis=-1, keepdims=True) + RMS_EPS) * gfin_ref[...]


def _output_stage(o_sb, o_nsa, proj_b, x2d, g_sb, g_nsa, g_fin, w_out, tm):
    m, d = x2d.shape
    half = SB_WIDTH
    assert m % tm == 0
    vmem = 2 * (4 * tm * half * 4 + 2 * tm * d * 4) + 2 * 2 * half * d * 2 + 6 * tm * d * 4 + (4 << 20)
    row = lambda c: pl.BlockSpec((tm, half), lambda i: (i, c))
    vec = lambda n: pl.BlockSpec((1, n), lambda i: (0, 0))
    return pl.pallas_call(
        _out_kernel,
        grid=(m // tm,),
        in_specs=[row(0), row(0), row(0), row(1),
                  pl.BlockSpec((tm, d), lambda i: (i, 0)),
                  vec(half), vec(half), vec(d),
                  pl.BlockSpec((half, d), lambda i: (0, 0)),
                  pl.BlockSpec((half, d), lambda i: (1, 0))],
        out_specs=pl.BlockSpec((tm, d), lambda i: (i, 0)),
        out_shape=jax.ShapeDtypeStruct((m, d), F32),
        compiler_params=_cparams(("parallel",), vmem),
        name="output_stage",
    )(o_sb, proj_b, o_nsa, proj_b, x2d, g_sb, g_nsa, g_fin, w_out, w_out)


def kernel(x, norm_in, w_in, cmp_k_pos, cmp_k_w1, cmp_k_w2, cmp_v_pos, cmp_v_w1, cmp_v_w2,
           rel_bias, norm_sb, norm_nsa, w_out, norm_final):
    batch, seq, d_model = x.shape
    assert w_in.shape[0] == 1, "single-layer trunk: the final norm is fused into the output stage"
    m = batch * seq
    nbp = seq // CMP_STRIDE
    assert seq % NSA_TK == 0 and nbp % V7X_LANES == 0

    sizes = (SB_WIDTH,) * 4 + (NSA_WIDTH,) + (NSA_KV_WIDTH,) * 6 + (N_GATES * NSA_HEADS, NSA_WIDTH)
    off = np.concatenate([[0], np.cumsum(sizes)])
    (c_sbq, c_sbk, c_sbv, c_sbz, c_nq, c_kc, c_vc, c_ks, c_vs, c_kw, c_vw, c_gate, c_nz, c_end) = [int(o) for o in off]
    cols = {"n_q": 3 * SB_WIDTH, "ks": 3 * SB_WIDTH + NSA_WIDTH}
    cols["vs"] = cols["ks"] + NSA_KV_WIDTH
    cols["kw"] = cols["vs"] + NSA_KV_WIDTH
    cols["vw"] = cols["kw"] + NSA_KV_WIDTH
    cols["gates"] = 2 * SB_WIDTH
    gates_per_group = N_GATES * NSA_GROUP_SIZE

    w = w_in[0]
    w_a = jnp.concatenate([w[:, c_sbq:c_sbz], w[:, c_nq:c_kc], w[:, c_ks:c_gate]], axis=1).astype(BF16)
    w_cv = w[:, c_kc:c_ks].astype(BF16)
    gate_pad = jnp.zeros((d_model, V7X_LANES - gates_per_group), w.dtype)
    w_b = jnp.concatenate(
        [w[:, c_sbz:c_nq], w[:, c_nz:c_end]]
        + [blk for g in range(NSA_KV_GROUPS)
           for blk in (w[:, c_gate + g * gates_per_group:c_gate + (g + 1) * gates_per_group], gate_pad)],
        axis=1).astype(BF16)
    g_in = norm_in[0].reshape(1, d_model)
    x2d = x.reshape(m, d_model)

    proj_a = _rms_proj(x2d, g_in, w_a, BF16, 1024, w_a.shape[1] // 4, "proj_attn")
    proj_cv = _rms_proj(x2d, g_in, w_cv, BF16, 1024, w_cv.shape[1], "proj_cmp")
    proj_b = _rms_proj(x2d, g_in, w_b, F32, 512, w_b.shape[1], "proj_gate")

    proj_a3 = proj_a.reshape(batch, seq, -1)
    o_sb = _sb_attention(proj_a3, batch, seq)
    o_nsa = _nsa_branch(proj_a3, proj_b.reshape(batch, seq, -1), proj_cv.reshape(batch, nbp, -1),
                        (cmp_k_pos[0], cmp_v_pos[0]), (cmp_k_w1[0], cmp_v_w1[0]), (cmp_k_w2[0], cmp_v_w2[0]),
                        rel_bias, cols)

    out = _output_stage(o_sb.reshape(m, SB_WIDTH), o_nsa.reshape(m, NSA_WIDTH), proj_b, x2d,
                        norm_sb[0].reshape(1, SB_WIDTH), norm_nsa[0].reshape(1, NSA_WIDTH),
                        norm_final.reshape(1, d_model), w_out[0].astype(BF16), 256)
    return out.reshape(batch, seq, d_model)
```

```python
import functools
import math

import numpy as np
import jax
import jax.numpy as jnp
from jax import lax
from jax.experimental import pallas as pl
from jax.experimental.pallas import tpu as pltpu

F32 = jnp.float32
BF16 = jnp.bfloat16

HEAD_DIM = 128
SB_HEADS = 8
NSA_HEADS = 8
NSA_KV_GROUPS = 2
NSA_GROUP_SIZE = NSA_HEADS // NSA_KV_GROUPS
SB_WIDTH = SB_HEADS * HEAD_DIM
NSA_WIDTH = NSA_HEADS * HEAD_DIM
NSA_KV_WIDTH = NSA_KV_GROUPS * HEAD_DIM
N_GATES = 3
CMP_STRIDE = 16
CMP_BLOCK = 2 * CMP_STRIDE
CMP_HIDDEN = 256
SEL_BLOCK = 64
SEL_TOP_N = 16
WINDOW = 512
REL_BUCKETS = 32
REL_MAX_EXACT = 16
REL_MAX_DISTANCE = 1024
RMS_EPS = 1e-6
SEL_FORCE = 1e9
MASK_VALUE = -1e30
SEL_MASK = -float(2 ** 100)
SCALE = 1.0 / math.sqrt(HEAD_DIM)
LOG2E = math.log2(math.e)

V7X_LANES = 128
V7X_MXU_DIM = 256
V7X_BF16_SUBLANES = 16
V7X_VMEM_BYTES = 64 * 1024 * 1024

SB_TQ = 4 * V7X_MXU_DIM
SB_TK = V7X_MXU_DIM
SB_HEADS_PER_STEP = 4
NSA_TQ = V7X_MXU_DIM
NSA_TK = V7X_MXU_DIM
NSA_ROWS = NSA_GROUP_SIZE * NSA_TQ

_NT = (((1,), (1,)), ((), ()))


def _bucket_thresholds():
    n = np.arange(0, 4 * REL_MAX_DISTANCE)
    nf = np.maximum(n, 1).astype(np.float64)
    large = REL_MAX_EXACT + (np.log(nf / REL_MAX_EXACT) / math.log(REL_MAX_DISTANCE / REL_MAX_EXACT)
                             * (REL_BUCKETS - REL_MAX_EXACT)).astype(np.int64)
    bucket = np.where(n < REL_MAX_EXACT, n, np.minimum(large, REL_BUCKETS - 1))
    assert np.all(np.diff(bucket) >= 0) and bucket[-1] == REL_BUCKETS - 1
    return [int(n[bucket >= k][0]) for k in range(1, REL_BUCKETS)]


BUCKET_THR = _bucket_thresholds()
LAST_THR = BUCKET_THR[-1]
BIAS_CONST_IDX = -(-(LAST_THR + NSA_TK - 1) // NSA_TQ)
BIAS_FAR = 1 << 20


def _cparams(sem, vmem_bytes):
    return pltpu.CompilerParams(dimension_semantics=sem, vmem_limit_bytes=int(min(vmem_bytes, V7X_VMEM_BYTES)))


def _bias_table_kernel(rb_ref, o_ref, *, stride, offset, const_idx, masked_from, limit):
    h = pl.program_id(0)
    tile = pl.program_id(1)
    rows, cols = o_ref.shape[2], o_ref.shape[3]
    base = tile * NSA_TQ + offset
    if const_idx is not None:
        base = jnp.where(tile == const_idx, BIAS_FAR, jnp.where(tile > const_idx, 0, base))
    ti = lax.broadcasted_iota(jnp.int32, (rows, cols), 0)
    cj = lax.broadcasted_iota(jnp.int32, (rows, cols), 1)
    dist = base + ti - stride * cj
    val = jnp.full((rows, cols), rb_ref[0, h], F32)
    for k in range(1, REL_BUCKETS):
        val = jnp.where(dist >= BUCKET_THR[k - 1], rb_ref[k, h], val)
    keep = dist >= 0
    if limit is not None:
        keep = keep & (dist < limit)
    keep = keep | (tile < masked_from)
    o_ref[0, 0] = val * LOG2E + jnp.where(keep, 0.0, MASK_VALUE)


def _bias_table(rel_bias, n_tiles, cols, stride, offset, const_idx, masked_from, limit):
    kern = functools.partial(_bias_table_kernel, stride=stride, offset=offset, const_idx=const_idx,
                             masked_from=masked_from, limit=limit)
    return pl.pallas_call(
        kern,
        grid=(NSA_HEADS, n_tiles),
        in_specs=[pl.BlockSpec(memory_space=pltpu.SMEM)],
        out_specs=pl.BlockSpec((1, 1, NSA_TQ, cols),
                               lambda h, t: (h // NSA_GROUP_SIZE, t, h % NSA_GROUP_SIZE, 0)),
        out_shape=jax.ShapeDtypeStruct((NSA_KV_GROUPS, n_tiles, NSA_ROWS, cols), F32),
        compiler_params=_cparams(("parallel", "parallel"), 8 << 20),
        name="bias_table",
    )(rel_bias)


def _proj_kernel(x_ref, g_ref, w_ref, cs_ref, o_ref, xn_ref, *, chunk):
    @pl.when(pl.program_id(1) == 0)
    def _normalise():
        def body(c, carry):
            rows = pl.ds(pl.multiple_of(c * chunk, chunk), chunk)
            x = x_ref[rows, :]
            ms = jnp.mean(x * x, axis=-1, keepdims=True)
            xn_ref[rows, :] = (x * lax.rsqrt(ms + RMS_EPS) * g_ref[...]).astype(BF16)
            return carry
        lax.fori_loop(0, x_ref.shape[0] // chunk, body, 0)

    acc = jnp.dot(xn_ref[...], w_ref[...], preferred_element_type=F32)
    o_ref[...] = (acc * cs_ref[...]).astype(o_ref.dtype)


def _rms_proj(x2d, g, w, col_scale, out_dtype, tm, tn, name):
    m, d = x2d.shape
    n = w.shape[1]
    assert m % tm == 0 and n % tn == 0 and col_scale.shape == (1, n)
    osz = jnp.dtype(out_dtype).itemsize
    vmem = 2 * tm * d * 4 + tm * d * 2 + 2 * d * tn * 2 + 2 * tm * tn * osz + tm * tn * 4 + (4 << 20)
    return pl.pallas_call(
        functools.partial(_proj_kernel, chunk=min(tm, 128)),
        grid=(m // tm, n // tn),
        in_specs=[pl.BlockSpec((tm, d), lambda i, j: (i, 0)),
                  pl.BlockSpec((1, d), lambda i, j: (0, 0)),
                  pl.BlockSpec((d, tn), lambda i, j: (0, j)),
                  pl.BlockSpec((1, tn), lambda i, j: (0, j))],
        out_specs=pl.BlockSpec((tm, tn), lambda i, j: (i, j)),
        out_shape=jax.ShapeDtypeStruct((m, n), out_dtype),
        scratch_shapes=[pltpu.VMEM((tm, d), BF16)],
        compiler_params=_cparams(("parallel", "arbitrary"), vmem),
        name=name,
    )(x2d, g, w, col_scale)


def _sb_kernel(q_ref, k_ref, v_ref, o_ref, acc_ref, run_ref):
    i = pl.program_id(2)
    tq, tk = q_ref.shape[1], SB_TK
    n_diag = tq // tk
    row = lax.broadcasted_iota(jnp.int32, (tq, tk), 0)
    col = lax.broadcasted_iota(jnp.int32, (tq, tk), 1)
    below = row > col
    suffix = jnp.where(below[:tk], 1.0, 0.0).astype(BF16)

    def tile(j, row0, diag):
        start = pl.multiple_of(j * tk, tk)
        rows = slice(row0, tq)
        causal = below[:tq - row0]
        n_rows = tq - row0
        zs, rests = [], []
        for hh in range(SB_HEADS_PER_STEP):
            lanes = slice(hh * HEAD_DIM, (hh + 1) * HEAD_DIM)
            z = lax.dot_general(q_ref[0, rows, lanes], k_ref[0, pl.ds(start, tk), lanes], _NT,
                                preferred_element_type=F32)
            sp = jnp.log2(1.0 + jnp.exp2(-jnp.abs(z)))
            neg_rest = jnp.maximum(z, 0.0) + sp
            if diag:
                neg_rest = jnp.where(causal, neg_rest, 0.0)
            zs.append(z)
            rests.append(neg_rest)
        later_all = jnp.dot(jnp.concatenate(rests, axis=0).astype(BF16), suffix, preferred_element_type=F32)
        for hh in range(SB_HEADS_PER_STEP):
            lanes = slice(hh * HEAD_DIM, (hh + 1) * HEAD_DIM)
            later = later_all[hh * n_rows:(hh + 1) * n_rows]
            a = jnp.exp2((zs[hh] - rests[hh]) - (later + run_ref[hh, rows]))
            if diag:
                a = jnp.where(causal, a, 0.0)
            acc_ref[hh, rows] += jnp.dot(a.astype(BF16), v_ref[0, pl.ds(start, tk), lanes],
                                         preferred_element_type=F32)
            run_ref[hh, rows] += jnp.sum(rests[hh], axis=-1, keepdims=True)

    acc_ref[...] = jnp.zeros_like(acc_ref)
    run_ref[...] = jnp.zeros_like(run_ref)
    for c in reversed(range(n_diag)):
        tile(i * n_diag + c, c * tk, True)

    def body(jj, carry):
        tile(i * n_diag - 1 - jj, 0, False)
        return carry
    lax.fori_loop(0, i * n_diag, body, 0)
    for hh in range(SB_HEADS_PER_STEP):
        o_ref[0, :, hh * HEAD_DIM:(hh + 1) * HEAD_DIM] = acc_ref[hh]


def _sb_attention(proj_a, batch, seq):
    tq = min(SB_TQ, seq)
    hps = SB_HEADS_PER_STEP
    width = hps * HEAD_DIM
    assert seq % tq == 0 and tq % SB_TK == 0 and SB_HEADS % hps == 0
    kblk = SB_WIDTH // width
    vmem = 2 * 2 * seq * width * 2 + 6 * tq * width * 4 + hps * 12 * tq * SB_TK * 4 + (4 << 20)
    return pl.pallas_call(
        _sb_kernel,
        grid=(batch, SB_HEADS // hps, seq // tq),
        in_specs=[pl.BlockSpec((1, tq, width), lambda b, h, i: (b, i, h)),
                  pl.BlockSpec((1, seq, width), lambda b, h, i: (b, 0, kblk + h)),
                  pl.BlockSpec((1, seq, width), lambda b, h, i: (b, 0, 2 * kblk + h))],
        out_specs=pl.BlockSpec((1, tq, width), lambda b, h, i: (b, i, h)),
        out_shape=jax.ShapeDtypeStruct((batch, seq, SB_WIDTH), F32),
        scratch_shapes=[pltpu.VMEM((hps, tq, HEAD_DIM), F32), pltpu.VMEM((hps, tq, 1), F32)],
        compiler_params=_cparams(("parallel", "parallel", "arbitrary"), vmem),
        name="sb_attention",
    )(proj_a, proj_a, proj_a)


def _compress_kernel(cv_ref, pos_ref, w1_ref, w2_ref, ko_ref, vo_ref):
    nbp = cv_ref.shape[1]
    tok = 2 * NSA_KV_WIDTH
    for which, out_ref in ((0, ko_ref), (1, vo_ref)):
        for g in range(NSA_KV_GROUPS):
            c0 = (which * NSA_KV_GROUPS + g) * HEAD_DIM
            chunks = jnp.concatenate(
                [cv_ref[0, :, l * tok + c0:l * tok + c0 + HEAD_DIM] for l in range(CMP_STRIDE)],
                axis=1).astype(F32)
            first = jnp.dot((chunks + pos_ref[which, 0]).astype(BF16), w1_ref[which, 0],
                            preferred_element_type=F32)
            second = jnp.dot((chunks + pos_ref[which, 1]).astype(BF16), w1_ref[which, 1],
                             preferred_element_type=F32)
            hid = jax.nn.gelu(first + pltpu.roll(second, nbp - 1, 0)).astype(BF16)
            out_ref[0, g] = jnp.dot(hid, w2_ref[which], preferred_element_type=F32).astype(BF16)


def _compress(cv, pos, w1, w2, batch, nbp):
    tokw = CMP_STRIDE * 2 * NSA_KV_WIDTH
    half = CMP_STRIDE * HEAD_DIM
    vmem = 2 * nbp * tokw * 2 + 2 * 4 * half * CMP_HIDDEN * 2 + 8 * nbp * half * 4 + (8 << 20)
    out_sds = [jax.ShapeDtypeStruct((batch, NSA_KV_GROUPS, nbp, HEAD_DIM), BF16)] * 2
    out_spec = [pl.BlockSpec((1, NSA_KV_GROUPS, nbp, HEAD_DIM), lambda b: (b, 0, 0, 0))] * 2
    return pl.pallas_call(
        _compress_kernel,
        grid=(batch,),
        in_specs=[pl.BlockSpec((1, nbp, tokw), lambda b: (b, 0, 0)),
                  pl.BlockSpec((2, 2, 1, half), lambda b: (0, 0, 0, 0)),
                  pl.BlockSpec((2, 2, half, CMP_HIDDEN), lambda b: (0, 0, 0, 0)),
                  pl.BlockSpec((2, CMP_HIDDEN, HEAD_DIM), lambda b: (0, 0, 0))],
        out_specs=out_spec,
        out_shape=out_sds,
        compiler_params=_cparams(("parallel",), vmem),
        name="kv_compress",
    )(cv, pos, w1, w2)


def _nsa_kernel(q_ref, kc_ref, vc_ref, ks_ref, vs_ref, kw_ref, vw_ref, gate_ref, tbl_ref, tblw_ref, tblc_ref,
                ovt_ref, expt_ref, o_ref, acc_s, m_s, acc_w, m_w):
    i = pl.program_id(2)
    tq, tk, rows = NSA_TQ, NSA_TK, NSA_ROWS
    seq = expt_ref.shape[0]
    nsel = seq // SEL_BLOCK
    t0 = i * tq
    qb = q_ref[0]
    q4 = jnp.concatenate([qb[:, r * HEAD_DIM:(r + 1) * HEAD_DIM] for r in range(NSA_GROUP_SIZE)], axis=0)

    s_c = lax.dot_general(q4, kc_ref[0, 0], _NT, preferred_element_type=F32) + tblc_ref[0, 0]
    m_c = jnp.max(s_c, axis=-1, keepdims=True)
    e_c = jnp.exp2(s_c - m_c)
    l_c = jnp.sum(e_c, axis=-1, keepdims=True)
    p_c = e_c * jnp.where(m_c > 0.5 * MASK_VALUE, 1.0 / l_c, 0.0)
    o_c = jnp.dot(p_c.astype(BF16), vc_ref[0, 0], preferred_element_type=F32)

    p_sum = p_c[0:tq]
    for r in range(1, NSA_GROUP_SIZE):
        p_sum = p_sum + p_c[r * tq:(r + 1) * tq]
    p_hi = p_sum.astype(BF16)
    p_lo = (p_sum - p_hi.astype(F32)).astype(BF16)
    ovt = ovt_ref[...]
    imp = (lax.dot_general(ovt, p_hi, _NT, preferred_element_type=F32)
           + lax.dot_general(ovt, p_lo, _NT, preferred_element_type=F32))[:nsel]
    jrow = lax.broadcasted_iota(jnp.int32, (nsel, tq), 0)
    tcol = t0 + lax.broadcasted_iota(jnp.int32, (nsel, tq), 1)
    cur = tcol // SEL_BLOCK
    valid = jrow * SEL_BLOCK <= tcol
    forced = (jrow == 0) | (jrow == cur) | (jrow == cur - 1)
    score = jnp.where(valid, jnp.where(forced, SEL_FORCE, imp), -SEL_FORCE)
    rank = jnp.zeros((nsel, tq), F32)
    for other in range(nsel):
        srow = score[other:other + 1, :]
        before = (srow > score) | ((srow >= score) & (jrow > other))
        rank = rank + jnp.where(before, 1.0, 0.0)
    drop = jnp.where(rank < float(min(SEL_TOP_N, nsel)), 0.0, SEL_MASK)
    if nsel < V7X_LANES:
        drop = jnp.concatenate([drop, jnp.zeros((V7X_LANES - nsel, tq), F32)], axis=0)
    drop_q = drop.T.astype(BF16)
    q4_sel = jnp.concatenate([q4, jnp.concatenate([drop_q] * NSA_GROUP_SIZE, axis=0)], axis=1)

    assert tq == tk, "the diagonal key tile is the query tile's own position range"
    selected = (ks_ref, vs_ref, m_s, acc_s, False)
    window = (kw_ref, vw_ref, m_w, acc_w, True)
    assert (WINDOW + tq - 1) // tk + 1 == 3, "the window of a query tile spans the diagonal tile and the two before"

    def flash(jt_hi, nt, chains, first):
        width = nt * tk
        start = pl.multiple_of((jt_hi - (nt - 1)) * tk, tk)
        d0 = [i - jt_hi + (nt - 1 - u) for u in range(nt)]
        ones = jnp.ones((width, HEAD_DIM), BF16)
        for k_ref, v_ref, m_ref, acc_ref, windowed in chains:
            k = k_ref[0, pl.ds(start, width), :]
            if windowed:
                bias = [tblw_ref[0, d] for d in d0]
                s = lax.dot_general(q4, k, _NT, preferred_element_type=F32)
            else:
                bias = [tbl_ref[0, (BIAS_CONST_IDX + 1) if first else jnp.minimum(d, BIAS_CONST_IDX)] for d in d0]
                k_sel = jnp.concatenate([k, expt_ref[pl.ds(start, width), :]], axis=1)
                s = lax.dot_general(q4_sel, k_sel, _NT, preferred_element_type=F32)
            s = s + (bias[0] if nt == 1 else jnp.concatenate(bias, axis=1))
            v_aug = jnp.concatenate([v_ref[0, pl.ds(start, width), :], ones], axis=1)
            m_tile = jnp.broadcast_to(jnp.max(s, axis=-1, keepdims=True), (rows, V7X_LANES))
            if first:
                m_new = m_tile
                p = jnp.exp2(s - jnp.concatenate([m_new] * (width // V7X_LANES), axis=1))
                acc_ref[...] = jnp.dot(p.astype(BF16), v_aug, preferred_element_type=F32)
            else:
                m_old = m_ref[...]
                m_new = jnp.maximum(m_old, m_tile)
                p = jnp.exp2(s - jnp.concatenate([m_new] * (width // V7X_LANES), axis=1))
                alpha = jnp.exp2(m_old - m_new)
                acc_ref[...] = (jnp.concatenate([alpha] * 2, axis=1) * acc_ref[...]
                                + jnp.dot(p.astype(BF16), v_aug, preferred_element_type=F32))
            m_ref[...] = m_new

    flash(i, 1, (selected, window), True)

    @pl.when(i >= 2)
    def _():
        flash(i - 1, 1, (selected, window), False)
        flash(i - 2, 1, (selected, window), False)

    @pl.when(i == 1)
    def _():
        flash(0, 1, (selected, window), False)

    def pair_body(p, carry):
        flash(i - 1 - 2 * p, 1, (selected,), False)
        flash(i - 2 - 2 * p, 1, (selected,), False)
        return carry
    lax.fori_loop(1, i >> 1, pair_body, 0)

    @pl.when((i >= 3) & ((i & 1) == 1))
    def _():
        flash(0, 1, (selected,), False)

    gate = 1.0 / (1.0 + jnp.exp(-gate_ref[0]))
    for r in range(NSA_GROUP_SIZE):
        rs = slice(r * tq, (r + 1) * tq)
        o_s = acc_s[rs, 0:HEAD_DIM] / acc_s[rs, HEAD_DIM:2 * HEAD_DIM]
        o_w = acc_w[rs, 0:HEAD_DIM] / acc_w[rs, HEAD_DIM:2 * HEAD_DIM]
        c = r * N_GATES
        out = gate[:, c:c + 1] * o_c[rs] + gate[:, c + 1:c + 2] * o_s + gate[:, c + 2:c + 3] * o_w
        o_ref[0, :, r * HEAD_DIM:(r + 1) * HEAD_DIM] = out


def _nsa_attention(proj_a, proj_b, kcmp, vcmp, tbl, tblw, tblc, ovt, expt, batch, seq, cols):
    tq, tk, rows = NSA_TQ, NSA_TK, NSA_ROWS
    nbp = kcmp.shape[2]
    n_tiles = tbl.shape[1] + tblw.shape[1]
    gw = NSA_GROUP_SIZE * HEAD_DIM
    qblk = cols["n_q"] // gw
    gblk = cols["gates"] // V7X_LANES

    def kv_spec(name):
        blk = cols[name] // HEAD_DIM
        return pl.BlockSpec((1, seq, HEAD_DIM), lambda b, g, i: (b, 0, blk + g))

    def resident(shape, index_map):
        return pl.BlockSpec(shape, index_map, pipeline_mode=pl.Buffered(1))

    cmp_spec = pl.BlockSpec((1, 1, nbp, HEAD_DIM), lambda b, g, i: (b, g, 0, 0))
    vmem = (2 * 4 * seq * HEAD_DIM * 2 + n_tiles * rows * tk * 4 + 2 * rows * nbp * 4
            + seq * V7X_LANES * 2 + 2 * rows * (tk + V7X_LANES) * 4
            + 16 * rows * tk * 4 + (8 << 20))
    return pl.pallas_call(
        _nsa_kernel,
        grid=(batch, NSA_KV_GROUPS, seq // tq),
        in_specs=[pl.BlockSpec((1, tq, gw), lambda b, g, i: (b, i, qblk + g)),
                  cmp_spec, cmp_spec,
                  kv_spec("ks"), kv_spec("vs"), kv_spec("kw"), kv_spec("vw"),
                  pl.BlockSpec((1, tq, V7X_LANES), lambda b, g, i: (b, i, gblk + g)),
                  resident((1, tbl.shape[1], rows, tk), lambda b, g, i: (g, 0, 0, 0)),
                  resident((1, tblw.shape[1], rows, tk), lambda b, g, i: (g, 0, 0, 0)),
                  pl.BlockSpec((1, 1, rows, nbp), lambda b, g, i: (g, i, 0, 0)),
                  resident((V7X_LANES, nbp), lambda b, g, i: (0, 0)),
                  resident((seq, V7X_LANES), lambda b, g, i: (0, 0))],
        out_specs=pl.BlockSpec((1, tq, gw), lambda b, g, i: (b, i, g)),
        out_shape=jax.ShapeDtypeStruct((batch, seq, NSA_WIDTH), F32),
        scratch_shapes=[pltpu.VMEM((rows, 2 * HEAD_DIM), F32), pltpu.VMEM((rows, V7X_LANES), F32),
                        pltpu.VMEM((rows, 2 * HEAD_DIM), F32), pltpu.VMEM((rows, V7X_LANES), F32)],
        compiler_params=_cparams(("parallel", "parallel", "arbitrary"), vmem),
        name="nsa_attention",
    )(proj_a, kcmp, vcmp, proj_a, proj_a, proj_a, proj_a, proj_b, tbl, tblw, tblc, ovt, expt)


def _selection_constants(seq, nbp):
    nb = seq // CMP_STRIDE - 1
    nsel = seq // SEL_BLOCK
    assert nsel <= V7X_LANES and nbp >= nb
    ci = np.arange(nbp)[None, :] * CMP_STRIDE
    sj = np.arange(V7X_LANES)[:, None] * SEL_BLOCK
    ovt = ((ci < sj + SEL_BLOCK) & (ci + CMP_BLOCK > sj) & (np.arange(nbp)[None, :] < nb)
           & (np.arange(V7X_LANES)[:, None] < nsel))
    expt = (np.arange(seq)[:, None] // SEL_BLOCK) == np.arange(V7X_LANES)[None, :]
    return jnp.asarray(ovt, BF16), jnp.asarray(expt, BF16)


def _nsa_branch(proj_a3, proj_b3, proj_cv3, cmp_pos, cmp_w1, cmp_w2, rel_bias, cols):
    batch, seq, _ = proj_a3.shape
    nbp = seq // CMP_STRIDE
    half = CMP_STRIDE * HEAD_DIM
    pos = jnp.stack(cmp_pos).reshape(2, 2, 1, half)
    w1 = jnp.stack(cmp_w1).reshape(2, 2, half, CMP_HIDDEN).astype(BF16)
    w2 = jnp.stack(cmp_w2).astype(BF16)
    kcmp, vcmp = _compress(proj_cv3, pos, w1, w2, batch, nbp)

    n_win = (WINDOW + NSA_TQ - 1) // NSA_TK + 1
    tbl = _bias_table(rel_bias, BIAS_CONST_IDX + 2, NSA_TK, 1, 0, BIAS_CONST_IDX, BIAS_CONST_IDX + 1, None)
    tblw = _bias_table(rel_bias, n_win, NSA_TK, 1, 0, None, 0, WINDOW)
    tblc = _bias_table(rel_bias, seq // NSA_TQ, nbp, CMP_STRIDE, -(CMP_BLOCK - 1), None, 0, None)
    ovt, expt = _selection_constants(seq, nbp)

    return _nsa_attention(proj_a3, proj_b3, kcmp, vcmp, tbl, tblw, tblc, ovt, expt, batch, seq, cols)


def _out_kernel(osb_ref, zsb_ref, onsa_ref, znsa_ref, x_ref, gsb_ref, gnsa_ref, gfin_ref, wsb_ref, wnsa_ref,
                o_ref):
    def gated(o_r, z_r, g_r):
        o = o_r[...]
        y = o * lax.rsqrt(jnp.mean(o * o, axis=-1, keepdims=True) + RMS_EPS) * g_r[...]
        z = z_r[...]
        return (y * (z * (1.0 / (1.0 + jnp.exp(-z))))).astype(BF16)

    mixed = (jnp.dot(gated(osb_ref, zsb_ref, gsb_ref), wsb_ref[...], preferred_element_type=F32)
             + jnp.dot(gated(onsa_ref, znsa_ref, gnsa_ref), wnsa_ref[...], preferred_element_type=F32))
    h = x_ref[...] + mixed
    o_ref[...] = h * lax.rsqrt(jnp.mean(h * h, axis=-1, keepdims=True) + RMS_EPS) * gfin_ref[...]


def _output_stage(o_sb, o_nsa, proj_b, x2d, g_sb, g_nsa, g_fin, w_out, tm):
    m, d = x2d.shape
    half = SB_WIDTH
    assert m % tm == 0
    vmem = 2 * (4 * tm * half * 4 + 2 * tm * d * 4) + 2 * 2 * half * d * 2 + 6 * tm * d * 4 + (4 << 20)
    row = lambda c: pl.BlockSpec((tm, half), lambda i: (i, c))
    vec = lambda n: pl.BlockSpec((1, n), lambda i: (0, 0))
    return pl.pallas_call(
        _out_kernel,
        grid=(m // tm,),
        in_specs=[row(0), row(0), row(0), row(1),
                  pl.BlockSpec((tm, d), lambda i: (i, 0)),
                  vec(half), vec(half), vec(d),
                  pl.BlockSpec((half, d), lambda i: (0, 0)),
                  pl.BlockSpec((half, d), lambda i: (1, 0))],
        out_specs=pl.BlockSpec((tm, d), lambda i: (i, 0)),
        out_shape=jax.ShapeDtypeStruct((m, d), F32),
        compiler_params=_cparams(("parallel",), vmem),
        name="output_stage",
    )(o_sb, proj_b, o_nsa, proj_b, x2d, g_sb, g_nsa, g_fin, w_out, w_out)


def kernel(x, norm_in, w_in, cmp_k_pos, cmp_k_w1, cmp_k_w2, cmp_v_pos, cmp_v_w1, cmp_v_w2,
           rel_bias, norm_sb, norm_nsa, w_out, norm_final):
    batch, seq, d_model = x.shape
    assert w_in.shape[0] == 1, "single-layer trunk: the final norm is fused into the output stage"
    m = batch * seq
    nbp = seq // CMP_STRIDE
    assert seq % NSA_TK == 0 and nbp % V7X_LANES == 0

    sizes = (SB_WIDTH,) * 4 + (NSA_WIDTH,) + (NSA_KV_WIDTH,) * 6 + (N_GATES * NSA_HEADS, NSA_WIDTH)
    off = np.concatenate([[0], np.cumsum(sizes)])
    (c_sbq, c_sbk, c_sbv, c_sbz, c_nq, c_kc, c_vc, c_ks, c_vs, c_kw, c_vw, c_gate, c_nz, c_end) = [int(o) for o in off]
    cols = {"n_q": 3 * SB_WIDTH, "ks": 3 * SB_WIDTH + NSA_WIDTH}
    cols["vs"] = cols["ks"] + NSA_KV_WIDTH
    cols["kw"] = cols["vs"] + NSA_KV_WIDTH
    cols["vw"] = cols["kw"] + NSA_KV_WIDTH
    cols["gates"] = 2 * SB_WIDTH
    gates_per_group = N_GATES * NSA_GROUP_SIZE

    w = w_in[0]
    w_a = jnp.concatenate([w[:, c_sbq:c_sbz], w[:, c_nq:c_kc], w[:, c_ks:c_gate]], axis=1).astype(BF16)
    w_cv = w[:, c_kc:c_ks].astype(BF16)
    gate_pad = jnp.zeros((d_model, V7X_LANES - gates_per_group), w.dtype)
    w_b = jnp.concatenate(
        [w[:, c_sbz:c_nq], w[:, c_nz:c_end]]
        + [blk for g in range(NSA_KV_GROUPS)
           for blk in (w[:, c_gate + g * gates_per_group:c_gate + (g + 1) * gates_per_group], gate_pad)],
        axis=1).astype(BF16)
    g_in = norm_in[0].reshape(1, d_model)
    x2d = x.reshape(m, d_model)

    q_scale = np.ones((1, w_a.shape[1]), np.float32)
    q_scale[:, c_sbq:c_sbq + SB_WIDTH] = SCALE * LOG2E
    q_scale[:, cols["n_q"]:cols["n_q"] + NSA_WIDTH] = SCALE * LOG2E
    proj_a = _rms_proj(x2d, g_in, w_a, jnp.asarray(q_scale), BF16, 1024, w_a.shape[1] // 4, "proj_attn")
    proj_cv = _rms_proj(x2d, g_in, w_cv, jnp.ones((1, w_cv.shape[1]), F32), BF16, 1024, w_cv.shape[1], "proj_cmp")
    proj_b = _rms_proj(x2d, g_in, w_b, jnp.ones((1, w_b.shape[1]), F32), F32, 512, w_b.shape[1], "proj_gate")

    proj_a3 = proj_a.reshape(batch, seq, -1)
    o_sb = _sb_attention(proj_a3, batch, seq)
    o_nsa = _nsa_branch(proj_a3, proj_b.reshape(batch, seq, -1), proj_cv.reshape(batch, nbp, -1),
                        (cmp_k_pos[0], cmp_v_pos[0]), (cmp_k_w1[0], cmp_v_w1[0]), (cmp_k_w2[0], cmp_v_w2[0]),
                        rel_bias, cols)

    out = _output_stage(o_sb.reshape(m, SB_WIDTH), o_nsa.reshape(m, NSA_WIDTH), proj_b, x2d,
                        norm_sb[0].reshape(1, SB_WIDTH), norm_nsa[0].reshape(1, NSA_WIDTH),
                        norm_final.reshape(1, d_model), w_out[0].astype(BF16), 256)
    return out.reshape(batch, seq, d_model)
```

```python
import functools
import math

import numpy as np
import jax
import jax.numpy as jnp
from jax import lax
from jax.experimental import pallas as pl
from jax.experimental.pallas import tpu as pltpu

F32 = jnp.float32
BF16 = jnp.bfloat16

HEAD_DIM = 128
SB_HEADS = 8
NSA_HEADS = 8
NSA_KV_GROUPS = 2
NSA_GROUP_SIZE = NSA_HEADS // NSA_KV_GROUPS
SB_WIDTH = SB_HEADS * HEAD_DIM
NSA_WIDTH = NSA_HEADS * HEAD_DIM
NSA_KV_WIDTH = NSA_KV_GROUPS * HEAD_DIM
N_GATES = 3
CMP_STRIDE = 16
CMP_BLOCK = 2 * CMP_STRIDE
CMP_HIDDEN = 256
SEL_BLOCK = 64
SEL_TOP_N = 16
WINDOW = 512
REL_BUCKETS = 32
REL_MAX_EXACT = 16
REL_MAX_DISTANCE = 1024
RMS_EPS = 1e-6
SEL_FORCE = 1e9
MASK_VALUE = -1e30
SEL_MASK = -float(2 ** 100)
SCALE = 1.0 / math.sqrt(HEAD_DIM)
LOG2E = math.log2(math.e)

V7X_LANES = 128
V7X_MXU_DIM = 256
V7X_BF16_SUBLANES = 16
V7X_VMEM_BYTES = 64 * 1024 * 1024

SB_TQ = 4 * V7X_MXU_DIM
SB_TK = V7X_MXU_DIM
SB_HEADS_PER_STEP = 4
SB_DEAD_LOG2 = 160.0
NSA_TQ = V7X_MXU_DIM
NSA_TK = V7X_MXU_DIM
NSA_ROWS = NSA_GROUP_SIZE * NSA_TQ

_NT = (((1,), (1,)), ((), ()))


def _bucket_thresholds():
    n = np.arange(0, 4 * REL_MAX_DISTANCE)
    nf = np.maximum(n, 1).astype(np.float64)
    large = REL_MAX_EXACT + (np.log(nf / REL_MAX_EXACT) / math.log(REL_MAX_DISTANCE / REL_MAX_EXACT)
                             * (REL_BUCKETS - REL_MAX_EXACT)).astype(np.int64)
    bucket = np.where(n < REL_MAX_EXACT, n, np.minimum(large, REL_BUCKETS - 1))
    assert np.all(np.diff(bucket) >= 0) and bucket[-1] == REL_BUCKETS - 1
    return [int(n[bucket >= k][0]) for k in range(1, REL_BUCKETS)]


BUCKET_THR = _bucket_thresholds()
LAST_THR = BUCKET_THR[-1]
BIAS_CONST_IDX = -(-(LAST_THR + NSA_TK - 1) // NSA_TQ)
BIAS_FAR = 1 << 20
WIN_LAST_D0 = 2
BIAS_WIN_IDX = BIAS_CONST_IDX + 1
BIAS_TILES = BIAS_CONST_IDX + 2
assert NSA_TQ == NSA_TK and NSA_TQ + NSA_TK - 1 < WINDOW <= WIN_LAST_D0 * NSA_TQ < BIAS_CONST_IDX * NSA_TQ


def _cparams(sem, vmem_bytes):
    return pltpu.CompilerParams(dimension_semantics=sem, vmem_limit_bytes=int(min(vmem_bytes, V7X_VMEM_BYTES)))


def _bias_table_kernel(rb_ref, o_ref, *, stride, offset, toeplitz):
    h = pl.program_id(0)
    tile = pl.program_id(1)
    rows, cols = o_ref.shape[2], o_ref.shape[3]
    base = tile * NSA_TQ + offset
    if toeplitz:
        base = jnp.where(tile == BIAS_CONST_IDX, BIAS_FAR,
                         jnp.where(tile == BIAS_WIN_IDX, WIN_LAST_D0 * NSA_TQ, base))
    ti = lax.broadcasted_iota(jnp.int32, (rows, cols), 0)
    cj = lax.broadcasted_iota(jnp.int32, (rows, cols), 1)
    dist = base + ti - stride * cj
    val = jnp.full((rows, cols), rb_ref[0, h], F32)
    for k in range(1, REL_BUCKETS):
        val = jnp.where(dist >= BUCKET_THR[k - 1], rb_ref[k, h], val)
    keep = dist >= 0
    if toeplitz:
        keep = (keep & (dist < WINDOW)) | ((tile != 0) & (tile != BIAS_WIN_IDX))
    o_ref[0, 0] = val * LOG2E + jnp.where(keep, 0.0, MASK_VALUE)


def _bias_table(rel_bias, n_tiles, cols, stride, offset, toeplitz):
    kern = functools.partial(_bias_table_kernel, stride=stride, offset=offset, toeplitz=toeplitz)
    return pl.pallas_call(
        kern,
        grid=(NSA_HEADS, n_tiles),
        in_specs=[pl.BlockSpec(memory_space=pltpu.SMEM)],
        out_specs=pl.BlockSpec((1, 1, NSA_TQ, cols),
                               lambda h, t: (h // NSA_GROUP_SIZE, t, h % NSA_GROUP_SIZE, 0)),
        out_shape=jax.ShapeDtypeStruct((NSA_KV_GROUPS, n_tiles, NSA_ROWS, cols), F32),
        compiler_params=_cparams(("parallel", "parallel"), 8 << 20),
        name="bias_table",
    )(rel_bias)


def _proj_kernel(x_ref, g_ref, w_ref, cs_ref, o_ref, xn_ref, *, chunk):
    @pl.when(pl.program_id(1) == 0)
    def _normalise():
        def body(c, carry):
            rows = pl.ds(pl.multiple_of(c * chunk, chunk), chunk)
            x = x_ref[rows, :]
            ms = jnp.mean(x * x, axis=-1, keepdims=True)
            xn_ref[rows, :] = (x * lax.rsqrt(ms + RMS_EPS) * g_ref[...]).astype(BF16)
            return carry
        lax.fori_loop(0, x_ref.shape[0] // chunk, body, 0)

    acc = jnp.dot(xn_ref[...], w_ref[...], preferred_element_type=F32)
    o_ref[...] = (acc * cs_ref[...]).astype(o_ref.dtype)


def _rms_proj(x2d, g, w, col_scale, out_dtype, tm, tn, name):
    m, d = x2d.shape
    n = w.shape[1]
    assert m % tm == 0 and n % tn == 0 and col_scale.shape == (1, n)
    osz = jnp.dtype(out_dtype).itemsize
    vmem = 2 * tm * d * 4 + tm * d * 2 + 2 * d * tn * 2 + 2 * tm * tn * osz + tm * tn * 4 + (4 << 20)
    return pl.pallas_call(
        functools.partial(_proj_kernel, chunk=min(tm, 128)),
        grid=(m // tm, n // tn),
        in_specs=[pl.BlockSpec((tm, d), lambda i, j: (i, 0)),
                  pl.BlockSpec((1, d), lambda i, j: (0, 0)),
                  pl.BlockSpec((d, tn), lambda i, j: (0, j)),
                  pl.BlockSpec((1, tn), lambda i, j: (0, j))],
        out_specs=pl.BlockSpec((tm, tn), lambda i, j: (i, j)),
        out_shape=jax.ShapeDtypeStruct((m, n), out_dtype),
        scratch_shapes=[pltpu.VMEM((tm, d), BF16)],
        compiler_params=_cparams(("parallel", "arbitrary"), vmem),
        name=name,
    )(x2d, g, w, col_scale)


def _sb_kernel(q_ref, k_ref, v_ref, o_ref, acc_ref, run_ref):
    i = pl.program_id(2)
    tq, tk = q_ref.shape[1], SB_TK
    n_diag = tq // tk
    row = lax.broadcasted_iota(jnp.int32, (tq, tk), 0)
    col = lax.broadcasted_iota(jnp.int32, (tq, tk), 1)
    below = row > col
    suffix = jnp.where(below[:tk], 1.0, 0.0).astype(BF16)

    def tile(j, row0, diag):
        start = pl.multiple_of(j * tk, tk)
        rows = slice(row0, tq)
        causal = below[:tq - row0]
        n_rows = tq - row0
        log_betas, rests, runs = [], [], []
        for hh in range(SB_HEADS_PER_STEP):
            lanes = slice(hh * HEAD_DIM, (hh + 1) * HEAD_DIM)
            z = lax.dot_general(q_ref[0, rows, lanes], k_ref[0, pl.ds(start, tk), lanes], _NT,
                                preferred_element_type=F32)
            sp = jnp.log2(1.0 + jnp.exp2(-jnp.abs(z)))
            neg_rest = jnp.maximum(z, 0.0) + sp
            log_beta = z - neg_rest
            if diag:
                neg_rest = jnp.where(causal, neg_rest, 0.0)
            run = run_ref[hh, rows]
            log_betas.append(log_beta - run)
            rests.append(neg_rest.astype(BF16))
            run_ref[hh, rows] = run + jnp.sum(neg_rest, axis=-1, keepdims=True)
        later_all = jnp.dot(jnp.concatenate(rests, axis=0), suffix, preferred_element_type=F32)
        for hh in range(SB_HEADS_PER_STEP):
            lanes = slice(hh * HEAD_DIM, (hh + 1) * HEAD_DIM)
            a = jnp.exp2(log_betas[hh] - later_all[hh * n_rows:(hh + 1) * n_rows])
            if diag:
                a = jnp.where(causal, a, 0.0)
            acc_ref[hh, rows] += jnp.dot(a.astype(BF16), v_ref[0, pl.ds(start, tk), lanes],
                                         preferred_element_type=F32)

    acc_ref[...] = jnp.zeros_like(acc_ref)
    run_ref[...] = jnp.zeros_like(run_ref)
    for c in reversed(range(n_diag)):
        tile(i * n_diag + c, c * tk, True)

    def walk(carry):
        jj, _ = carry
        tile(i * n_diag - 1 - jj, 0, False)
        return jj + 1, jnp.min(run_ref[...])

    def alive(carry):
        jj, least_run = carry
        return (jj < i * n_diag) & (least_run < SB_DEAD_LOG2)
    lax.while_loop(alive, walk, (jnp.int32(0), jnp.min(run_ref[...])))
    for hh in range(SB_HEADS_PER_STEP):
        o_ref[0, :, hh * HEAD_DIM:(hh + 1) * HEAD_DIM] = acc_ref[hh]


def _sb_attention(proj_a, batch, seq):
    tq = min(SB_TQ, seq)
    hps = SB_HEADS_PER_STEP
    width = hps * HEAD_DIM
    assert seq % tq == 0 and tq % SB_TK == 0 and SB_HEADS % hps == 0
    kblk = SB_WIDTH // width
    vmem = 2 * 2 * seq * width * 2 + 6 * tq * width * 4 + hps * 12 * tq * SB_TK * 4 + (4 << 20)
    return pl.pallas_call(
        _sb_kernel,
        grid=(batch, SB_HEADS // hps, seq // tq),
        in_specs=[pl.BlockSpec((1, tq, width), lambda b, h, i: (b, i, h)),
                  pl.BlockSpec((1, seq, width), lambda b, h, i: (b, 0, kblk + h)),
                  pl.BlockSpec((1, seq, width), lambda b, h, i: (b, 0, 2 * kblk + h))],
        out_specs=pl.BlockSpec((1, tq, width), lambda b, h, i: (b, i, h)),
        out_shape=jax.ShapeDtypeStruct((batch, seq, SB_WIDTH), F32),
        scratch_shapes=[pltpu.VMEM((hps, tq, HEAD_DIM), F32), pltpu.VMEM((hps, tq, 1), F32)],
        compiler_params=_cparams(("parallel", "parallel", "arbitrary"), vmem),
        name="sb_attention",
    )(proj_a, proj_a, proj_a)


def _compress_kernel(cv_ref, pos_ref, w1_ref, w2_ref, ko_ref, vo_ref):
    nbp = cv_ref.shape[1]
    tok = 2 * NSA_KV_WIDTH
    for which, out_ref in ((0, ko_ref), (1, vo_ref)):
        for g in range(NSA_KV_GROUPS):
            c0 = (which * NSA_KV_GROUPS + g) * HEAD_DIM
            chunks = jnp.concatenate(
                [cv_ref[0, :, l * tok + c0:l * tok + c0 + HEAD_DIM] for l in range(CMP_STRIDE)],
                axis=1).astype(F32)
            first = jnp.dot((chunks + pos_ref[which, 0]).astype(BF16), w1_ref[which, 0],
                            preferred_element_type=F32)
            second = jnp.dot((chunks + pos_ref[which, 1]).astype(BF16), w1_ref[which, 1],
                             preferred_element_type=F32)
            hid = jax.nn.gelu(first + pltpu.roll(second, nbp - 1, 0)).astype(BF16)
            out_ref[0, g] = jnp.dot(hid, w2_ref[which], preferred_element_type=F32).astype(BF16)


def _compress(cv, pos, w1, w2, batch, nbp):
    tokw = CMP_STRIDE * 2 * NSA_KV_WIDTH
    half = CMP_STRIDE * HEAD_DIM
    vmem = 2 * nbp * tokw * 2 + 2 * 4 * half * CMP_HIDDEN * 2 + 8 * nbp * half * 4 + (8 << 20)
    out_sds = [jax.ShapeDtypeStruct((batch, NSA_KV_GROUPS, nbp, HEAD_DIM), BF16)] * 2
    out_spec = [pl.BlockSpec((1, NSA_KV_GROUPS, nbp, HEAD_DIM), lambda b: (b, 0, 0, 0))] * 2
    return pl.pallas_call(
        _compress_kernel,
        grid=(batch,),
        in_specs=[pl.BlockSpec((1, nbp, tokw), lambda b: (b, 0, 0)),
                  pl.BlockSpec((2, 2, 1, half), lambda b: (0, 0, 0, 0)),
                  pl.BlockSpec((2, 2, half, CMP_HIDDEN), lambda b: (0, 0, 0, 0)),
                  pl.BlockSpec((2, CMP_HIDDEN, HEAD_DIM), lambda b: (0, 0, 0))],
        out_specs=out_spec,
        out_shape=out_sds,
        compiler_params=_cparams(("parallel",), vmem),
        name="kv_compress",
    )(cv, pos, w1, w2)


def _nsa_kernel(q_ref, kc_ref, vc_ref, ks_ref, vs_ref, kw_ref, vw_ref, gate_ref, tbl_ref, tblc_ref,
                ovt_ref, expt_ref, o_ref, acc_ref, m_ref):
    i = pl.program_id(1)
    tq, tk, rows = NSA_TQ, NSA_TK, NSA_ROWS
    gw = NSA_GROUP_SIZE * HEAD_DIM
    seq = expt_ref.shape[0]
    nsel = seq // SEL_BLOCK
    t0 = i * tq
    ovt = ovt_ref[...]
    jrow = lax.broadcasted_iota(jnp.int32, (nsel, tq), 0)
    tcol = t0 + lax.broadcasted_iota(jnp.int32, (nsel, tq), 1)
    cur = tcol // SEL_BLOCK
    valid = jrow * SEL_BLOCK <= tcol
    forced = (jrow == 0) | (jrow == cur) | (jrow == cur - 1)
    slab_row = lax.broadcasted_iota(jnp.int32, (8, tq), 0)

    def compressed_and_selection(g):
        qb = q_ref[0, :, g * gw:(g + 1) * gw]
        q4 = jnp.concatenate([qb[:, r * HEAD_DIM:(r + 1) * HEAD_DIM] for r in range(NSA_GROUP_SIZE)], axis=0)
        s_c = lax.dot_general(q4, kc_ref[0, g], _NT, preferred_element_type=F32) + tblc_ref[g, 0]
        reps = s_c.shape[1] // V7X_LANES
        m_c = jnp.broadcast_to(jnp.max(s_c, axis=-1, keepdims=True), (rows, V7X_LANES))
        e_c = jnp.exp2(s_c - jnp.concatenate([m_c] * reps, axis=1))
        l_c = jnp.broadcast_to(jnp.sum(e_c, axis=-1, keepdims=True), (rows, V7X_LANES))
        inv = jnp.where(m_c > 0.5 * MASK_VALUE, 1.0 / l_c, 0.0)
        p_c = e_c * jnp.concatenate([inv] * reps, axis=1)
        o_c = jnp.dot(p_c.astype(BF16), vc_ref[0, g], preferred_element_type=F32)

        p_sum = p_c[0:tq]
        for r in range(1, NSA_GROUP_SIZE):
            p_sum = p_sum + p_c[r * tq:(r + 1) * tq]
        p_hi = p_sum.astype(BF16)
        p_lo = (p_sum - p_hi.astype(F32)).astype(BF16)
        imp = (lax.dot_general(ovt, p_hi, _NT, preferred_element_type=F32)
               + lax.dot_general(ovt, p_lo, _NT, preferred_element_type=F32))[:nsel]
        score = jnp.where(valid, jnp.where(forced, SEL_FORCE, imp), -SEL_FORCE)
        score3 = score.reshape(nsel // 8, 8, tq)
        slabs = [score3[n] for n in range(nsel // 8)]
        ranks = [jnp.zeros((8, tq), F32) for _ in slabs]
        for other in range(nsel):
            srow = slabs[other // 8][other % 8:other % 8 + 1, :]
            for n, slab in enumerate(slabs):
                if 8 * n > other:
                    before = srow >= slab
                elif 8 * n + 7 <= other:
                    before = srow > slab
                else:
                    before = (srow > slab) | ((srow >= slab) & (slab_row > other - 8 * n))
                ranks[n] = ranks[n] + jnp.where(before, 1.0, 0.0)
        rank = jnp.stack(ranks, axis=0).reshape(nsel, tq)
        drop = jnp.where(rank < float(min(SEL_TOP_N, nsel)), 0.0, SEL_MASK)
        if nsel < V7X_LANES:
            drop = jnp.concatenate([drop, jnp.zeros((V7X_LANES - nsel, tq), F32)], axis=0)
        drop_q = drop.T.astype(BF16)
        q4_sel = jnp.concatenate([q4, jnp.concatenate([drop_q] * NSA_GROUP_SIZE, axis=0)], axis=1)
        return q4, q4_sel, o_c

    per_group = [compressed_and_selection(g) for g in range(NSA_KV_GROUPS)]

    selected = [(g, 0) for g in range(NSA_KV_GROUPS)]
    both = selected + [(g, 1) for g in range(NSA_KV_GROUPS)]
    ones = jnp.ones((tk, HEAD_DIM), BF16)

    def flash(jt, chains, first):
        start = pl.multiple_of(jt * tk, tk)
        d0 = i - jt
        for g, branch in chains:
            q4, q4_sel, _ = per_group[g]
            lanes = slice(g * HEAD_DIM, (g + 1) * HEAD_DIM)
            if branch == 1:
                bias = tbl_ref[g, 0 if first else jnp.where(d0 == WIN_LAST_D0, BIAS_WIN_IDX, d0)]
                s = lax.dot_general(q4, kw_ref[0, pl.ds(start, tk), lanes], _NT, preferred_element_type=F32)
                v = vw_ref[0, pl.ds(start, tk), lanes]
            else:
                bias = tbl_ref[g, 0 if first else jnp.minimum(d0, BIAS_CONST_IDX)]
                k_sel = jnp.concatenate([ks_ref[0, pl.ds(start, tk), lanes], expt_ref[pl.ds(start, tk), :]], axis=1)
                s = lax.dot_general(q4_sel, k_sel, _NT, preferred_element_type=F32)
                v = vs_ref[0, pl.ds(start, tk), lanes]
            s = s + bias
            v_aug = jnp.concatenate([v, ones], axis=1)
            m_tile = jnp.broadcast_to(jnp.max(s, axis=-1, keepdims=True), (rows, V7X_LANES))
            if first:
                m_new = m_tile
                p = jnp.exp2(s - jnp.concatenate([m_new] * (tk // V7X_LANES), axis=1))
                acc_ref[g, branch] = jnp.dot(p.astype(BF16), v_aug, preferred_element_type=F32)
            else:
                m_old = m_ref[g, branch]
                m_new = jnp.maximum(m_old, m_tile)
                p = jnp.exp2(s - jnp.concatenate([m_new] * (tk // V7X_LANES), axis=1))
                alpha = jnp.exp2(m_old - m_new)
                acc_ref[g, branch] = (jnp.concatenate([alpha] * 2, axis=1) * acc_ref[g, branch]
                                      + jnp.dot(p.astype(BF16), v_aug, preferred_element_type=F32))
            m_ref[g, branch] = m_new

    flash(i, both, True)

    @pl.when(i >= 2)
    def _():
        flash(i - 1, both, False)
        flash(i - 2, both, False)

    @pl.when(i == 1)
    def _():
        flash(0, both, False)

    def pair_body(p, carry):
        flash(i - 1 - 2 * p, selected, False)
        flash(i - 2 - 2 * p, selected, False)
        return carry
    lax.fori_loop(1, i >> 1, pair_body, 0)

    @pl.when((i >= 3) & ((i & 1) == 1))
    def _():
        flash(0, selected, False)

    for g in range(NSA_KV_GROUPS):
        gate = 1.0 / (1.0 + jnp.exp(-gate_ref[0, :, g * V7X_LANES:(g + 1) * V7X_LANES]))
        o_c = per_group[g][2]
        for r in range(NSA_GROUP_SIZE):
            rs = slice(r * tq, (r + 1) * tq)
            o_s = acc_ref[g, 0, rs, 0:HEAD_DIM] / acc_ref[g, 0, rs, HEAD_DIM:2 * HEAD_DIM]
            o_w = acc_ref[g, 1, rs, 0:HEAD_DIM] / acc_ref[g, 1, rs, HEAD_DIM:2 * HEAD_DIM]
            c = r * N_GATES
            out = gate[:, c:c + 1] * o_c[rs] + gate[:, c + 1:c + 2] * o_s + gate[:, c + 2:c + 3] * o_w
            col = (g * NSA_GROUP_SIZE + r) * HEAD_DIM
            o_ref[0, :, col:col + HEAD_DIM] = out


def _nsa_attention(proj_a, proj_b, kcmp, vcmp, tbl, tblc, ovt, expt, batch, seq, cols):
    tq, tk, rows = NSA_TQ, NSA_TK, NSA_ROWS
    nbp = kcmp.shape[2]
    n_tiles = tbl.shape[1]
    groups = NSA_KV_GROUPS
    qblk = cols["n_q"] // NSA_WIDTH
    gblk = cols["gates"] // (groups * V7X_LANES)
    assert cols["n_q"] % NSA_WIDTH == 0 and cols["gates"] % (groups * V7X_LANES) == 0

    def resident(shape, index_map):
        return pl.BlockSpec(shape, index_map, pipeline_mode=pl.Buffered(1))

    def kv_spec(name):
        assert cols[name] % NSA_KV_WIDTH == 0
        blk = cols[name] // NSA_KV_WIDTH
        return resident((1, seq, NSA_KV_WIDTH), lambda b, i: (b, 0, blk))

    cmp_spec = resident((1, groups, nbp, HEAD_DIM), lambda b, i: (b, 0, 0, 0))
    vmem = (4 * seq * NSA_KV_WIDTH * 2 + groups * n_tiles * rows * tk * 4 + 2 * groups * rows * nbp * 4
            + seq * V7X_LANES * 2 + 2 * groups * rows * (2 * HEAD_DIM + V7X_LANES) * 4
            + 4 * tq * (NSA_WIDTH + groups * V7X_LANES) * 4
            + 12 * rows * tk * 4 + (4 << 20))
    return pl.pallas_call(
        _nsa_kernel,
        grid=(batch, seq // tq),
        in_specs=[pl.BlockSpec((1, tq, NSA_WIDTH), lambda b, i: (b, i, qblk)),
                  cmp_spec, cmp_spec,
                  kv_spec("ks"), kv_spec("vs"), kv_spec("kw"), kv_spec("vw"),
                  pl.BlockSpec((1, tq, groups * V7X_LANES), lambda b, i: (b, i, gblk)),
                  resident((groups, n_tiles, rows, tk), lambda b, i: (0, 0, 0, 0)),
                  pl.BlockSpec((groups, 1, rows, nbp), lambda b, i: (0, i, 0, 0)),
                  resident((V7X_LANES, nbp), lambda b, i: (0, 0)),
                  resident((seq, V7X_LANES), lambda b, i: (0, 0))],
        out_specs=pl.BlockSpec((1, tq, NSA_WIDTH), lambda b, i: (b, i, 0)),
        out_shape=jax.ShapeDtypeStruct((batch, seq, NSA_WIDTH), F32),
        scratch_shapes=[pltpu.VMEM((groups, 2, rows, 2 * HEAD_DIM), F32),
                        pltpu.VMEM((groups, 2, rows, V7X_LANES), F32)],
        compiler_params=_cparams(("parallel", "arbitrary"), vmem),
        name="nsa_attention",
    )(proj_a, kcmp, vcmp, proj_a, proj_a, proj_a, proj_a, proj_b, tbl, tblc, ovt, expt)


def _selection_constants(seq, nbp):
    nb = seq // CMP_STRIDE - 1
    nsel = seq // SEL_BLOCK
    assert nsel <= V7X_LANES and nbp >= nb
    ci = np.arange(nbp)[None, :] * CMP_STRIDE
    sj = np.arange(V7X_LANES)[:, None] * SEL_BLOCK
    ovt = ((ci < sj + SEL_BLOCK) & (ci + CMP_BLOCK > sj) & (np.arange(nbp)[None, :] < nb)
           & (np.arange(V7X_LANES)[:, None] < nsel))
    expt = (np.arange(seq)[:, None] // SEL_BLOCK) == np.arange(V7X_LANES)[None, :]
    return jnp.asarray(ovt, BF16), jnp.asarray(expt, BF16)


def _nsa_branch(proj_a3, proj_b3, proj_cv3, cmp_pos, cmp_w1, cmp_w2, rel_bias, cols):
    batch, seq, _ = proj_a3.shape
    nbp = seq // CMP_STRIDE
    half = CMP_STRIDE * HEAD_DIM
    pos = jnp.stack(cmp_pos).reshape(2, 2, 1, half)
    w1 = jnp.stack(cmp_w1).reshape(2, 2, half, CMP_HIDDEN).astype(BF16)
    w2 = jnp.stack(cmp_w2).astype(BF16)
    kcmp, vcmp = _compress(proj_cv3, pos, w1, w2, batch, nbp)

    tbl = _bias_table(rel_bias, BIAS_TILES, NSA_TK, 1, 0, True)
    tblc = _bias_table(rel_bias, seq // NSA_TQ, nbp, CMP_STRIDE, -(CMP_BLOCK - 1), False)
    ovt, expt = _selection_constants(seq, nbp)
    return _nsa_attention(proj_a3, proj_b3, kcmp, vcmp, tbl, tblc, ovt, expt, batch, seq, cols)


def _out_kernel(osb_ref, zsb_ref, onsa_ref, znsa_ref, x_ref, gsb_ref, gnsa_ref, gfin_ref, wsb_ref, wnsa_ref,
                o_ref):
    def gated(o_r, z_r, g_r):
        o = o_r[...]
        y = o * lax.rsqrt(jnp.mean(o * o, axis=-1, keepdims=True) + RMS_EPS) * g_r[...]
        z = z_r[...]
        return (y * (z * (1.0 / (1.0 + jnp.exp(-z))))).astype(BF16)

    mixed = (jnp.dot(gated(osb_ref, zsb_ref, gsb_ref), wsb_ref[...], preferred_element_type=F32)
             + jnp.dot(gated(onsa_ref, znsa_ref, gnsa_ref), wnsa_ref[...], preferred_element_type=F32))
    h = x_ref[...] + mixed
    o_ref[...] = h * lax.rsqrt(jnp.mean(h * h, axis=-1, keepdims=True) + RMS_EPS) * gfin_ref[...]


def _output_stage(o_sb, o_nsa, proj_b, x2d, g_sb, g_nsa, g_fin, w_out, tm):
    m, d = x2d.shape
    half = SB_WIDTH
    assert m % tm == 0
    vmem = 2 * (4 * tm * half * 4 + 2 * tm * d * 4) + 2 * 2 * half * d * 2 + 6 * tm * d * 4 + (4 << 20)
    row = lambda c: pl.BlockSpec((tm, half), lambda i: (i, c))
    vec = lambda n: pl.BlockSpec((1, n), lambda i: (0, 0))
    return pl.pallas_call(
        _out_kernel,
        grid=(m // tm,),
        in_specs=[row(0), row(0), row(0), row(1),
                  pl.BlockSpec((tm, d), lambda i: (i, 0)),
                  vec(half), vec(half), vec(d),
                  pl.BlockSpec((half, d), lambda i: (0, 0)),
                  pl.BlockSpec((half, d), lambda i: (1, 0))],
        out_specs=pl.BlockSpec((tm, d), lambda i: (i, 0)),
        out_shape=jax.ShapeDtypeStruct((m, d), F32),
        compiler_params=_cparams(("parallel",), vmem),
        name="output_stage",
    )(o_sb, proj_b, o_nsa, proj_b, x2d, g_sb, g_nsa, g_fin, w_out, w_out)


def kernel(x, norm_in, w_in, cmp_k_pos, cmp_k_w1, cmp_k_w2, cmp_v_pos, cmp_v_w1, cmp_v_w2,
           rel_bias, norm_sb, norm_nsa, w_out, norm_final):
    batch, seq, d_model = x.shape
    assert w_in.shape[0] == 1, "single-layer trunk: the final norm is fused into the output stage"
    m = batch * seq
    nbp = seq // CMP_STRIDE
    assert seq % NSA_TK == 0 and nbp % V7X_LANES == 0

    sizes = (SB_WIDTH,) * 4 + (NSA_WIDTH,) + (NSA_KV_WIDTH,) * 6 + (N_GATES * NSA_HEADS, NSA_WIDTH)
    off = np.concatenate([[0], np.cumsum(sizes)])
    (c_sbq, c_sbk, c_sbv, c_sbz, c_nq, c_kc, c_vc, c_ks, c_vs, c_kw, c_vw, c_gate, c_nz, c_end) = [int(o) for o in off]
    cols = {"n_q": 3 * SB_WIDTH, "ks": 3 * SB_WIDTH + NSA_WIDTH}
    cols["vs"] = cols["ks"] + NSA_KV_WIDTH
    cols["kw"] = cols["vs"] + NSA_KV_WIDTH
    cols["vw"] = cols["kw"] + NSA_KV_WIDTH
    cols["gates"] = 2 * SB_WIDTH
    gates_per_group = N_GATES * NSA_GROUP_SIZE

    w = w_in[0]
    w_a = jnp.concatenate([w[:, c_sbq:c_sbz], w[:, c_nq:c_kc], w[:, c_ks:c_gate]], axis=1).astype(BF16)
    w_cv = w[:, c_kc:c_ks].astype(BF16)
    gate_pad = jnp.zeros((d_model, V7X_LANES - gates_per_group), w.dtype)
    w_b = jnp.concatenate(
        [w[:, c_sbz:c_nq], w[:, c_nz:c_end]]
        + [blk for g in range(NSA_KV_GROUPS)
           for blk in (w[:, c_gate + g * gates_per_group:c_gate + (g + 1) * gates_per_group], gate_pad)],
        axis=1).astype(BF16)
    g_in = norm_in[0].reshape(1, d_model)
    x2d = x.reshape(m, d_model)

    q_scale = np.ones((1, w_a.shape[1]), np.float32)
    q_scale[:, c_sbq:c_sbq + SB_WIDTH] = SCALE * LOG2E
    q_scale[:, cols["n_q"]:cols["n_q"] + NSA_WIDTH] = SCALE * LOG2E
    proj_a = _rms_proj(x2d, g_in, w_a, jnp.asarray(q_scale), BF16, 1024, w_a.shape[1] // 4, "proj_attn")
    proj_cv = _rms_proj(x2d, g_in, w_cv, jnp.ones((1, w_cv.shape[1]), F32), BF16, 1024, w_cv.shape[1], "proj_cmp")
    proj_b = _rms_proj(x2d, g_in, w_b, jnp.ones((1, w_b.shape[1]), F32), F32, 512, w_b.shape[1], "proj_gate")

    proj_a3 = proj_a.reshape(batch, seq, -1)
    o_sb = _sb_attention(proj_a3, batch, seq)
    o_nsa = _nsa_branch(proj_a3, proj_b.reshape(batch, seq, -1), proj_cv.reshape(batch, nbp, -1),
                        (cmp_k_pos[0], cmp_v_pos[0]), (cmp_k_w1[0], cmp_v_w1[0]), (cmp_k_w2[0], cmp_v_w2[0]),
                        rel_bias, cols)

    out = _output_stage(o_sb.reshape(m, SB_WIDTH), o_nsa.reshape(m, NSA_WIDTH), proj_b, x2d,
                        norm_sb[0].reshape(1, SB_WIDTH), norm_nsa[0].reshape(1, NSA_WIDTH),
                        norm_final.reshape(1, d_model), w_out[0].astype(BF16), 256)
    return out.reshape(batch, seq, d_model)
```

```python
import functools
import math

import numpy as np
import jax
import jax.numpy as jnp
from jax import lax
from jax.experimental import pallas as pl
from jax.experimental.pallas import tpu as pltpu

F32 = jnp.float32
BF16 = jnp.bfloat16

HEAD_DIM = 128
SB_HEADS = 8
NSA_HEADS = 8
NSA_KV_GROUPS = 2
NSA_GROUP_SIZE = NSA_HEADS // NSA_KV_GROUPS
SB_WIDTH = SB_HEADS * HEAD_DIM
NSA_WIDTH = NSA_HEADS * HEAD_DIM
NSA_KV_WIDTH = NSA_KV_GROUPS * HEAD_DIM
N_GATES = 3
CMP_STRIDE = 16
CMP_BLOCK = 2 * CMP_STRIDE
CMP_HIDDEN = 256
SEL_BLOCK = 64
SEL_TOP_N = 16
WINDOW = 512
REL_BUCKETS = 32
REL_MAX_EXACT = 16
REL_MAX_DISTANCE = 1024
RMS_EPS = 1e-6
SEL_FORCE = 1e9
MASK_VALUE = -1e30
SEL_MASK = -float(2 ** 100)
SCALE = 1.0 / math.sqrt(HEAD_DIM)
LOG2E = math.log2(math.e)

V7X_LANES = 128
V7X_MXU_DIM = 256
V7X_BF16_SUBLANES = 16
V7X_VMEM_BYTES = 64 * 1024 * 1024

SB_TQ = 1 * V7X_MXU_DIM
SB_TK = V7X_MXU_DIM
SB_HEADS_PER_STEP = 8
SB_DEAD_LOG2 = 160.0
NSA_TQ = V7X_MXU_DIM
NSA_TK = V7X_MXU_DIM
NSA_ROWS = NSA_GROUP_SIZE * NSA_TQ
OUT_TM = 2 * V7X_MXU_DIM
OUT_CHUNK = V7X_MXU_DIM

_NT = (((1,), (1,)), ((), ()))


def _bucket_thresholds():
    n = np.arange(0, 4 * REL_MAX_DISTANCE)
    nf = np.maximum(n, 1).astype(np.float64)
    large = REL_MAX_EXACT + (np.log(nf / REL_MAX_EXACT) / math.log(REL_MAX_DISTANCE / REL_MAX_EXACT)
                             * (REL_BUCKETS - REL_MAX_EXACT)).astype(np.int64)
    bucket = np.where(n < REL_MAX_EXACT, n, np.minimum(large, REL_BUCKETS - 1))
    assert np.all(np.diff(bucket) >= 0) and bucket[-1] == REL_BUCKETS - 1
    return [int(n[bucket >= k][0]) for k in range(1, REL_BUCKETS)]


BUCKET_THR = _bucket_thresholds()
LAST_THR = BUCKET_THR[-1]
BIAS_CONST_IDX = -(-(LAST_THR + NSA_TK - 1) // NSA_TQ)
BIAS_FAR = 1 << 20
WIN_LAST_D0 = 2
BIAS_WIN_IDX = BIAS_CONST_IDX + 1
BIAS_TILES = BIAS_CONST_IDX + 2
assert NSA_TQ == NSA_TK and NSA_TQ + NSA_TK - 1 < WINDOW <= WIN_LAST_D0 * NSA_TQ < BIAS_CONST_IDX * NSA_TQ


def _cparams(sem, vmem_bytes):
    return pltpu.CompilerParams(dimension_semantics=sem, vmem_limit_bytes=int(min(vmem_bytes, V7X_VMEM_BYTES)))


def _bias_table_kernel(rb_ref, o_ref, *, stride, offset, toeplitz):
    h = pl.program_id(0)
    tile = pl.program_id(1)
    rows, cols = o_ref.shape[2], o_ref.shape[3]
    base = tile * NSA_TQ + offset
    if toeplitz:
        base = jnp.where(tile == BIAS_CONST_IDX, BIAS_FAR,
                         jnp.where(tile == BIAS_WIN_IDX, WIN_LAST_D0 * NSA_TQ, base))
    ti = lax.broadcasted_iota(jnp.int32, (rows, cols), 0)
    cj = lax.broadcasted_iota(jnp.int32, (rows, cols), 1)
    dist = base + ti - stride * cj
    val = jnp.full((rows, cols), rb_ref[0, h], F32)
    for k in range(1, REL_BUCKETS):
        val = jnp.where(dist >= BUCKET_THR[k - 1], rb_ref[k, h], val)
    keep = dist >= 0
    if toeplitz:
        keep = (keep & (dist < WINDOW)) | ((tile != 0) & (tile != BIAS_WIN_IDX))
    o_ref[0, 0] = val * LOG2E + jnp.where(keep, 0.0, MASK_VALUE)


def _bias_table(rel_bias, n_tiles, cols, stride, offset, toeplitz):
    kern = functools.partial(_bias_table_kernel, stride=stride, offset=offset, toeplitz=toeplitz)
    return pl.pallas_call(
        kern,
        grid=(NSA_HEADS, n_tiles),
        in_specs=[pl.BlockSpec(memory_space=pltpu.SMEM)],
        out_specs=pl.BlockSpec((1, 1, NSA_TQ, cols),
                               lambda h, t: (h // NSA_GROUP_SIZE, t, h % NSA_GROUP_SIZE, 0)),
        out_shape=jax.ShapeDtypeStruct((NSA_KV_GROUPS, n_tiles, NSA_ROWS, cols), F32),
        compiler_params=_cparams(("parallel", "parallel"), 8 << 20),
        name="bias_table",
    )(rel_bias)


def _proj_kernel(x_ref, g_ref, w_ref, cs_ref, o_ref, xn_ref, *, chunk):
    @pl.when(pl.program_id(1) == 0)
    def _normalise():
        def body(c, carry):
            rows = pl.ds(pl.multiple_of(c * chunk, chunk), chunk)
            x = x_ref[rows, :]
            ms = jnp.mean(x * x, axis=-1, keepdims=True)
            xn_ref[rows, :] = (x * lax.rsqrt(ms + RMS_EPS) * g_ref[...]).astype(BF16)
            return carry
        lax.fori_loop(0, x_ref.shape[0] // chunk, body, 0)

    acc = jnp.dot(xn_ref[...], w_ref[...], preferred_element_type=F32)
    o_ref[...] = (acc * cs_ref[...]).astype(o_ref.dtype)


def _rms_proj(x2d, g, w, col_scale, out_dtype, tm, tn, name):
    m, d = x2d.shape
    n = w.shape[1]
    assert m % tm == 0 and n % tn == 0 and col_scale.shape == (1, n)
    osz = jnp.dtype(out_dtype).itemsize
    vmem = 2 * tm * d * 4 + tm * d * 2 + 2 * d * tn * 2 + 2 * tm * tn * osz + tm * tn * 4 + (4 << 20)
    return pl.pallas_call(
        functools.partial(_proj_kernel, chunk=min(tm, 128)),
        grid=(m // tm, n // tn),
        in_specs=[pl.BlockSpec((tm, d), lambda i, j: (i, 0)),
                  pl.BlockSpec((1, d), lambda i, j: (0, 0)),
                  pl.BlockSpec((d, tn), lambda i, j: (0, j)),
                  pl.BlockSpec((1, tn), lambda i, j: (0, j))],
        out_specs=pl.BlockSpec((tm, tn), lambda i, j: (i, j)),
        out_shape=jax.ShapeDtypeStruct((m, n), out_dtype),
        scratch_shapes=[pltpu.VMEM((tm, d), BF16)],
        compiler_params=_cparams(("parallel", "arbitrary"), vmem),
        name=name,
    )(x2d, g, w, col_scale)


def _sb_kernel(q_ref, k_ref, v_ref, o_ref, acc_ref, run_ref):
    i = pl.program_id(2)
    tq, tk = q_ref.shape[1], SB_TK
    n_diag = tq // tk
    row = lax.broadcasted_iota(jnp.int32, (tq, tk), 0)
    col = lax.broadcasted_iota(jnp.int32, (tq, tk), 1)
    below = row > col
    suffix = jnp.where(below[:tk], 1.0, 0.0).astype(BF16)

    def tile(j, row0, diag):
        start = pl.multiple_of(j * tk, tk)
        rows = slice(row0, tq)
        causal = below[:tq - row0]
        n_rows = tq - row0
        log_betas, rests, runs = [], [], []
        for hh in range(SB_HEADS_PER_STEP):
            lanes = slice(hh * HEAD_DIM, (hh + 1) * HEAD_DIM)
            z = lax.dot_general(q_ref[0, rows, lanes], k_ref[0, pl.ds(start, tk), lanes], _NT,
                                preferred_element_type=F32)
            sp = jnp.log2(1.0 + jnp.exp2(-jnp.abs(z)))
            neg_rest = jnp.maximum(z, 0.0) + sp
            log_beta = z - neg_rest
            if diag:
                neg_rest = jnp.where(causal, neg_rest, 0.0)
            run = run_ref[hh, rows]
            log_betas.append(log_beta - run)
            rests.append(neg_rest.astype(BF16))
            run_ref[hh, rows] = run + jnp.sum(neg_rest, axis=-1, keepdims=True)
        later_all = jnp.dot(jnp.concatenate(rests, axis=0), suffix, preferred_element_type=F32)
        for hh in range(SB_HEADS_PER_STEP):
            lanes = slice(hh * HEAD_DIM, (hh + 1) * HEAD_DIM)
            a = jnp.exp2(log_betas[hh] - later_all[hh * n_rows:(hh + 1) * n_rows])
            if diag:
                a = jnp.where(causal, a, 0.0)
            acc_ref[hh, rows] += jnp.dot(a.astype(BF16), v_ref[0, pl.ds(start, tk), lanes],
                                         preferred_element_type=F32)

    acc_ref[...] = jnp.zeros_like(acc_ref)
    run_ref[...] = jnp.zeros_like(run_ref)
    for c in reversed(range(n_diag)):
        tile(i * n_diag + c, c * tk, True)

    def walk(carry):
        jj, _ = carry
        tile(i * n_diag - 1 - jj, 0, False)
        return jj + 1, jnp.min(run_ref[...])

    def alive(carry):
        jj, least_run = carry
        return (jj < i * n_diag) & (least_run < SB_DEAD_LOG2)
    lax.while_loop(alive, walk, (jnp.int32(0), jnp.min(run_ref[...])))
    for hh in range(SB_HEADS_PER_STEP):
        o_ref[0, :, hh * HEAD_DIM:(hh + 1) * HEAD_DIM] = acc_ref[hh]


def _sb_attention(proj_a, batch, seq):
    tq = min(SB_TQ, seq)
    hps = SB_HEADS_PER_STEP
    width = hps * HEAD_DIM
    assert seq % tq == 0 and tq % SB_TK == 0 and SB_HEADS % hps == 0
    kblk = SB_WIDTH // width
    vmem = 2 * seq * width * 2 + 6 * tq * width * 4 + hps * 12 * tq * SB_TK * 4 + (4 << 20)
    whole_seq = pl.Buffered(1)
    return pl.pallas_call(
        _sb_kernel,
        grid=(batch, SB_HEADS // hps, seq // tq),
        in_specs=[pl.BlockSpec((1, tq, width), lambda b, h, i: (b, i, h)),
                  pl.BlockSpec((1, seq, width), lambda b, h, i: (b, 0, kblk + h), pipeline_mode=whole_seq),
                  pl.BlockSpec((1, seq, width), lambda b, h, i: (b, 0, 2 * kblk + h), pipeline_mode=whole_seq)],
        out_specs=pl.BlockSpec((1, tq, width), lambda b, h, i: (b, i, h)),
        out_shape=jax.ShapeDtypeStruct((batch, seq, SB_WIDTH), F32),
        scratch_shapes=[pltpu.VMEM((hps, tq, HEAD_DIM), F32), pltpu.VMEM((hps, tq, 1), F32)],
        compiler_params=_cparams(("parallel", "parallel", "arbitrary"), vmem),
        name="sb_attention",
    )(proj_a, proj_a, proj_a)


def _compress_kernel(cv_ref, pos_ref, w1_ref, w2_ref, ko_ref, vo_ref):
    nbp = cv_ref.shape[1]
    tok = 2 * NSA_KV_WIDTH
    for which, out_ref in ((0, ko_ref), (1, vo_ref)):
        for g in range(NSA_KV_GROUPS):
            c0 = (which * NSA_KV_GROUPS + g) * HEAD_DIM
            chunks = jnp.concatenate(
                [cv_ref[0, :, l * tok + c0:l * tok + c0 + HEAD_DIM] for l in range(CMP_STRIDE)],
                axis=1).astype(F32)
            first = jnp.dot((chunks + pos_ref[which, 0]).astype(BF16), w1_ref[which, 0],
                            preferred_element_type=F32)
            second = jnp.dot((chunks + pos_ref[which, 1]).astype(BF16), w1_ref[which, 1],
                             preferred_element_type=F32)
            hid = jax.nn.gelu(first + pltpu.roll(second, nbp - 1, 0)).astype(BF16)
            out_ref[0, g] = jnp.dot(hid, w2_ref[which], preferred_element_type=F32).astype(BF16)


def _compress(cv, pos, w1, w2, batch, nbp):
    tokw = CMP_STRIDE * 2 * NSA_KV_WIDTH
    half = CMP_STRIDE * HEAD_DIM
    vmem = 2 * nbp * tokw * 2 + 2 * 4 * half * CMP_HIDDEN * 2 + 8 * nbp * half * 4 + (8 << 20)
    out_sds = [jax.ShapeDtypeStruct((batch, NSA_KV_GROUPS, nbp, HEAD_DIM), BF16)] * 2
    out_spec = [pl.BlockSpec((1, NSA_KV_GROUPS, nbp, HEAD_DIM), lambda b: (b, 0, 0, 0))] * 2
    return pl.pallas_call(
        _compress_kernel,
        grid=(batch,),
        in_specs=[pl.BlockSpec((1, nbp, tokw), lambda b: (b, 0, 0)),
                  pl.BlockSpec((2, 2, 1, half), lambda b: (0, 0, 0, 0)),
                  pl.BlockSpec((2, 2, half, CMP_HIDDEN), lambda b: (0, 0, 0, 0)),
                  pl.BlockSpec((2, CMP_HIDDEN, HEAD_DIM), lambda b: (0, 0, 0))],
        out_specs=out_spec,
        out_shape=out_sds,
        compiler_params=_cparams(("parallel",), vmem),
        name="kv_compress",
    )(cv, pos, w1, w2)


def _nsa_kernel(q_ref, kc_ref, vc_ref, ks_ref, vs_ref, kw_ref, vw_ref, gate_ref, tbl_ref, tblc_ref,
                ovt_ref, expt_ref, o_ref, acc_ref, m_ref):
    i = pl.program_id(1)
    tq, tk, rows = NSA_TQ, NSA_TK, NSA_ROWS
    gw = NSA_GROUP_SIZE * HEAD_DIM
    seq = expt_ref.shape[0]
    nsel = seq // SEL_BLOCK
    t0 = i * tq
    ovt = ovt_ref[...]
    jrow = lax.broadcasted_iota(jnp.int32, (nsel, tq), 0)
    tcol = t0 + lax.broadcasted_iota(jnp.int32, (nsel, tq), 1)
    cur = tcol // SEL_BLOCK
    valid = jrow * SEL_BLOCK <= tcol
    forced = (jrow == 0) | (jrow == cur) | (jrow == cur - 1)
    slab_row = lax.broadcasted_iota(jnp.int32, (8, tq), 0)

    def compressed_and_selection(g):
        qb = q_ref[0, :, g * gw:(g + 1) * gw]
        q4 = jnp.concatenate([qb[:, r * HEAD_DIM:(r + 1) * HEAD_DIM] for r in range(NSA_GROUP_SIZE)], axis=0)
        s_c = lax.dot_general(q4, kc_ref[0, g], _NT, preferred_element_type=F32) + tblc_ref[g, 0]
        reps = s_c.shape[1] // V7X_LANES
        m_c = jnp.broadcast_to(jnp.max(s_c, axis=-1, keepdims=True), (rows, V7X_LANES))
        e_c = jnp.exp2(s_c - jnp.concatenate([m_c] * reps, axis=1))
        l_c = jnp.broadcast_to(jnp.sum(e_c, axis=-1, keepdims=True), (rows, V7X_LANES))
        inv = jnp.where(m_c > 0.5 * MASK_VALUE, 1.0 / l_c, 0.0)
        p_c = e_c * jnp.concatenate([inv] * reps, axis=1)
        o_c = jnp.dot(p_c.astype(BF16), vc_ref[0, g], preferred_element_type=F32)

        p_sum = p_c[0:tq]
        for r in range(1, NSA_GROUP_SIZE):
            p_sum = p_sum + p_c[r * tq:(r + 1) * tq]
        p_hi = p_sum.astype(BF16)
        p_lo = (p_sum - p_hi.astype(F32)).astype(BF16)
        imp = (lax.dot_general(ovt, p_hi, _NT, preferred_element_type=F32)
               + lax.dot_general(ovt, p_lo, _NT, preferred_element_type=F32))[:nsel]
        score = jnp.where(valid, jnp.where(forced, SEL_FORCE, imp), -SEL_FORCE)
        score3 = score.reshape(nsel // 8, 8, tq)
        slabs = [score3[n] for n in range(nsel // 8)]
        ranks = [jnp.zeros((8, tq), F32) for _ in slabs]
        for other in range(nsel):
            srow = slabs[other // 8][other % 8:other % 8 + 1, :]
            for n, slab in enumerate(slabs):
                if 8 * n > other:
                    before = srow >= slab
                elif 8 * n + 7 <= other:
                    before = srow > slab
                else:
                    before = (srow > slab) | ((srow >= slab) & (slab_row > other - 8 * n))
                ranks[n] = ranks[n] + jnp.where(before, 1.0, 0.0)
        rank = jnp.stack(ranks, axis=0).reshape(nsel, tq)
        drop = jnp.where(rank < float(min(SEL_TOP_N, nsel)), 0.0, SEL_MASK)
        if nsel < V7X_LANES:
            drop = jnp.concatenate([drop, jnp.zeros((V7X_LANES - nsel, tq), F32)], axis=0)
        drop_q = drop.T.astype(BF16)
        q4_sel = jnp.concatenate([q4, jnp.concatenate([drop_q] * NSA_GROUP_SIZE, axis=0)], axis=1)
        return q4, q4_sel, o_c

    per_group = [compressed_and_selection(g) for g in range(NSA_KV_GROUPS)]

    selected = [(g, 0) for g in range(NSA_KV_GROUPS)]
    both = selected + [(g, 1) for g in range(NSA_KV_GROUPS)]
    ones = jnp.ones((tk, HEAD_DIM), BF16)

    def flash(jt, chains, first):
        start = pl.multiple_of(jt * tk, tk)
        d0 = i - jt
        for g, branch in chains:
            q4, q4_sel, _ = per_group[g]
            lanes = slice(g * HEAD_DIM, (g + 1) * HEAD_DIM)
            if branch == 1:
                bias = tbl_ref[g, 0 if first else jnp.where(d0 == WIN_LAST_D0, BIAS_WIN_IDX, d0)]
                s = lax.dot_general(q4, kw_ref[0, pl.ds(start, tk), lanes], _NT, preferred_element_type=F32)
                v = vw_ref[0, pl.ds(start, tk), lanes]
            else:
                bias = tbl_ref[g, 0 if first else jnp.minimum(d0, BIAS_CONST_IDX)]
                k_sel = jnp.concatenate([ks_ref[0, pl.ds(start, tk), lanes], expt_ref[pl.ds(start, tk), :]], axis=1)
                s = lax.dot_general(q4_sel, k_sel, _NT, preferred_element_type=F32)
                v = vs_ref[0, pl.ds(start, tk), lanes]
            s = s + bias
            v_aug = jnp.concatenate([v, ones], axis=1)
            m_tile = jnp.broadcast_to(jnp.max(s, axis=-1, keepdims=True), (rows, V7X_LANES))
            if first:
                m_new = m_tile
                p = jnp.exp2(s - jnp.concatenate([m_new] * (tk // V7X_LANES), axis=1))
                acc_ref[g, branch] = jnp.dot(p.astype(BF16), v_aug, preferred_element_type=F32)
            else:
                m_old = m_ref[g, branch]
                m_new = jnp.maximum(m_old, m_tile)
                p = jnp.exp2(s - jnp.concatenate([m_new] * (tk // V7X_LANES), axis=1))
                alpha = jnp.exp2(m_old - m_new)
                acc_ref[g, branch] = (jnp.concatenate([alpha] * 2, axis=1) * acc_ref[g, branch]
                                      + jnp.dot(p.astype(BF16), v_aug, preferred_element_type=F32))
            m_ref[g, branch] = m_new

    flash(i, both, True)

    @pl.when(i >= 2)
    def _():
        flash(i - 1, both, False)
        flash(i - 2, both, False)

    @pl.when(i == 1)
    def _():
        flash(0, both, False)

    def pair_body(p, carry):
        flash(i - 1 - 2 * p, selected, False)
        flash(i - 2 - 2 * p, selected, False)
        return carry
    lax.fori_loop(1, i >> 1, pair_body, 0)

    @pl.when((i >= 3) & ((i & 1) == 1))
    def _():
        flash(0, selected, False)

    for g in range(NSA_KV_GROUPS):
        gate = 1.0 / (1.0 + jnp.exp(-gate_ref[0, :, g * V7X_LANES:(g + 1) * V7X_LANES]))
        o_c = per_group[g][2]
        for r in range(NSA_GROUP_SIZE):
            rs = slice(r * tq, (r + 1) * tq)
            o_s = acc_ref[g, 0, rs, 0:HEAD_DIM] / acc_ref[g, 0, rs, HEAD_DIM:2 * HEAD_DIM]
            o_w = acc_ref[g, 1, rs, 0:HEAD_DIM] / acc_ref[g, 1, rs, HEAD_DIM:2 * HEAD_DIM]
            c = r * N_GATES
            out = gate[:, c:c + 1] * o_c[rs] + gate[:, c + 1:c + 2] * o_s + gate[:, c + 2:c + 3] * o_w
            col = (g * NSA_GROUP_SIZE + r) * HEAD_DIM
            o_ref[0, :, col:col + HEAD_DIM] = out


def _nsa_attention(proj_a, proj_b, kcmp, vcmp, tbl, tblc, ovt, expt, batch, seq, cols):
    tq, tk, rows = NSA_TQ, NSA_TK, NSA_ROWS
    nbp = kcmp.shape[2]
    n_tiles = tbl.shape[1]
    groups = NSA_KV_GROUPS
    qblk = cols["n_q"] // NSA_WIDTH
    gblk = cols["gates"] // (groups * V7X_LANES)
    assert cols["n_q"] % NSA_WIDTH == 0 and cols["gates"] % (groups * V7X_LANES) == 0

    def resident(shape, index_map):
        return pl.BlockSpec(shape, index_map, pipeline_mode=pl.Buffered(1))

    def kv_spec(name):
        assert cols[name] % NSA_KV_WIDTH == 0
        blk = cols[name] // NSA_KV_WIDTH
        return resident((1, seq, NSA_KV_WIDTH), lambda b, i: (b, 0, blk))

    cmp_spec = resident((1, groups, nbp, HEAD_DIM), lambda b, i: (b, 0, 0, 0))
    vmem = (4 * seq * NSA_KV_WIDTH * 2 + groups * n_tiles * rows * tk * 4 + 2 * groups * rows * nbp * 4
            + seq * V7X_LANES * 2 + 2 * groups * rows * (2 * HEAD_DIM + V7X_LANES) * 4
            + 4 * tq * (NSA_WIDTH + groups * V7X_LANES) * 4
            + 12 * rows * tk * 4 + (4 << 20))
    return pl.pallas_call(
        _nsa_kernel,
        grid=(batch, seq // tq),
        in_specs=[pl.BlockSpec((1, tq, NSA_WIDTH), lambda b, i: (b, i, qblk)),
                  cmp_spec, cmp_spec,
                  kv_spec("ks"), kv_spec("vs"), kv_spec("kw"), kv_spec("vw"),
                  pl.BlockSpec((1, tq, groups * V7X_LANES), lambda b, i: (b, i, gblk)),
                  resident((groups, n_tiles, rows, tk), lambda b, i: (0, 0, 0, 0)),
                  pl.BlockSpec((groups, 1, rows, nbp), lambda b, i: (0, i, 0, 0)),
                  resident((V7X_LANES, nbp), lambda b, i: (0, 0)),
                  resident((seq, V7X_LANES), lambda b, i: (0, 0))],
        out_specs=pl.BlockSpec((1, tq, NSA_WIDTH), lambda b, i: (b, i, 0)),
        out_shape=jax.ShapeDtypeStruct((batch, seq, NSA_WIDTH), F32),
        scratch_shapes=[pltpu.VMEM((groups, 2, rows, 2 * HEAD_DIM), F32),
                        pltpu.VMEM((groups, 2, rows, V7X_LANES), F32)],
        compiler_params=_cparams(("parallel", "arbitrary"), vmem),
        name="nsa_attention",
    )(proj_a, kcmp, vcmp, proj_a, proj_a, proj_a, proj_a, proj_b, tbl, tblc, ovt, expt)


def _selection_constants(seq, nbp):
    nb = seq // CMP_STRIDE - 1
    nsel = seq // SEL_BLOCK
    assert nsel <= V7X_LANES and nbp >= nb
    ci = np.arange(nbp)[None, :] * CMP_STRIDE
    sj = np.arange(V7X_LANES)[:, None] * SEL_BLOCK
    ovt = ((ci < sj + SEL_BLOCK) & (ci + CMP_BLOCK > sj) & (np.arange(nbp)[None, :] < nb)
           & (np.arange(V7X_LANES)[:, None] < nsel))
    expt = (np.arange(seq)[:, None] // SEL_BLOCK) == np.arange(V7X_LANES)[None, :]
    return jnp.asarray(ovt, BF16), jnp.asarray(expt, BF16)


def _nsa_branch(proj_a3, proj_b3, proj_cv3, cmp_pos, cmp_w1, cmp_w2, rel_bias, cols):
    batch, seq, _ = proj_a3.shape
    nbp = seq // CMP_STRIDE
    half = CMP_STRIDE * HEAD_DIM
    pos = jnp.stack(cmp_pos).reshape(2, 2, 1, half)
    w1 = jnp.stack(cmp_w1).reshape(2, 2, half, CMP_HIDDEN).astype(BF16)
    w2 = jnp.stack(cmp_w2).astype(BF16)
    kcmp, vcmp = _compress(proj_cv3, pos, w1, w2, batch, nbp)

    tbl = _bias_table(rel_bias, BIAS_TILES, NSA_TK, 1, 0, True)
    tblc = _bias_table(rel_bias, seq // NSA_TQ, nbp, CMP_STRIDE, -(CMP_BLOCK - 1), False)
    ovt, expt = _selection_constants(seq, nbp)
    return _nsa_attention(proj_a3, proj_b3, kcmp, vcmp, tbl, tblc, ovt, expt, batch, seq, cols)


def _out_kernel(osb_ref, zsb_ref, onsa_ref, znsa_ref, x_ref, gsb_ref, gnsa_ref, gfin_ref, wsb_ref, wnsa_ref,
                o_ref):
    def gated(o_r, z_r, g_r, rows):
        o = o_r[rows, :]
        y = o * lax.rsqrt(jnp.mean(o * o, axis=-1, keepdims=True) + RMS_EPS) * g_r[...]
        z = z_r[rows, :]
        return (y * (z * (1.0 / (1.0 + jnp.exp(-z))))).astype(BF16)

    for c in range(x_ref.shape[0] // OUT_CHUNK):
        rows = slice(c * OUT_CHUNK, (c + 1) * OUT_CHUNK)
        mixed = (jnp.dot(gated(osb_ref, zsb_ref, gsb_ref, rows), wsb_ref[...], preferred_element_type=F32)
                 + jnp.dot(gated(onsa_ref, znsa_ref, gnsa_ref, rows), wnsa_ref[...], preferred_element_type=F32))
        h = x_ref[rows, :] + mixed
        o_ref[rows, :] = h * lax.rsqrt(jnp.mean(h * h, axis=-1, keepdims=True) + RMS_EPS) * gfin_ref[...]


def _output_stage(o_sb, o_nsa, proj_b, x2d, g_sb, g_nsa, g_fin, w_out, tm):
    m, d = x2d.shape
    half = SB_WIDTH
    assert m % tm == 0 and tm % OUT_CHUNK == 0
    vmem = 2 * (4 * tm * half * 4 + 2 * tm * d * 4) + 2 * half * d * 2 + 6 * OUT_CHUNK * d * 4 + (4 << 20)
    row = lambda c: pl.BlockSpec((tm, half), lambda i: (i, c))
    vec = lambda n: pl.BlockSpec((1, n), lambda i: (0, 0))
    weights = pl.Buffered(1)
    return pl.pallas_call(
        _out_kernel,
        grid=(m // tm,),
        in_specs=[row(0), row(0), row(0), row(1),
                  pl.BlockSpec((tm, d), lambda i: (i, 0)),
                  vec(half), vec(half), vec(d),
                  pl.BlockSpec((half, d), lambda i: (0, 0), pipeline_mode=weights),
                  pl.BlockSpec((half, d), lambda i: (1, 0), pipeline_mode=weights)],
        out_specs=pl.BlockSpec((tm, d), lambda i: (i, 0)),
        out_shape=jax.ShapeDtypeStruct((m, d), F32),
        compiler_params=_cparams(("parallel",), vmem),
        name="output_stage",
    )(o_sb, proj_b, o_nsa, proj_b, x2d, g_sb, g_nsa, g_fin, w_out, w_out)


def kernel(x, norm_in, w_in, cmp_k_pos, cmp_k_w1, cmp_k_w2, cmp_v_pos, cmp_v_w1, cmp_v_w2,
           rel_bias, norm_sb, norm_nsa, w_out, norm_final):
    batch, seq, d_model = x.shape
    assert w_in.shape[0] == 1, "single-layer trunk: the final norm is fused into the output stage"
    m = batch * seq
    nbp = seq // CMP_STRIDE
    assert seq % NSA_TK == 0 and nbp % V7X_LANES == 0

    sizes = (SB_WIDTH,) * 4 + (NSA_WIDTH,) + (NSA_KV_WIDTH,) * 6 + (N_GATES * NSA_HEADS, NSA_WIDTH)
    off = np.concatenate([[0], np.cumsum(sizes)])
    (c_sbq, c_sbk, c_sbv, c_sbz, c_nq, c_kc, c_vc, c_ks, c_vs, c_kw, c_vw, c_gate, c_nz, c_end) = [int(o) for o in off]
    cols = {"n_q": 3 * SB_WIDTH, "ks": 3 * SB_WIDTH + NSA_WIDTH}
    cols["vs"] = cols["ks"] + NSA_KV_WIDTH
    cols["kw"] = cols["vs"] + NSA_KV_WIDTH
    cols["vw"] = cols["kw"] + NSA_KV_WIDTH
    cols["gates"] = 2 * SB_WIDTH
    gates_per_group = N_GATES * NSA_GROUP_SIZE

    w = w_in[0]
    w_a = jnp.concatenate([w[:, c_sbq:c_sbz], w[:, c_nq:c_kc], w[:, c_ks:c_gate]], axis=1).astype(BF16)
    w_cv = w[:, c_kc:c_ks].astype(BF16)
    gate_pad = jnp.zeros((d_model, V7X_LANES - gates_per_group), w.dtype)
    w_b = jnp.concatenate(
        [w[:, c_sbz:c_nq], w[:, c_nz:c_end]]
        + [blk for g in range(NSA_KV_GROUPS)
           for blk in (w[:, c_gate + g * gates_per_group:c_gate + (g + 1) * gates_per_group], gate_pad)],
        axis=1).astype(BF16)
    g_in = norm_in[0].reshape(1, d_model)
    x2d = x.reshape(m, d_model)

    q_scale = np.ones((1, w_a.shape[1]), np.float32)
    q_scale[:, c_sbq:c_sbq + SB_WIDTH] = SCALE * LOG2E
    q_scale[:, cols["n_q"]:cols["n_q"] + NSA_WIDTH] = SCALE * LOG2E
    proj_a = _rms_proj(x2d, g_in, w_a, jnp.asarray(q_scale), BF16, 1024, w_a.shape[1] // 4, "proj_attn")
    proj_cv = _rms_proj(x2d, g_in, w_cv, jnp.ones((1, w_cv.shape[1]), F32), BF16, 1024, w_cv.shape[1], "proj_cmp")
    proj_b = _rms_proj(x2d, g_in, w_b, jnp.ones((1, w_b.shape[1]), F32), F32, 512, w_b.shape[1], "proj_gate")

    proj_a3 = proj_a.reshape(batch, seq, -1)
    o_sb = _sb_attention(proj_a3, batch, seq)
    o_nsa = _nsa_branch(proj_a3, proj_b.reshape(batch, seq, -1), proj_cv.reshape(batch, nbp, -1),
                        (cmp_k_pos[0], cmp_v_pos[0]), (cmp_k_w1[0], cmp_v_w1[0]), (cmp_k_w2[0], cmp_v_w2[0]),
                        rel_bias, cols)

    out = _output_stage(o_sb.reshape(m, SB_WIDTH), o_nsa.reshape(m, NSA_WIDTH), proj_b, x2d,
                        norm_sb[0].reshape(1, SB_WIDTH), norm_nsa[0].reshape(1, NSA_WIDTH),
                        norm_final.reshape(1, d_model), w_out[0].astype(BF16), OUT_TM)
    return out.reshape(batch, seq, d_model)
```

```python
import functools
import math

import numpy as np
import jax
import jax.numpy as jnp
from jax import lax
from jax.experimental import pallas as pl
from jax.experimental.pallas import tpu as pltpu

F32 = jnp.float32
BF16 = jnp.bfloat16

HEAD_DIM = 128
SB_HEADS = 8
NSA_HEADS = 8
NSA_KV_GROUPS = 2
NSA_GROUP_SIZE = NSA_HEADS // NSA_KV_GROUPS
SB_WIDTH = SB_HEADS * HEAD_DIM
NSA_WIDTH = NSA_HEADS * HEAD_DIM
NSA_KV_WIDTH = NSA_KV_GROUPS * HEAD_DIM
N_GATES = 3
CMP_STRIDE = 16
CMP_BLOCK = 2 * CMP_STRIDE
CMP_HIDDEN = 256
SEL_BLOCK = 64
SEL_TOP_N = 16
WINDOW = 512
REL_BUCKETS = 32
REL_MAX_EXACT = 16
REL_MAX_DISTANCE = 1024
RMS_EPS = 1e-6
SEL_FORCE = 1e9
MASK_VALUE = -1e30
SEL_MASK = -float(2 ** 100)
SCALE = 1.0 / math.sqrt(HEAD_DIM)
LOG2E = math.log2(math.e)

V7X_LANES = 128
V7X_MXU_DIM = 256
V7X_BF16_SUBLANES = 16
V7X_VMEM_BYTES = 64 * 1024 * 1024

SB_TQ = 1 * V7X_MXU_DIM
SB_TK = V7X_MXU_DIM
SB_HEADS_PER_STEP = 8
SB_DEAD_LOG2 = 160.0
NSA_TQ = V7X_MXU_DIM
NSA_TK = V7X_MXU_DIM
NSA_ROWS = NSA_GROUP_SIZE * NSA_TQ
NSA_TRIP_SHIFT = 2
NSA_TILES_PER_TRIP = 1 << NSA_TRIP_SHIFT
OUT_TM = 2 * V7X_MXU_DIM
OUT_CHUNK = V7X_MXU_DIM

_NT = (((1,), (1,)), ((), ()))


def _bucket_thresholds():
    n = np.arange(0, 4 * REL_MAX_DISTANCE)
    nf = np.maximum(n, 1).astype(np.float64)
    large = REL_MAX_EXACT + (np.log(nf / REL_MAX_EXACT) / math.log(REL_MAX_DISTANCE / REL_MAX_EXACT)
                             * (REL_BUCKETS - REL_MAX_EXACT)).astype(np.int64)
    bucket = np.where(n < REL_MAX_EXACT, n, np.minimum(large, REL_BUCKETS - 1))
    assert np.all(np.diff(bucket) >= 0) and bucket[-1] == REL_BUCKETS - 1
    return [int(n[bucket >= k][0]) for k in range(1, REL_BUCKETS)]


BUCKET_THR = _bucket_thresholds()
LAST_THR = BUCKET_THR[-1]
BIAS_CONST_IDX = -(-(LAST_THR + NSA_TK - 1) // NSA_TQ)
BIAS_FAR = 1 << 20
WIN_LAST_D0 = 2
BIAS_WIN_IDX = BIAS_CONST_IDX + 1
BIAS_TILES = BIAS_CONST_IDX + 2
assert NSA_TQ == NSA_TK and NSA_TQ + NSA_TK - 1 < WINDOW <= WIN_LAST_D0 * NSA_TQ < BIAS_CONST_IDX * NSA_TQ


def _cparams(sem, vmem_bytes):
    return pltpu.CompilerParams(dimension_semantics=sem, vmem_limit_bytes=int(min(vmem_bytes, V7X_VMEM_BYTES)))


def _bias_table_kernel(rb_ref, o_ref, *, stride, offset, toeplitz):
    h = pl.program_id(0)
    tile = pl.program_id(1)
    rows, cols = o_ref.shape[2], o_ref.shape[3]
    base = tile * NSA_TQ + offset
    if toeplitz:
        base = jnp.where(tile == BIAS_CONST_IDX, BIAS_FAR,
                         jnp.where(tile == BIAS_WIN_IDX, WIN_LAST_D0 * NSA_TQ, base))
    ti = lax.broadcasted_iota(jnp.int32, (rows, cols), 0)
    cj = lax.broadcasted_iota(jnp.int32, (rows, cols), 1)
    dist = base + ti - stride * cj
    val = jnp.full((rows, cols), rb_ref[0, h], F32)
    for k in range(1, REL_BUCKETS):
        val = jnp.where(dist >= BUCKET_THR[k - 1], rb_ref[k, h], val)
    keep = dist >= 0
    if toeplitz:
        keep = (keep & (dist < WINDOW)) | ((tile != 0) & (tile != BIAS_WIN_IDX))
    o_ref[0, 0] = val * LOG2E + jnp.where(keep, 0.0, MASK_VALUE)


def _bias_table(rel_bias, n_tiles, cols, stride, offset, toeplitz):
    kern = functools.partial(_bias_table_kernel, stride=stride, offset=offset, toeplitz=toeplitz)
    return pl.pallas_call(
        kern,
        grid=(NSA_HEADS, n_tiles),
        in_specs=[pl.BlockSpec(memory_space=pltpu.SMEM)],
        out_specs=pl.BlockSpec((1, 1, NSA_TQ, cols),
                               lambda h, t: (h // NSA_GROUP_SIZE, t, h % NSA_GROUP_SIZE, 0)),
        out_shape=jax.ShapeDtypeStruct((NSA_KV_GROUPS, n_tiles, NSA_ROWS, cols), F32),
        compiler_params=_cparams(("parallel", "parallel"), 8 << 20),
        name="bias_table",
    )(rel_bias)


def _normalise_rows(x_ref, g_ref, xn_ref, chunk):
    def body(c, carry):
        rows = pl.ds(pl.multiple_of(c * chunk, chunk), chunk)
        x = x_ref[rows, :]
        ms = jnp.mean(x * x, axis=-1, keepdims=True)
        xn_ref[rows, :] = (x * lax.rsqrt(ms + RMS_EPS) * g_ref[...]).astype(BF16)
        return carry
    lax.fori_loop(0, x_ref.shape[0] // chunk, body, 0)


def _proj_kernel(x_ref, g_ref, w_ref, cs_ref, o_ref, xn_ref, *, chunk):
    @pl.when(pl.program_id(1) == 0)
    def _():
        _normalise_rows(x_ref, g_ref, xn_ref, chunk)

    acc = jnp.dot(xn_ref[...], w_ref[...], preferred_element_type=F32)
    o_ref[...] = (acc * cs_ref[...]).astype(o_ref.dtype)


def _proj_chunk_major_kernel(x_ref, g_ref, w_ref, o_ref, xn_ref, acc_ref, *, chunk):
    _normalise_rows(x_ref, g_ref, xn_ref, chunk)
    acc = jnp.dot(xn_ref[...], w_ref[...], preferred_element_type=F32)
    n = acc.shape[1]
    slabs = n // V7X_LANES
    for s in range(slabs):
        acc_ref[s] = acc[:, s * V7X_LANES:(s + 1) * V7X_LANES]
    n_chunks = o_ref.shape[0]
    for l in range(CMP_STRIDE):
        for s in range(slabs):
            tokens = acc_ref[s, pl.ds(l, n_chunks, stride=CMP_STRIDE), :]
            o_ref[:, l * n + s * V7X_LANES:l * n + (s + 1) * V7X_LANES] = tokens.astype(o_ref.dtype)


def _rms_proj_chunk_major(x2d, g, w, tm, name):
    m, d = x2d.shape
    n = w.shape[1]
    assert m % tm == 0 and tm % (CMP_STRIDE * V7X_BF16_SUBLANES) == 0 and n % V7X_LANES == 0
    vmem = 2 * tm * d * 4 + tm * d * 2 + 2 * d * n * 2 + 2 * tm * n * 2 + 2 * tm * n * 4 + (4 << 20)
    return pl.pallas_call(
        functools.partial(_proj_chunk_major_kernel, chunk=min(tm, 128)),
        grid=(m // tm,),
        in_specs=[pl.BlockSpec((tm, d), lambda i: (i, 0)),
                  pl.BlockSpec((1, d), lambda i: (0, 0)),
                  pl.BlockSpec((d, n), lambda i: (0, 0))],
        out_specs=pl.BlockSpec((tm // CMP_STRIDE, CMP_STRIDE * n), lambda i: (i, 0)),
        out_shape=jax.ShapeDtypeStruct((m // CMP_STRIDE, CMP_STRIDE * n), BF16),
        scratch_shapes=[pltpu.VMEM((tm, d), BF16), pltpu.VMEM((n // V7X_LANES, tm, V7X_LANES), F32)],
        compiler_params=_cparams(("parallel",), vmem),
        name=name,
    )(x2d, g, w)


def _rms_proj(x2d, g, w, col_scale, out_dtype, tm, tn, name):
    m, d = x2d.shape
    n = w.shape[1]
    assert m % tm == 0 and n % tn == 0 and col_scale.shape == (1, n)
    osz = jnp.dtype(out_dtype).itemsize
    vmem = 2 * tm * d * 4 + tm * d * 2 + 2 * d * tn * 2 + 2 * tm * tn * osz + tm * tn * 4 + (4 << 20)
    return pl.pallas_call(
        functools.partial(_proj_kernel, chunk=min(tm, 128)),
        grid=(m // tm, n // tn),
        in_specs=[pl.BlockSpec((tm, d), lambda i, j: (i, 0)),
                  pl.BlockSpec((1, d), lambda i, j: (0, 0)),
                  pl.BlockSpec((d, tn), lambda i, j: (0, j)),
                  pl.BlockSpec((1, tn), lambda i, j: (0, j))],
        out_specs=pl.BlockSpec((tm, tn), lambda i, j: (i, j)),
        out_shape=jax.ShapeDtypeStruct((m, n), out_dtype),
        scratch_shapes=[pltpu.VMEM((tm, d), BF16)],
        compiler_params=_cparams(("parallel", "arbitrary"), vmem),
        name=name,
    )(x2d, g, w, col_scale)


def _sb_kernel(q_ref, k_ref, v_ref, o_ref, acc_ref, run_ref):
    i = pl.program_id(2)
    tq, tk = q_ref.shape[1], SB_TK
    n_diag = tq // tk
    row = lax.broadcasted_iota(jnp.int32, (tq, tk), 0)
    col = lax.broadcasted_iota(jnp.int32, (tq, tk), 1)
    below = row > col
    suffix = jnp.where(below[:tk], 1.0, 0.0).astype(BF16)

    def tile(j, row0, diag):
        start = pl.multiple_of(j * tk, tk)
        rows = slice(row0, tq)
        causal = below[:tq - row0]
        n_rows = tq - row0
        log_betas, rests, runs = [], [], []
        for hh in range(SB_HEADS_PER_STEP):
            lanes = slice(hh * HEAD_DIM, (hh + 1) * HEAD_DIM)
            z = lax.dot_general(q_ref[0, rows, lanes], k_ref[0, pl.ds(start, tk), lanes], _NT,
                                preferred_element_type=F32)
            sp = jnp.log2(1.0 + jnp.exp2(-jnp.abs(z)))
            neg_rest = jnp.maximum(z, 0.0) + sp
            log_beta = z - neg_rest
            if diag:
                neg_rest = jnp.where(causal, neg_rest, 0.0)
            run = run_ref[hh, rows]
            log_betas.append(log_beta - run)
            rests.append(neg_rest.astype(BF16))
            run_ref[hh, rows] = run + jnp.sum(neg_rest, axis=-1, keepdims=True)
        later_all = jnp.dot(jnp.concatenate(rests, axis=0), suffix, preferred_element_type=F32)
        for hh in range(SB_HEADS_PER_STEP):
            lanes = slice(hh * HEAD_DIM, (hh + 1) * HEAD_DIM)
            a = jnp.exp2(log_betas[hh] - later_all[hh * n_rows:(hh + 1) * n_rows])
            if diag:
                a = jnp.where(causal, a, 0.0)
            acc_ref[hh, rows] += jnp.dot(a.astype(BF16), v_ref[0, pl.ds(start, tk), lanes],
                                         preferred_element_type=F32)

    acc_ref[...] = jnp.zeros_like(acc_ref)
    run_ref[...] = jnp.zeros_like(run_ref)
    for c in reversed(range(n_diag)):
        tile(i * n_diag + c, c * tk, True)

    def walk(carry):
        jj, _ = carry
        tile(i * n_diag - 1 - jj, 0, False)
        return jj + 1, jnp.min(run_ref[...])

    def alive(carry):
        jj, least_run = carry
        return (jj < i * n_diag) & (least_run < SB_DEAD_LOG2)
    lax.while_loop(alive, walk, (jnp.int32(0), jnp.min(run_ref[...])))
    for hh in range(SB_HEADS_PER_STEP):
        o_ref[0, :, hh * HEAD_DIM:(hh + 1) * HEAD_DIM] = acc_ref[hh]


def _sb_attention(proj_a, batch, seq):
    tq = min(SB_TQ, seq)
    hps = SB_HEADS_PER_STEP
    width = hps * HEAD_DIM
    assert seq % tq == 0 and tq % SB_TK == 0 and SB_HEADS % hps == 0
    kblk = SB_WIDTH // width
    vmem = 2 * seq * width * 2 + 6 * tq * width * 4 + hps * 12 * tq * SB_TK * 4 + (4 << 20)
    whole_seq = pl.Buffered(1)
    return pl.pallas_call(
        _sb_kernel,
        grid=(batch, SB_HEADS // hps, seq // tq),
        in_specs=[pl.BlockSpec((1, tq, width), lambda b, h, i: (b, i, h)),
                  pl.BlockSpec((1, seq, width), lambda b, h, i: (b, 0, kblk + h), pipeline_mode=whole_seq),
                  pl.BlockSpec((1, seq, width), lambda b, h, i: (b, 0, 2 * kblk + h), pipeline_mode=whole_seq)],
        out_specs=pl.BlockSpec((1, tq, width), lambda b, h, i: (b, i, h)),
        out_shape=jax.ShapeDtypeStruct((batch, seq, SB_WIDTH), F32),
        scratch_shapes=[pltpu.VMEM((hps, tq, HEAD_DIM), F32), pltpu.VMEM((hps, tq, 1), F32)],
        compiler_params=_cparams(("parallel", "parallel", "arbitrary"), vmem),
        name="sb_attention",
    )(proj_a, proj_a, proj_a)


def _compress_kernel(cv_ref, pos_ref, w1_ref, w2_ref, ko_ref, vo_ref):
    nbp = cv_ref.shape[1]
    tok = 2 * NSA_KV_WIDTH
    for which, out_ref in ((0, ko_ref), (1, vo_ref)):
        for g in range(NSA_KV_GROUPS):
            c0 = (which * NSA_KV_GROUPS + g) * HEAD_DIM
            chunks = jnp.concatenate(
                [cv_ref[0, :, l * tok + c0:l * tok + c0 + HEAD_DIM] for l in range(CMP_STRIDE)],
                axis=1).astype(F32)
            first = jnp.dot((chunks + pos_ref[which, 0]).astype(BF16), w1_ref[which, 0],
                            preferred_element_type=F32)
            second = jnp.dot((chunks + pos_ref[which, 1]).astype(BF16), w1_ref[which, 1],
                             preferred_element_type=F32)
            hid = jax.nn.gelu(first + pltpu.roll(second, nbp - 1, 0)).astype(BF16)
            out_ref[0, g] = jnp.dot(hid, w2_ref[which], preferred_element_type=F32).astype(BF16)


def _compress(cv, pos, w1, w2, batch, nbp):
    tokw = CMP_STRIDE * 2 * NSA_KV_WIDTH
    half = CMP_STRIDE * HEAD_DIM
    vmem = 2 * nbp * tokw * 2 + 2 * 4 * half * CMP_HIDDEN * 2 + 8 * nbp * half * 4 + (8 << 20)
    out_sds = [jax.ShapeDtypeStruct((batch, NSA_KV_GROUPS, nbp, HEAD_DIM), BF16)] * 2
    out_spec = [pl.BlockSpec((1, NSA_KV_GROUPS, nbp, HEAD_DIM), lambda b: (b, 0, 0, 0))] * 2
    return pl.pallas_call(
        _compress_kernel,
        grid=(batch,),
        in_specs=[pl.BlockSpec((1, nbp, tokw), lambda b: (b, 0, 0)),
                  pl.BlockSpec((2, 2, 1, half), lambda b: (0, 0, 0, 0)),
                  pl.BlockSpec((2, 2, half, CMP_HIDDEN), lambda b: (0, 0, 0, 0)),
                  pl.BlockSpec((2, CMP_HIDDEN, HEAD_DIM), lambda b: (0, 0, 0))],
        out_specs=out_spec,
        out_shape=out_sds,
        compiler_params=_cparams(("parallel",), vmem),
        name="kv_compress",
    )(cv, pos, w1, w2)


def _nsa_kernel(q_ref, kc_ref, vc_ref, ks_ref, vs_ref, kw_ref, vw_ref, gate_ref, tbl_ref, tblc_ref,
                ovt_ref, expt_ref, o_ref, acc_ref, m_ref):
    i = pl.program_id(1)
    tq, tk, rows = NSA_TQ, NSA_TK, NSA_ROWS
    gw = NSA_GROUP_SIZE * HEAD_DIM
    seq = expt_ref.shape[0]
    nsel = seq // SEL_BLOCK
    t0 = i * tq
    ovt = ovt_ref[...]
    jrow = lax.broadcasted_iota(jnp.int32, (nsel, tq), 0)
    tcol = t0 + lax.broadcasted_iota(jnp.int32, (nsel, tq), 1)
    cur = tcol // SEL_BLOCK
    valid = jrow * SEL_BLOCK <= tcol
    forced = (jrow == 0) | (jrow == cur) | (jrow == cur - 1)
    slab_row = lax.broadcasted_iota(jnp.int32, (8, tq), 0)

    def compressed_and_selection(g):
        qb = q_ref[0, :, g * gw:(g + 1) * gw]
        q4 = jnp.concatenate([qb[:, r * HEAD_DIM:(r + 1) * HEAD_DIM] for r in range(NSA_GROUP_SIZE)], axis=0)
        s_c = lax.dot_general(q4, kc_ref[0, g], _NT, preferred_element_type=F32) + tblc_ref[g, 0]
        reps = s_c.shape[1] // V7X_LANES
        m_c = jnp.broadcast_to(jnp.max(s_c, axis=-1, keepdims=True), (rows, V7X_LANES))
        e_c = jnp.exp2(s_c - jnp.concatenate([m_c] * reps, axis=1))
        l_c = jnp.broadcast_to(jnp.sum(e_c, axis=-1, keepdims=True), (rows, V7X_LANES))
        inv = jnp.where(m_c > 0.5 * MASK_VALUE, 1.0 / l_c, 0.0)
        p_c = e_c * jnp.concatenate([inv] * reps, axis=1)
        o_c = jnp.dot(p_c.astype(BF16), vc_ref[0, g], preferred_element_type=F32)

        p_sum = p_c[0:tq]
        for r in range(1, NSA_GROUP_SIZE):
            p_sum = p_sum + p_c[r * tq:(r + 1) * tq]
        p_hi = p_sum.astype(BF16)
        p_lo = (p_sum - p_hi.astype(F32)).astype(BF16)
        imp = (lax.dot_general(ovt, p_hi, _NT, preferred_element_type=F32)
               + lax.dot_general(ovt, p_lo, _NT, preferred_element_type=F32))[:nsel]
        score = jnp.where(valid, jnp.where(forced, SEL_FORCE, imp), -SEL_FORCE)
        score3 = score.reshape(nsel // 8, 8, tq)
        slabs = [score3[n] for n in range(nsel // 8)]
        ranks = [jnp.zeros((8, tq), F32) for _ in slabs]
        for other in range(nsel):
            srow = slabs[other // 8][other % 8:other % 8 + 1, :]
            for n, slab in enumerate(slabs):
                if 8 * n > other:
                    before = srow >= slab
                elif 8 * n + 7 <= other:
                    before = srow > slab
                else:
                    before = (srow > slab) | ((srow >= slab) & (slab_row > other - 8 * n))
                ranks[n] = ranks[n] + jnp.where(before, 1.0, 0.0)
        rank = jnp.stack(ranks, axis=0).reshape(nsel, tq)
        drop = jnp.where(rank < float(min(SEL_TOP_N, nsel)), 0.0, SEL_MASK)
        if nsel < V7X_LANES:
            drop = jnp.concatenate([drop, jnp.zeros((V7X_LANES - nsel, tq), F32)], axis=0)
        drop_q = drop.T.astype(BF16)
        q4_sel = jnp.concatenate([q4, jnp.concatenate([drop_q] * NSA_GROUP_SIZE, axis=0)], axis=1)
        return q4, q4_sel, o_c

    per_group = [compressed_and_selection(g) for g in range(NSA_KV_GROUPS)]

    selected = [(g, 0) for g in range(NSA_KV_GROUPS)]
    both = selected + [(g, 1) for g in range(NSA_KV_GROUPS)]
    ones = jnp.ones((tk, HEAD_DIM), BF16)

    def flash(jt, chains, first):
        start = pl.multiple_of(jt * tk, tk)
        d0 = i - jt
        for g, branch in chains:
            q4, q4_sel, _ = per_group[g]
            lanes = slice(g * HEAD_DIM, (g + 1) * HEAD_DIM)
            if branch == 1:
                bias = tbl_ref[g, 0 if first else jnp.where(d0 == WIN_LAST_D0, BIAS_WIN_IDX, d0)]
                s = lax.dot_general(q4, kw_ref[0, pl.ds(start, tk), lanes], _NT, preferred_element_type=F32)
                v = vw_ref[0, pl.ds(start, tk), lanes]
            else:
                bias = tbl_ref[g, 0 if first else jnp.minimum(d0, BIAS_CONST_IDX)]
                k_sel = jnp.concatenate([ks_ref[0, pl.ds(start, tk), lanes], expt_ref[pl.ds(start, tk), :]], axis=1)
                s = lax.dot_general(q4_sel, k_sel, _NT, preferred_element_type=F32)
                v = vs_ref[0, pl.ds(start, tk), lanes]
            s = s + bias
            v_aug = jnp.concatenate([v, ones], axis=1)
            m_tile = jnp.broadcast_to(jnp.max(s, axis=-1, keepdims=True), (rows, V7X_LANES))
            if first:
                m_new = m_tile
                p = jnp.exp2(s - jnp.concatenate([m_new] * (tk // V7X_LANES), axis=1))
                acc_ref[g, branch] = jnp.dot(p.astype(BF16), v_aug, preferred_element_type=F32)
            else:
                m_old = m_ref[g, branch]
                m_new = jnp.maximum(m_old, m_tile)
                p = jnp.exp2(s - jnp.concatenate([m_new] * (tk // V7X_LANES), axis=1))
                alpha = jnp.exp2(m_old - m_new)
                acc_ref[g, branch] = (jnp.concatenate([alpha] * 2, axis=1) * acc_ref[g, branch]
                                      + jnp.dot(p.astype(BF16), v_aug, preferred_element_type=F32))
            m_ref[g, branch] = m_new

    flash(i, both, True)

    @pl.when(i >= 2)
    def _():
        flash(i - 1, both, False)
        flash(i - 2, both, False)

    @pl.when(i == 1)
    def _():
        flash(0, both, False)

    remaining = jnp.maximum(i - 2, 0)
    trips = remaining >> NSA_TRIP_SHIFT
    rest = remaining & (NSA_TILES_PER_TRIP - 1)

    def trip_body(p, carry):
        for u in range(NSA_TILES_PER_TRIP):
            flash(i - 3 - NSA_TILES_PER_TRIP * p - u, selected, False)
        return carry
    lax.fori_loop(0, trips, trip_body, 0)

    for n_left in range(1, NSA_TILES_PER_TRIP):
        @pl.when(rest == n_left)
        def _(n_left=n_left):
            for jt in reversed(range(n_left)):
                flash(jt, selected, False)

    for g in range(NSA_KV_GROUPS):
        gate = 1.0 / (1.0 + jnp.exp(-gate_ref[0, :, g * V7X_LANES:(g + 1) * V7X_LANES]))
        o_c = per_group[g][2]
        for r in range(NSA_GROUP_SIZE):
            rs = slice(r * tq, (r + 1) * tq)
            o_s = acc_ref[g, 0, rs, 0:HEAD_DIM] / acc_ref[g, 0, rs, HEAD_DIM:2 * HEAD_DIM]
            o_w = acc_ref[g, 1, rs, 0:HEAD_DIM] / acc_ref[g, 1, rs, HEAD_DIM:2 * HEAD_DIM]
            c = r * N_GATES
            out = gate[:, c:c + 1] * o_c[rs] + gate[:, c + 1:c + 2] * o_s + gate[:, c + 2:c + 3] * o_w
            col = (g * NSA_GROUP_SIZE + r) * HEAD_DIM
            o_ref[0, :, col:col + HEAD_DIM] = out


def _nsa_attention(proj_a, proj_b, kcmp, vcmp, tbl, tblc, ovt, expt, batch, seq, cols):
    tq, tk, rows = NSA_TQ, NSA_TK, NSA_ROWS
    nbp = kcmp.shape[2]
    n_tiles = tbl.shape[1]
    groups = NSA_KV_GROUPS
    qblk = cols["n_q"] // NSA_WIDTH
    gblk = cols["gates"] // (groups * V7X_LANES)
    assert cols["n_q"] % NSA_WIDTH == 0 and cols["gates"] % (groups * V7X_LANES) == 0

    def resident(shape, index_map):
        return pl.BlockSpec(shape, index_map, pipeline_mode=pl.Buffered(1))

    def kv_spec(name):
        assert cols[name] % NSA_KV_WIDTH == 0
        blk = cols[name] // NSA_KV_WIDTH
        return resident((1, seq, NSA_KV_WIDTH), lambda b, i: (b, 0, blk))

    cmp_spec = resident((1, groups, nbp, HEAD_DIM), lambda b, i: (b, 0, 0, 0))
    vmem = (4 * seq * NSA_KV_WIDTH * 2 + groups * n_tiles * rows * tk * 4 + 2 * groups * rows * nbp * 4
            + seq * V7X_LANES * 2 + 2 * groups * rows * (2 * HEAD_DIM + V7X_LANES) * 4
            + 4 * tq * (NSA_WIDTH + groups * V7X_LANES) * 4
            + 12 * rows * tk * 4 + (4 << 20))
    return pl.pallas_call(
        _nsa_kernel,
        grid=(batch, seq // tq),
        in_specs=[pl.BlockSpec((1, tq, NSA_WIDTH), lambda b, i: (b, i, qblk)),
                  cmp_spec, cmp_spec,
                  kv_spec("ks"), kv_spec("vs"), kv_spec("kw"), kv_spec("vw"),
                  pl.BlockSpec((1, tq, groups * V7X_LANES), lambda b, i: (b, i, gblk)),
                  resident((groups, n_tiles, rows, tk), lambda b, i: (0, 0, 0, 0)),
                  pl.BlockSpec((groups, 1, rows, nbp), lambda b, i: (0, i, 0, 0)),
                  resident((V7X_LANES, nbp), lambda b, i: (0, 0)),
                  resident((seq, V7X_LANES), lambda b, i: (0, 0))],
        out_specs=pl.BlockSpec((1, tq, NSA_WIDTH), lambda b, i: (b, i, 0)),
        out_shape=jax.ShapeDtypeStruct((batch, seq, NSA_WIDTH), F32),
        scratch_shapes=[pltpu.VMEM((groups, 2, rows, 2 * HEAD_DIM), F32),
                        pltpu.VMEM((groups, 2, rows, V7X_LANES), F32)],
        compiler_params=_cparams(("parallel", "arbitrary"), vmem),
        name="nsa_attention",
    )(proj_a, kcmp, vcmp, proj_a, proj_a, proj_a, proj_a, proj_b, tbl, tblc, ovt, expt)


def _selection_constants(seq, nbp):
    nb = seq // CMP_STRIDE - 1
    nsel = seq // SEL_BLOCK
    assert nsel <= V7X_LANES and nbp >= nb
    ci = np.arange(nbp)[None, :] * CMP_STRIDE
    sj = np.arange(V7X_LANES)[:, None] * SEL_BLOCK
    ovt = ((ci < sj + SEL_BLOCK) & (ci + CMP_BLOCK > sj) & (np.arange(nbp)[None, :] < nb)
           & (np.arange(V7X_LANES)[:, None] < nsel))
    expt = (np.arange(seq)[:, None] // SEL_BLOCK) == np.arange(V7X_LANES)[None, :]
    return jnp.asarray(ovt, BF16), jnp.asarray(expt, BF16)


def _nsa_branch(proj_a3, proj_b3, proj_cv3, cmp_pos, cmp_w1, cmp_w2, rel_bias, cols):
    batch, seq, _ = proj_a3.shape
    nbp = seq // CMP_STRIDE
    half = CMP_STRIDE * HEAD_DIM
    pos = jnp.stack(cmp_pos).reshape(2, 2, 1, half)
    w1 = jnp.stack(cmp_w1).reshape(2, 2, half, CMP_HIDDEN).astype(BF16)
    w2 = jnp.stack(cmp_w2).astype(BF16)
    kcmp, vcmp = _compress(proj_cv3, pos, w1, w2, batch, nbp)

    tbl = _bias_table(rel_bias, BIAS_TILES, NSA_TK, 1, 0, True)
    tblc = _bias_table(rel_bias, seq // NSA_TQ, nbp, CMP_STRIDE, -(CMP_BLOCK - 1), False)
    ovt, expt = _selection_constants(seq, nbp)
    return _nsa_attention(proj_a3, proj_b3, kcmp, vcmp, tbl, tblc, ovt, expt, batch, seq, cols)


def _out_kernel(osb_ref, zsb_ref, onsa_ref, znsa_ref, x_ref, gsb_ref, gnsa_ref, gfin_ref, wsb_ref, wnsa_ref,
                o_ref):
    def gated(o_r, z_r, g_r, rows):
        o = o_r[rows, :]
        y = o * lax.rsqrt(jnp.mean(o * o, axis=-1, keepdims=True) + RMS_EPS) * g_r[...]
        z = z_r[rows, :]
        return (y * (z * (1.0 / (1.0 + jnp.exp(-z))))).astype(BF16)

    for c in range(x_ref.shape[0] // OUT_CHUNK):
        rows = slice(c * OUT_CHUNK, (c + 1) * OUT_CHUNK)
        mixed = (jnp.dot(gated(osb_ref, zsb_ref, gsb_ref, rows), wsb_ref[...], preferred_element_type=F32)
                 + jnp.dot(gated(onsa_ref, znsa_ref, gnsa_ref, rows), wnsa_ref[...], preferred_element_type=F32))
        h = x_ref[rows, :] + mixed
        o_ref[rows, :] = h * lax.rsqrt(jnp.mean(h * h, axis=-1, keepdims=True) + RMS_EPS) * gfin_ref[...]


def _output_stage(o_sb, o_nsa, proj_b, x2d, g_sb, g_nsa, g_fin, w_out, tm):
    m, d = x2d.shape
    half = SB_WIDTH
    assert m % tm == 0 and tm % OUT_CHUNK == 0
    vmem = 2 * (4 * tm * half * 4 + 2 * tm * d * 4) + 2 * half * d * 2 + 6 * OUT_CHUNK * d * 4 + (4 << 20)
    row = lambda c: pl.BlockSpec((tm, half), lambda i: (i, c))
    vec = lambda n: pl.BlockSpec((1, n), lambda i: (0, 0))
    weights = pl.Buffered(1)
    return pl.pallas_call(
        _out_kernel,
        grid=(m // tm,),
        in_specs=[row(0), row(0), row(0), row(1),
                  pl.BlockSpec((tm, d), lambda i: (i, 0)),
                  vec(half), vec(half), vec(d),
                  pl.BlockSpec((half, d), lambda i: (0, 0), pipeline_mode=weights),
                  pl.BlockSpec((half, d), lambda i: (1, 0), pipeline_mode=weights)],
        out_specs=pl.BlockSpec((tm, d), lambda i: (i, 0)),
        out_shape=jax.ShapeDtypeStruct((m, d), F32),
        compiler_params=_cparams(("parallel",), vmem),
        name="output_stage",
    )(o_sb, proj_b, o_nsa, proj_b, x2d, g_sb, g_nsa, g_fin, w_out, w_out)


def kernel(x, norm_in, w_in, cmp_k_pos, cmp_k_w1, cmp_k_w2, cmp_v_pos, cmp_v_w1, cmp_v_w2,
           rel_bias, norm_sb, norm_nsa, w_out, norm_final):
    batch, seq, d_model = x.shape
    assert w_in.shape[0] == 1, "single-layer trunk: the final norm is fused into the output stage"
    m = batch * seq
    nbp = seq // CMP_STRIDE
    assert seq % NSA_TK == 0 and nbp % V7X_LANES == 0

    sizes = (SB_WIDTH,) * 4 + (NSA_WIDTH,) + (NSA_KV_WIDTH,) * 6 + (N_GATES * NSA_HEADS, NSA_WIDTH)
    off = np.concatenate([[0], np.cumsum(sizes)])
    (c_sbq, c_sbk, c_sbv, c_sbz, c_nq, c_kc, c_vc, c_ks, c_vs, c_kw, c_vw, c_gate, c_nz, c_end) = [int(o) for o in off]
    cols = {"n_q": 3 * SB_WIDTH, "ks": 3 * SB_WIDTH + NSA_WIDTH}
    cols["vs"] = cols["ks"] + NSA_KV_WIDTH
    cols["kw"] = cols["vs"] + NSA_KV_WIDTH
    cols["vw"] = cols["kw"] + NSA_KV_WIDTH
    cols["gates"] = 2 * SB_WIDTH
    gates_per_group = N_GATES * NSA_GROUP_SIZE

    w = w_in[0]
    w_a = jnp.concatenate([w[:, c_sbq:c_sbz], w[:, c_nq:c_kc], w[:, c_ks:c_gate]], axis=1).astype(BF16)
    w_cv = w[:, c_kc:c_ks].astype(BF16)
    gate_pad = jnp.zeros((d_model, V7X_LANES - gates_per_group), w.dtype)
    w_b = jnp.concatenate(
        [w[:, c_sbz:c_nq], w[:, c_nz:c_end]]
        + [blk for g in range(NSA_KV_GROUPS)
           for blk in (w[:, c_gate + g * gates_per_group:c_gate + (g + 1) * gates_per_group], gate_pad)],
        axis=1).astype(BF16)
    g_in = norm_in[0].reshape(1, d_model)
    x2d = x.reshape(m, d_model)

    q_scale = np.ones((1, w_a.shape[1]), np.float32)
    q_scale[:, c_sbq:c_sbq + SB_WIDTH] = SCALE * LOG2E
    q_scale[:, cols["n_q"]:cols["n_q"] + NSA_WIDTH] = SCALE * LOG2E
    proj_a = _rms_proj(x2d, g_in, w_a, jnp.asarray(q_scale), BF16, 1024, w_a.shape[1] // 4, "proj_attn")
    proj_cv = _rms_proj_chunk_major(x2d, g_in, w_cv, 1024, "proj_cmp")
    proj_b = _rms_proj(x2d, g_in, w_b, jnp.ones((1, w_b.shape[1]), F32), F32, 512, w_b.shape[1], "proj_gate")

    proj_a3 = proj_a.reshape(batch, seq, -1)
    o_sb = _sb_attention(proj_a3, batch, seq)
    o_nsa = _nsa_branch(proj_a3, proj_b.reshape(batch, seq, -1), proj_cv.reshape(batch, nbp, -1),
                        (cmp_k_pos[0], cmp_v_pos[0]), (cmp_k_w1[0], cmp_v_w1[0]), (cmp_k_w2[0], cmp_v_w2[0]),
                        rel_bias, cols)

    out = _output_stage(o_sb.reshape(m, SB_WIDTH), o_nsa.reshape(m, NSA_WIDTH), proj_b, x2d,
                        norm_sb[0].reshape(1, SB_WIDTH), norm_nsa[0].reshape(1, NSA_WIDTH),
                        norm_final.reshape(1, d_model), w_out[0].astype(BF16), OUT_TM)
    return out.reshape(batch, seq, d_model)
```

```python
import functools
import math

import numpy as np
import jax
import jax.numpy as jnp
from jax import lax
from jax.experimental import pallas as pl
from jax.experimental.pallas import tpu as pltpu

F32 = jnp.float32
BF16 = jnp.bfloat16

HEAD_DIM = 128
SB_HEADS = 8
NSA_HEADS = 8
NSA_KV_GROUPS = 2
NSA_GROUP_SIZE = NSA_HEADS // NSA_KV_GROUPS
SB_WIDTH = SB_HEADS * HEAD_DIM
NSA_WIDTH = NSA_HEADS * HEAD_DIM
NSA_KV_WIDTH = NSA_KV_GROUPS * HEAD_DIM
N_GATES = 3
CMP_STRIDE = 16
CMP_BLOCK = 2 * CMP_STRIDE
CMP_HIDDEN = 256
SEL_BLOCK = 64
SEL_TOP_N = 16
WINDOW = 512
REL_BUCKETS = 32
REL_MAX_EXACT = 16
REL_MAX_DISTANCE = 1024
RMS_EPS = 1e-6
SEL_FORCE = 1e9
MASK_VALUE = -1e30
SEL_MASK = -float(2 ** 100)
SCALE = 1.0 / math.sqrt(HEAD_DIM)
LOG2E = math.log2(math.e)

V7X_LANES = 128
V7X_MXU_DIM = 256
V7X_BF16_SUBLANES = 16
V7X_VMEM_BYTES = 64 * 1024 * 1024

SB_TQ = 1 * V7X_MXU_DIM
SB_TK = V7X_MXU_DIM
SB_HEADS_PER_STEP = 8
SB_DEAD_LOG2 = 160.0
NSA_TQ = V7X_MXU_DIM
NSA_TK = V7X_MXU_DIM
NSA_ROWS = NSA_GROUP_SIZE * NSA_TQ
NSA_TRIP_SHIFT = 2
NSA_TILES_PER_TRIP = 1 << NSA_TRIP_SHIFT
OUT_TM = 2 * V7X_MXU_DIM
OUT_CHUNK = V7X_MXU_DIM

_NT = (((1,), (1,)), ((), ()))


def _bucket_thresholds():
    n = np.arange(0, 4 * REL_MAX_DISTANCE)
    nf = np.maximum(n, 1).astype(np.float64)
    large = REL_MAX_EXACT + (np.log(nf / REL_MAX_EXACT) / math.log(REL_MAX_DISTANCE / REL_MAX_EXACT)
                             * (REL_BUCKETS - REL_MAX_EXACT)).astype(np.int64)
    bucket = np.where(n < REL_MAX_EXACT, n, np.minimum(large, REL_BUCKETS - 1))
    assert np.all(np.diff(bucket) >= 0) and bucket[-1] == REL_BUCKETS - 1
    return [int(n[bucket >= k][0]) for k in range(1, REL_BUCKETS)]


BUCKET_THR = _bucket_thresholds()
LAST_THR = BUCKET_THR[-1]
BIAS_CONST_IDX = -(-(LAST_THR + NSA_TK - 1) // NSA_TQ)
BIAS_FAR = 1 << 20
WIN_LAST_D0 = 2
BIAS_WIN_IDX = BIAS_CONST_IDX + 1
BIAS_TILES = BIAS_CONST_IDX + 2
assert NSA_TQ == NSA_TK and NSA_TQ + NSA_TK - 1 < WINDOW <= WIN_LAST_D0 * NSA_TQ < BIAS_CONST_IDX * NSA_TQ


def _cparams(sem, vmem_bytes):
    return pltpu.CompilerParams(dimension_semantics=sem, vmem_limit_bytes=int(min(vmem_bytes, V7X_VMEM_BYTES)))


def _bias_table_kernel(rb_ref, o_ref, *, stride, offset, toeplitz):
    h = pl.program_id(0)
    tile = pl.program_id(1)
    rows, cols = o_ref.shape[2], o_ref.shape[3]
    base = tile * NSA_TQ + offset
    if toeplitz:
        base = jnp.where(tile == BIAS_CONST_IDX, BIAS_FAR,
                         jnp.where(tile == BIAS_WIN_IDX, WIN_LAST_D0 * NSA_TQ, base))
    ti = lax.broadcasted_iota(jnp.int32, (rows, cols), 0)
    cj = lax.broadcasted_iota(jnp.int32, (rows, cols), 1)
    dist = base + ti - stride * cj
    val = jnp.full((rows, cols), rb_ref[0, h], F32)
    for k in range(1, REL_BUCKETS):
        val = jnp.where(dist >= BUCKET_THR[k - 1], rb_ref[k, h], val)
    keep = dist >= 0
    if toeplitz:
        keep = (keep & (dist < WINDOW)) | ((tile != 0) & (tile != BIAS_WIN_IDX))
    o_ref[0, 0] = val * LOG2E + jnp.where(keep, 0.0, MASK_VALUE)


def _bias_table(rel_bias, n_tiles, cols, stride, offset, toeplitz):
    kern = functools.partial(_bias_table_kernel, stride=stride, offset=offset, toeplitz=toeplitz)
    return pl.pallas_call(
        kern,
        grid=(NSA_HEADS, n_tiles),
        in_specs=[pl.BlockSpec(memory_space=pltpu.SMEM)],
        out_specs=pl.BlockSpec((1, 1, NSA_TQ, cols),
                               lambda h, t: (h // NSA_GROUP_SIZE, t, h % NSA_GROUP_SIZE, 0)),
        out_shape=jax.ShapeDtypeStruct((NSA_KV_GROUPS, n_tiles, NSA_ROWS, cols), F32),
        compiler_params=_cparams(("parallel", "parallel"), 8 << 20),
        name="bias_table",
    )(rel_bias)


def _normalise_rows(x_ref, g_ref, xn_ref, chunk):
    def body(c, carry):
        rows = pl.ds(pl.multiple_of(c * chunk, chunk), chunk)
        x = x_ref[rows, :]
        ms = jnp.mean(x * x, axis=-1, keepdims=True)
        xn_ref[rows, :] = (x * lax.rsqrt(ms + RMS_EPS) * g_ref[...]).astype(BF16)
        return carry
    lax.fori_loop(0, x_ref.shape[0] // chunk, body, 0)


def _proj_kernel(x_ref, g_ref, w_ref, cs_ref, o_ref, xn_ref, *, chunk):
    @pl.when(pl.program_id(1) == 0)
    def _():
        _normalise_rows(x_ref, g_ref, xn_ref, chunk)

    acc = jnp.dot(xn_ref[...], w_ref[...], preferred_element_type=F32)
    o_ref[...] = (acc * cs_ref[...]).astype(o_ref.dtype)


def _plain_proj_kernel(xn_ref, w_ref, o_ref):
    o_ref[...] = jnp.dot(xn_ref[...], w_ref[...], preferred_element_type=F32).astype(o_ref.dtype)


def _proj_chunk_major_kernel(xn_ref, w_ref, o_ref, acc_ref):
    acc = jnp.dot(xn_ref[...], w_ref[...], preferred_element_type=F32)
    n = acc.shape[1]
    slabs = n // V7X_LANES
    for s in range(slabs):
        acc_ref[s] = acc[:, s * V7X_LANES:(s + 1) * V7X_LANES]
    n_chunks = o_ref.shape[0]
    for l in range(CMP_STRIDE):
        for s in range(slabs):
            tokens = acc_ref[s, pl.ds(l, n_chunks, stride=CMP_STRIDE), :]
            o_ref[:, l * n + s * V7X_LANES:l * n + (s + 1) * V7X_LANES] = tokens.astype(o_ref.dtype)


def _proj_chunk_major(xn, w, tm, name):
    m, d = xn.shape
    n = w.shape[1]
    assert m % tm == 0 and tm % (CMP_STRIDE * V7X_BF16_SUBLANES) == 0 and n % V7X_LANES == 0
    vmem = 2 * tm * d * 2 + 2 * d * n * 2 + 2 * tm * n * 2 + 2 * tm * n * 4 + (4 << 20)
    return pl.pallas_call(
        _proj_chunk_major_kernel,
        grid=(m // tm,),
        in_specs=[pl.BlockSpec((tm, d), lambda i: (i, 0)),
                  pl.BlockSpec((d, n), lambda i: (0, 0))],
        out_specs=pl.BlockSpec((tm // CMP_STRIDE, CMP_STRIDE * n), lambda i: (i, 0)),
        out_shape=jax.ShapeDtypeStruct((m // CMP_STRIDE, CMP_STRIDE * n), BF16),
        scratch_shapes=[pltpu.VMEM((n // V7X_LANES, tm, V7X_LANES), F32)],
        compiler_params=_cparams(("parallel",), vmem),
        name=name,
    )(xn, w)


def _plain_proj(xn, w, out_dtype, tm, name):
    m, d = xn.shape
    n = w.shape[1]
    assert m % tm == 0
    osz = jnp.dtype(out_dtype).itemsize
    vmem = 2 * tm * d * 2 + 2 * d * n * 2 + 2 * tm * n * osz + tm * n * 4 + (4 << 20)
    return pl.pallas_call(
        _plain_proj_kernel,
        grid=(m // tm,),
        in_specs=[pl.BlockSpec((tm, d), lambda i: (i, 0)),
                  pl.BlockSpec((d, n), lambda i: (0, 0))],
        out_specs=pl.BlockSpec((tm, n), lambda i: (i, 0)),
        out_shape=jax.ShapeDtypeStruct((m, n), out_dtype),
        compiler_params=_cparams(("parallel",), vmem),
        name=name,
    )(xn, w)


def _rms_proj(x2d, g, w, col_scale, out_dtype, tm, tn, name):
    m, d = x2d.shape
    n = w.shape[1]
    assert m % tm == 0 and n % tn == 0 and col_scale.shape == (1, n)
    osz = jnp.dtype(out_dtype).itemsize
    vmem = 2 * tm * d * 4 + 2 * tm * d * 2 + 2 * d * tn * 2 + 2 * tm * tn * osz + tm * tn * 4 + (4 << 20)
    return pl.pallas_call(
        functools.partial(_proj_kernel, chunk=min(tm, 128)),
        grid=(m // tm, n // tn),
        in_specs=[pl.BlockSpec((tm, d), lambda i, j: (i, 0)),
                  pl.BlockSpec((1, d), lambda i, j: (0, 0)),
                  pl.BlockSpec((d, tn), lambda i, j: (0, j)),
                  pl.BlockSpec((1, tn), lambda i, j: (0, j))],
        out_specs=[pl.BlockSpec((tm, tn), lambda i, j: (i, j)),
                   pl.BlockSpec((tm, d), lambda i, j: (i, 0))],
        out_shape=[jax.ShapeDtypeStruct((m, n), out_dtype), jax.ShapeDtypeStruct((m, d), BF16)],
        compiler_params=_cparams(("parallel", "arbitrary"), vmem),
        name=name,
    )(x2d, g, w, col_scale)


def _sb_kernel(q_ref, k_ref, v_ref, o_ref, acc_ref, run_ref):
    i = pl.program_id(2)
    tq, tk = q_ref.shape[1], SB_TK
    n_diag = tq // tk
    row = lax.broadcasted_iota(jnp.int32, (tq, tk), 0)
    col = lax.broadcasted_iota(jnp.int32, (tq, tk), 1)
    below = row > col
    suffix = jnp.where(below[:tk], 1.0, 0.0).astype(BF16)

    def tile(j, row0, diag):
        start = pl.multiple_of(j * tk, tk)
        rows = slice(row0, tq)
        causal = below[:tq - row0]
        n_rows = tq - row0
        log_betas, rests, runs = [], [], []
        for hh in range(SB_HEADS_PER_STEP):
            lanes = slice(hh * HEAD_DIM, (hh + 1) * HEAD_DIM)
            z = lax.dot_general(q_ref[0, rows, lanes], k_ref[0, pl.ds(start, tk), lanes], _NT,
                                preferred_element_type=F32)
            sp = jnp.log2(1.0 + jnp.exp2(-jnp.abs(z)))
            neg_rest = jnp.maximum(z, 0.0) + sp
            log_beta = z - neg_rest
            if diag:
                neg_rest = jnp.where(causal, neg_rest, 0.0)
            run = run_ref[hh, rows]
            log_betas.append(log_beta - run)
            rests.append(neg_rest.astype(BF16))
            run_ref[hh, rows] = run + jnp.sum(neg_rest, axis=-1, keepdims=True)
        later_all = jnp.dot(jnp.concatenate(rests, axis=0), suffix, preferred_element_type=F32)
        for hh in range(SB_HEADS_PER_STEP):
            lanes = slice(hh * HEAD_DIM, (hh + 1) * HEAD_DIM)
            a = jnp.exp2(log_betas[hh] - later_all[hh * n_rows:(hh + 1) * n_rows])
            if diag:
                a = jnp.where(causal, a, 0.0)
            acc_ref[hh, rows] += jnp.dot(a.astype(BF16), v_ref[0, pl.ds(start, tk), lanes],
                                         preferred_element_type=F32)

    acc_ref[...] = jnp.zeros_like(acc_ref)
    run_ref[...] = jnp.zeros_like(run_ref)
    for c in reversed(range(n_diag)):
        tile(i * n_diag + c, c * tk, True)

    def walk(carry):
        jj, _ = carry
        tile(i * n_diag - 1 - jj, 0, False)
        return jj + 1, jnp.min(run_ref[...])

    def alive(carry):
        jj, least_run = carry
        return (jj < i * n_diag) & (least_run < SB_DEAD_LOG2)
    lax.while_loop(alive, walk, (jnp.int32(0), jnp.min(run_ref[...])))
    for hh in range(SB_HEADS_PER_STEP):
        o_ref[0, :, hh * HEAD_DIM:(hh + 1) * HEAD_DIM] = acc_ref[hh]


def _sb_attention(proj_a, batch, seq):
    tq = min(SB_TQ, seq)
    hps = SB_HEADS_PER_STEP
    width = hps * HEAD_DIM
    assert seq % tq == 0 and tq % SB_TK == 0 and SB_HEADS % hps == 0
    kblk = SB_WIDTH // width
    vmem = 2 * seq * width * 2 + 6 * tq * width * 4 + hps * 12 * tq * SB_TK * 4 + (4 << 20)
    whole_seq = pl.Buffered(1)
    return pl.pallas_call(
        _sb_kernel,
        grid=(batch, SB_HEADS // hps, seq // tq),
        in_specs=[pl.BlockSpec((1, tq, width), lambda b, h, i: (b, i, h)),
                  pl.BlockSpec((1, seq, width), lambda b, h, i: (b, 0, kblk + h), pipeline_mode=whole_seq),
                  pl.BlockSpec((1, seq, width), lambda b, h, i: (b, 0, 2 * kblk + h), pipeline_mode=whole_seq)],
        out_specs=pl.BlockSpec((1, tq, width), lambda b, h, i: (b, i, h)),
        out_shape=jax.ShapeDtypeStruct((batch, seq, SB_WIDTH), F32),
        scratch_shapes=[pltpu.VMEM((hps, tq, HEAD_DIM), F32), pltpu.VMEM((hps, tq, 1), F32)],
        compiler_params=_cparams(("parallel", "parallel", "arbitrary"), vmem),
        name="sb_attention",
    )(proj_a, proj_a, proj_a)


def _compress_kernel(cv_ref, pos_ref, w1_ref, w2_ref, ko_ref, vo_ref):
    nbp = cv_ref.shape[1]
    tok = 2 * NSA_KV_WIDTH
    for which, out_ref in ((0, ko_ref), (1, vo_ref)):
        for g in range(NSA_KV_GROUPS):
            c0 = (which * NSA_KV_GROUPS + g) * HEAD_DIM
            chunks = jnp.concatenate(
                [cv_ref[0, :, l * tok + c0:l * tok + c0 + HEAD_DIM] for l in range(CMP_STRIDE)],
                axis=1).astype(F32)
            first = jnp.dot((chunks + pos_ref[which, 0]).astype(BF16), w1_ref[which, 0],
                            preferred_element_type=F32)
            second = jnp.dot((chunks + pos_ref[which, 1]).astype(BF16), w1_ref[which, 1],
                             preferred_element_type=F32)
            hid = jax.nn.gelu(first + pltpu.roll(second, nbp - 1, 0)).astype(BF16)
            out_ref[0, g] = jnp.dot(hid, w2_ref[which], preferred_element_type=F32).astype(BF16)


def _compress(cv, pos, w1, w2, batch, nbp):
    tokw = CMP_STRIDE * 2 * NSA_KV_WIDTH
    half = CMP_STRIDE * HEAD_DIM
    vmem = 2 * nbp * tokw * 2 + 2 * 4 * half * CMP_HIDDEN * 2 + 8 * nbp * half * 4 + (8 << 20)
    out_sds = [jax.ShapeDtypeStruct((batch, NSA_KV_GROUPS, nbp, HEAD_DIM), BF16)] * 2
    out_spec = [pl.BlockSpec((1, NSA_KV_GROUPS, nbp, HEAD_DIM), lambda b: (b, 0, 0, 0))] * 2
    return pl.pallas_call(
        _compress_kernel,
        grid=(batch,),
        in_specs=[pl.BlockSpec((1, nbp, tokw), lambda b: (b, 0, 0)),
                  pl.BlockSpec((2, 2, 1, half), lambda b: (0, 0, 0, 0)),
                  pl.BlockSpec((2, 2, half, CMP_HIDDEN), lambda b: (0, 0, 0, 0)),
                  pl.BlockSpec((2, CMP_HIDDEN, HEAD_DIM), lambda b: (0, 0, 0))],
        out_specs=out_spec,
        out_shape=out_sds,
        compiler_params=_cparams(("parallel",), vmem),
        name="kv_compress",
    )(cv, pos, w1, w2)


def _nsa_kernel(q_ref, kc_ref, vc_ref, ks_ref, vs_ref, kw_ref, vw_ref, gate_ref, tbl_ref, tblc_ref,
                ovt_ref, expt_ref, o_ref, acc_ref, m_ref):
    i = pl.program_id(1)
    tq, tk, rows = NSA_TQ, NSA_TK, NSA_ROWS
    gw = NSA_GROUP_SIZE * HEAD_DIM
    seq = expt_ref.shape[0]
    nsel = seq // SEL_BLOCK
    t0 = i * tq
    ovt = ovt_ref[...]
    jrow = lax.broadcasted_iota(jnp.int32, (nsel, tq), 0)
    tcol = t0 + lax.broadcasted_iota(jnp.int32, (nsel, tq), 1)
    cur = tcol // SEL_BLOCK
    valid = jrow * SEL_BLOCK <= tcol
    forced = (jrow == 0) | (jrow == cur) | (jrow == cur - 1)
    slab_row = lax.broadcasted_iota(jnp.int32, (8, tq), 0)

    def compressed_and_selection(g):
        qb = q_ref[0, :, g * gw:(g + 1) * gw]
        q4 = jnp.concatenate([qb[:, r * HEAD_DIM:(r + 1) * HEAD_DIM] for r in range(NSA_GROUP_SIZE)], axis=0)
        s_c = lax.dot_general(q4, kc_ref[0, g], _NT, preferred_element_type=F32) + tblc_ref[g, 0]
        reps = s_c.shape[1] // V7X_LANES
        m_c = jnp.broadcast_to(jnp.max(s_c, axis=-1, keepdims=True), (rows, V7X_LANES))
        e_c = jnp.exp2(s_c - jnp.concatenate([m_c] * reps, axis=1))
        l_c = jnp.broadcast_to(jnp.sum(e_c, axis=-1, keepdims=True), (rows, V7X_LANES))
        inv = jnp.where(m_c > 0.5 * MASK_VALUE, 1.0 / l_c, 0.0)
        p_c = e_c * jnp.concatenate([inv] * reps, axis=1)
        o_c = jnp.dot(p_c.astype(BF16), vc_ref[0, g], preferred_element_type=F32)

        p_sum = p_c[0:tq]
        for r in range(1, NSA_GROUP_SIZE):
            p_sum = p_sum + p_c[r * tq:(r + 1) * tq]
        p_hi = p_sum.astype(BF16)
        p_lo = (p_sum - p_hi.astype(F32)).astype(BF16)
        imp = (lax.dot_general(ovt, p_hi, _NT, preferred_element_type=F32)
               + lax.dot_general(ovt, p_lo, _NT, preferred_element_type=F32))[:nsel]
        score = jnp.where(valid, jnp.where(forced, SEL_FORCE, imp), -SEL_FORCE)
        score3 = score.reshape(nsel // 8, 8, tq)
        slabs = [score3[n] for n in range(nsel // 8)]
        ranks = [jnp.zeros((8, tq), F32) for _ in slabs]
        for other in range(nsel):
            srow = slabs[other // 8][other % 8:other % 8 + 1, :]
            for n, slab in enumerate(slabs):
                if 8 * n > other:
                    before = srow >= slab
                elif 8 * n + 7 <= other:
                    before = srow > slab
                else:
                    before = (srow > slab) | ((srow >= slab) & (slab_row > other - 8 * n))
                ranks[n] = ranks[n] + jnp.where(before, 1.0, 0.0)
        rank = jnp.stack(ranks, axis=0).reshape(nsel, tq)
        drop = jnp.where(rank < float(min(SEL_TOP_N, nsel)), 0.0, SEL_MASK)
        if nsel < V7X_LANES:
            drop = jnp.concatenate([drop, jnp.zeros((V7X_LANES - nsel, tq), F32)], axis=0)
        drop_q = drop.T.astype(BF16)
        q4_sel = jnp.concatenate([q4, jnp.concatenate([drop_q] * NSA_GROUP_SIZE, axis=0)], axis=1)
        return q4, q4_sel, o_c

    per_group = [compressed_and_selection(g) for g in range(NSA_KV_GROUPS)]

    selected = [(g, 0) for g in range(NSA_KV_GROUPS)]
    both = selected + [(g, 1) for g in range(NSA_KV_GROUPS)]
    ones = jnp.ones((tk, HEAD_DIM), BF16)

    def flash(jt, chains, first):
        start = pl.multiple_of(jt * tk, tk)
        d0 = i - jt
        for g, branch in chains:
            q4, q4_sel, _ = per_group[g]
            lanes = slice(g * HEAD_DIM, (g + 1) * HEAD_DIM)
            if branch == 1:
                bias = tbl_ref[g, 0 if first else jnp.where(d0 == WIN_LAST_D0, BIAS_WIN_IDX, d0)]
                s = lax.dot_general(q4, kw_ref[0, pl.ds(start, tk), lanes], _NT, preferred_element_type=F32)
                v = vw_ref[0, pl.ds(start, tk), lanes]
            else:
                bias = tbl_ref[g, 0 if first else jnp.minimum(d0, BIAS_CONST_IDX)]
                k_sel = jnp.concatenate([ks_ref[0, pl.ds(start, tk), lanes], expt_ref[pl.ds(start, tk), :]], axis=1)
                s = lax.dot_general(q4_sel, k_sel, _NT, preferred_element_type=F32)
                v = vs_ref[0, pl.ds(start, tk), lanes]
            s = s + bias
            v_aug = jnp.concatenate([v, ones], axis=1)
            m_tile = jnp.broadcast_to(jnp.max(s, axis=-1, keepdims=True), (rows, V7X_LANES))
            if first:
                m_new = m_tile
                p = jnp.exp2(s - jnp.concatenate([m_new] * (tk // V7X_LANES), axis=1))
                acc_ref[g, branch] = jnp.dot(p.astype(BF16), v_aug, preferred_element_type=F32)
            else:
                m_old = m_ref[g, branch]
                m_new = jnp.maximum(m_old, m_tile)
                p = jnp.exp2(s - jnp.concatenate([m_new] * (tk // V7X_LANES), axis=1))
                alpha = jnp.exp2(m_old - m_new)
                acc_ref[g, branch] = (jnp.concatenate([alpha] * 2, axis=1) * acc_ref[g, branch]
                                      + jnp.dot(p.astype(BF16), v_aug, preferred_element_type=F32))
            m_ref[g, branch] = m_new

    flash(i, both, True)

    @pl.when(i >= 2)
    def _():
        flash(i - 1, both, False)
        flash(i - 2, both, False)

    @pl.when(i == 1)
    def _():
        flash(0, both, False)

    remaining = jnp.maximum(i - 2, 0)
    trips = remaining >> NSA_TRIP_SHIFT
    rest = remaining & (NSA_TILES_PER_TRIP - 1)

    def trip_body(p, carry):
        for u in range(NSA_TILES_PER_TRIP):
            flash(i - 3 - NSA_TILES_PER_TRIP * p - u, selected, False)
        return carry
    lax.fori_loop(0, trips, trip_body, 0)

    for n_left in range(1, NSA_TILES_PER_TRIP):
        @pl.when(rest == n_left)
        def _(n_left=n_left):
            for jt in reversed(range(n_left)):
                flash(jt, selected, False)

    gate = 1.0 / (1.0 + jnp.exp(-gate_ref[0]))
    for g in range(NSA_KV_GROUPS):
        o_c = per_group[g][2]
        for r in range(NSA_GROUP_SIZE):
            rs = slice(r * tq, (r + 1) * tq)
            o_s = acc_ref[g, 0, rs, 0:HEAD_DIM] / acc_ref[g, 0, rs, HEAD_DIM:2 * HEAD_DIM]
            o_w = acc_ref[g, 1, rs, 0:HEAD_DIM] / acc_ref[g, 1, rs, HEAD_DIM:2 * HEAD_DIM]
            c = (g * NSA_GROUP_SIZE + r) * N_GATES
            out = gate[:, c:c + 1] * o_c[rs] + gate[:, c + 1:c + 2] * o_s + gate[:, c + 2:c + 3] * o_w
            col = (g * NSA_GROUP_SIZE + r) * HEAD_DIM
            o_ref[0, :, col:col + HEAD_DIM] = out


def _nsa_attention(proj_a, proj_b, kcmp, vcmp, tbl, tblc, ovt, expt, batch, seq, cols):
    tq, tk, rows = NSA_TQ, NSA_TK, NSA_ROWS
    nbp = kcmp.shape[2]
    n_tiles = tbl.shape[1]
    groups = NSA_KV_GROUPS
    qblk = cols["n_q"] // NSA_WIDTH
    gblk = cols["gates"] // V7X_LANES
    assert cols["n_q"] % NSA_WIDTH == 0 and cols["gates"] % V7X_LANES == 0 and N_GATES * NSA_HEADS <= V7X_LANES

    def resident(shape, index_map):
        return pl.BlockSpec(shape, index_map, pipeline_mode=pl.Buffered(1))

    def kv_spec(name):
        assert cols[name] % NSA_KV_WIDTH == 0
        blk = cols[name] // NSA_KV_WIDTH
        return resident((1, seq, NSA_KV_WIDTH), lambda b, i: (b, 0, blk))

    cmp_spec = resident((1, groups, nbp, HEAD_DIM), lambda b, i: (b, 0, 0, 0))
    vmem = (4 * seq * NSA_KV_WIDTH * 2 + groups * n_tiles * rows * tk * 4 + 2 * groups * rows * nbp * 4
            + seq * V7X_LANES * 2 + 2 * groups * rows * (2 * HEAD_DIM + V7X_LANES) * 4
            + 4 * tq * (NSA_WIDTH + groups * V7X_LANES) * 4
            + 12 * rows * tk * 4 + (4 << 20))
    return pl.pallas_call(
        _nsa_kernel,
        grid=(batch, seq // tq),
        in_specs=[pl.BlockSpec((1, tq, NSA_WIDTH), lambda b, i: (b, i, qblk)),
                  cmp_spec, cmp_spec,
                  kv_spec("ks"), kv_spec("vs"), kv_spec("kw"), kv_spec("vw"),
                  pl.BlockSpec((1, tq, V7X_LANES), lambda b, i: (b, i, gblk)),
                  resident((groups, n_tiles, rows, tk), lambda b, i: (0, 0, 0, 0)),
                  pl.BlockSpec((groups, 1, rows, nbp), lambda b, i: (0, i, 0, 0)),
                  resident((V7X_LANES, nbp), lambda b, i: (0, 0)),
                  resident((seq, V7X_LANES), lambda b, i: (0, 0))],
        out_specs=pl.BlockSpec((1, tq, NSA_WIDTH), lambda b, i: (b, i, 0)),
        out_shape=jax.ShapeDtypeStruct((batch, seq, NSA_WIDTH), F32),
        scratch_shapes=[pltpu.VMEM((groups, 2, rows, 2 * HEAD_DIM), F32),
                        pltpu.VMEM((groups, 2, rows, V7X_LANES), F32)],
        compiler_params=_cparams(("parallel", "arbitrary"), vmem),
        name="nsa_attention",
    )(proj_a, kcmp, vcmp, proj_a, proj_a, proj_a, proj_a, proj_b, tbl, tblc, ovt, expt)


def _selection_constants(seq, nbp):
    nb = seq // CMP_STRIDE - 1
    nsel = seq // SEL_BLOCK
    assert nsel <= V7X_LANES and nbp >= nb
    ci = np.arange(nbp)[None, :] * CMP_STRIDE
    sj = np.arange(V7X_LANES)[:, None] * SEL_BLOCK
    ovt = ((ci < sj + SEL_BLOCK) & (ci + CMP_BLOCK > sj) & (np.arange(nbp)[None, :] < nb)
           & (np.arange(V7X_LANES)[:, None] < nsel))
    expt = (np.arange(seq)[:, None] // SEL_BLOCK) == np.arange(V7X_LANES)[None, :]
    return jnp.asarray(ovt, BF16), jnp.asarray(expt, BF16)


def _nsa_branch(proj_a3, proj_b3, proj_cv3, cmp_pos, cmp_w1, cmp_w2, rel_bias, cols):
    batch, seq, _ = proj_a3.shape
    nbp = seq // CMP_STRIDE
    half = CMP_STRIDE * HEAD_DIM
    pos = jnp.stack(cmp_pos).reshape(2, 2, 1, half)
    w1 = jnp.stack(cmp_w1).reshape(2, 2, half, CMP_HIDDEN).astype(BF16)
    w2 = jnp.stack(cmp_w2).astype(BF16)
    kcmp, vcmp = _compress(proj_cv3, pos, w1, w2, batch, nbp)

    tbl = _bias_table(rel_bias, BIAS_TILES, NSA_TK, 1, 0, True)
    tblc = _bias_table(rel_bias, seq // NSA_TQ, nbp, CMP_STRIDE, -(CMP_BLOCK - 1), False)
    ovt, expt = _selection_constants(seq, nbp)
    return _nsa_attention(proj_a3, proj_b3, kcmp, vcmp, tbl, tblc, ovt, expt, batch, seq, cols)


def _out_kernel(osb_ref, zsb_ref, onsa_ref, znsa_ref, x_ref, gsb_ref, gnsa_ref, gfin_ref, wsb_ref, wnsa_ref,
                o_ref):
    def gated(o_r, z_r, g_r, rows):
        o = o_r[rows, :]
        y = o * lax.rsqrt(jnp.mean(o * o, axis=-1, keepdims=True) + RMS_EPS) * g_r[...]
        z = z_r[rows, :]
        return (y * (z * (1.0 / (1.0 + jnp.exp(-z))))).astype(BF16)

    for c in range(x_ref.shape[0] // OUT_CHUNK):
        rows = slice(c * OUT_CHUNK, (c + 1) * OUT_CHUNK)
        mixed = (jnp.dot(gated(osb_ref, zsb_ref, gsb_ref, rows), wsb_ref[...], preferred_element_type=F32)
                 + jnp.dot(gated(onsa_ref, znsa_ref, gnsa_ref, rows), wnsa_ref[...], preferred_element_type=F32))
        h = x_ref[rows, :] + mixed
        o_ref[rows, :] = h * lax.rsqrt(jnp.mean(h * h, axis=-1, keepdims=True) + RMS_EPS) * gfin_ref[...]


def _output_stage(o_sb, o_nsa, proj_b, x2d, g_sb, g_nsa, g_fin, w_out, tm):
    m, d = x2d.shape
    half = SB_WIDTH
    assert m % tm == 0 and tm % OUT_CHUNK == 0
    vmem = 2 * (4 * tm * half * 4 + 2 * tm * d * 4) + 2 * half * d * 2 + 6 * OUT_CHUNK * d * 4 + (4 << 20)
    row = lambda c: pl.BlockSpec((tm, half), lambda i: (i, c))
    vec = lambda n: pl.BlockSpec((1, n), lambda i: (0, 0))
    weights = pl.Buffered(1)
    return pl.pallas_call(
        _out_kernel,
        grid=(m // tm,),
        in_specs=[row(0), row(0), row(0), row(1),
                  pl.BlockSpec((tm, d), lambda i: (i, 0)),
                  vec(half), vec(half), vec(d),
                  pl.BlockSpec((half, d), lambda i: (0, 0), pipeline_mode=weights),
                  pl.BlockSpec((half, d), lambda i: (1, 0), pipeline_mode=weights)],
        out_specs=pl.BlockSpec((tm, d), lambda i: (i, 0)),
        out_shape=jax.ShapeDtypeStruct((m, d), F32),
        compiler_params=_cparams(("parallel",), vmem),
        name="output_stage",
    )(o_sb, proj_b, o_nsa, proj_b, x2d, g_sb, g_nsa, g_fin, w_out, w_out)


def kernel(x, norm_in, w_in, cmp_k_pos, cmp_k_w1, cmp_k_w2, cmp_v_pos, cmp_v_w1, cmp_v_w2,
           rel_bias, norm_sb, norm_nsa, w_out, norm_final):
    batch, seq, d_model = x.shape
    assert w_in.shape[0] == 1, "single-layer trunk: the final norm is fused into the output stage"
    m = batch * seq
    nbp = seq // CMP_STRIDE
    assert seq % NSA_TK == 0 and nbp % V7X_LANES == 0

    sizes = (SB_WIDTH,) * 4 + (NSA_WIDTH,) + (NSA_KV_WIDTH,) * 6 + (N_GATES * NSA_HEADS, NSA_WIDTH)
    off = np.concatenate([[0], np.cumsum(sizes)])
    (c_sbq, c_sbk, c_sbv, c_sbz, c_nq, c_kc, c_vc, c_ks, c_vs, c_kw, c_vw, c_gate, c_nz, c_end) = [int(o) for o in off]
    cols = {"n_q": 3 * SB_WIDTH, "ks": 3 * SB_WIDTH + NSA_WIDTH}
    cols["vs"] = cols["ks"] + NSA_KV_WIDTH
    cols["kw"] = cols["vs"] + NSA_KV_WIDTH
    cols["vw"] = cols["kw"] + NSA_KV_WIDTH
    cols["gates"] = 2 * SB_WIDTH

    w = w_in[0]
    w_a = jnp.concatenate([w[:, c_sbq:c_sbz], w[:, c_nq:c_kc], w[:, c_ks:c_gate]], axis=1).astype(BF16)
    w_cv = w[:, c_kc:c_ks].astype(BF16)
    gate_pad = jnp.zeros((d_model, V7X_LANES - N_GATES * NSA_HEADS), w.dtype)
    w_b = jnp.concatenate([w[:, c_sbz:c_nq], w[:, c_nz:c_end], w[:, c_gate:c_nz], gate_pad], axis=1).astype(BF16)
    g_in = norm_in[0].reshape(1, d_model)
    x2d = x.reshape(m, d_model)

    q_scale = np.ones((1, w_a.shape[1]), np.float32)
    q_scale[:, c_sbq:c_sbq + SB_WIDTH] = SCALE * LOG2E
    q_scale[:, cols["n_q"]:cols["n_q"] + NSA_WIDTH] = SCALE * LOG2E
    proj_a, xn = _rms_proj(x2d, g_in, w_a, jnp.asarray(q_scale), BF16, 1024, w_a.shape[1] // 4, "proj_attn")
    proj_cv = _proj_chunk_major(xn, w_cv, 1024, "proj_cmp")
    proj_b = _plain_proj(xn, w_b, F32, 512, "proj_gate")

    proj_a3 = proj_a.reshape(batch, seq, -1)
    o_sb = _sb_attention(proj_a3, batch, seq)
    o_nsa = _nsa_branch(proj_a3, proj_b.reshape(batch, seq, -1), proj_cv.reshape(batch, nbp, -1),
                        (cmp_k_pos[0], cmp_v_pos[0]), (cmp_k_w1[0], cmp_v_w1[0]), (cmp_k_w2[0], cmp_v_w2[0]),
                        rel_bias, cols)

    out = _output_stage(o_sb.reshape(m, SB_WIDTH), o_nsa.reshape(m, NSA_WIDTH), proj_b, x2d,
                        norm_sb[0].reshape(1, SB_WIDTH), norm_nsa[0].reshape(1, NSA_WIDTH),
                        norm_final.reshape(1, d_model), w_out[0].astype(BF16), OUT_TM)
    return out.reshape(batch, seq, d_model)
```

```python
import functools
import math

import numpy as np
import jax
import jax.numpy as jnp
from jax import lax
from jax.experimental import pallas as pl
from jax.experimental.pallas import tpu as pltpu

F32 = jnp.float32
BF16 = jnp.bfloat16

HEAD_DIM = 128
SB_HEADS = 8
NSA_HEADS = 8
NSA_KV_GROUPS = 2
NSA_GROUP_SIZE = NSA_HEADS // NSA_KV_GROUPS
SB_WIDTH = SB_HEADS * HEAD_DIM
NSA_WIDTH = NSA_HEADS * HEAD_DIM
NSA_KV_WIDTH = NSA_KV_GROUPS * HEAD_DIM
N_GATES = 3
CMP_STRIDE = 16
CMP_BLOCK = 2 * CMP_STRIDE
CMP_HIDDEN = 256
SEL_BLOCK = 64
SEL_TOP_N = 16
WINDOW = 512
REL_BUCKETS = 32
REL_MAX_EXACT = 16
REL_MAX_DISTANCE = 1024
RMS_EPS = 1e-6
SEL_FORCE = 1e9
MASK_VALUE = -1e30
SEL_MASK = -float(2 ** 100)
SCALE = 1.0 / math.sqrt(HEAD_DIM)
LOG2E = math.log2(math.e)

V7X_LANES = 128
V7X_MXU_DIM = 256
V7X_BF16_SUBLANES = 16
V7X_VMEM_BYTES = 64 * 1024 * 1024

SB_TQ = 1 * V7X_MXU_DIM
SB_TK = V7X_MXU_DIM
SB_HEADS_PER_STEP = 8
SB_DEAD_LOG2 = 160.0
NSA_TQ = V7X_MXU_DIM
NSA_TK = V7X_MXU_DIM
NSA_ROWS = NSA_GROUP_SIZE * NSA_TQ
NSA_TRIP_SHIFT = 2
NSA_TILES_PER_TRIP = 1 << NSA_TRIP_SHIFT
OUT_TM = 2 * V7X_MXU_DIM
OUT_CHUNK = V7X_MXU_DIM

_NT = (((1,), (1,)), ((), ()))


def _bucket_thresholds():
    n = np.arange(0, 4 * REL_MAX_DISTANCE)
    nf = np.maximum(n, 1).astype(np.float64)
    large = REL_MAX_EXACT + (np.log(nf / REL_MAX_EXACT) / math.log(REL_MAX_DISTANCE / REL_MAX_EXACT)
                             * (REL_BUCKETS - REL_MAX_EXACT)).astype(np.int64)
    bucket = np.where(n < REL_MAX_EXACT, n, np.minimum(large, REL_BUCKETS - 1))
    assert np.all(np.diff(bucket) >= 0) and bucket[-1] == REL_BUCKETS - 1
    return [int(n[bucket >= k][0]) for k in range(1, REL_BUCKETS)]


BUCKET_THR = _bucket_thresholds()
LAST_THR = BUCKET_THR[-1]
BIAS_CONST_IDX = -(-(LAST_THR + NSA_TK - 1) // NSA_TQ)
BIAS_FAR = 1 << 20
WIN_LAST_D0 = 2
BIAS_WIN_IDX = BIAS_CONST_IDX + 1
BIAS_TILES = BIAS_CONST_IDX + 2
assert NSA_TQ == NSA_TK and NSA_TQ + NSA_TK - 1 < WINDOW <= WIN_LAST_D0 * NSA_TQ < BIAS_CONST_IDX * NSA_TQ


def _cparams(sem, vmem_bytes):
    return pltpu.CompilerParams(dimension_semantics=sem, vmem_limit_bytes=int(min(vmem_bytes, V7X_VMEM_BYTES)))


def _bias_table_kernel(rb_ref, o_ref):
    h = pl.program_id(0)
    tile = pl.program_id(1)
    rows, cols = o_ref.shape[2], o_ref.shape[3]
    base = jnp.where(tile == BIAS_CONST_IDX, BIAS_FAR,
                     jnp.where(tile == BIAS_WIN_IDX, WIN_LAST_D0 * NSA_TQ, tile * NSA_TQ))
    ti = lax.broadcasted_iota(jnp.int32, (rows, cols), 0)
    cj = lax.broadcasted_iota(jnp.int32, (rows, cols), 1)
    dist = base + ti - cj
    val = jnp.full((rows, cols), rb_ref[0, h], F32)
    for k in range(1, REL_BUCKETS):
        val = jnp.where(dist >= BUCKET_THR[k - 1], rb_ref[k, h], val)
    keep = ((dist >= 0) & (dist < WINDOW)) | ((tile != 0) & (tile != BIAS_WIN_IDX))
    o_ref[0, 0] = val * LOG2E + jnp.where(keep, 0.0, MASK_VALUE)


def _bias_lookup(rb_ref, h, dist):
    val = jnp.full(dist.shape, rb_ref[0, h], F32)
    for k in range(1, REL_BUCKETS):
        val = jnp.where(dist >= BUCKET_THR[k - 1], rb_ref[k, h], val)
    return val * LOG2E + jnp.where(dist >= 0, 0.0, MASK_VALUE)


def _cmp_bias_table_kernel(rb_ref, o_ref):
    h = pl.program_id(0)
    n_tiles, tq, nbp = o_ref.shape[1], o_ref.shape[2], o_ref.shape[3]
    step = tq // CMP_STRIDE
    assert n_tiles * step <= nbp
    ti = lax.broadcasted_iota(jnp.int32, (tq, 2 * nbp), 0)
    cj = lax.broadcasted_iota(jnp.int32, (tq, 2 * nbp), 1)
    master = _bias_lookup(rb_ref, h, ti - CMP_STRIDE * (cj - nbp) - (CMP_BLOCK - 1))
    for i in range(n_tiles):
        o_ref[0, i] = pltpu.roll(master, nbp + step * i, 1)[:, :nbp]


def _cmp_bias_table(rel_bias, n_tiles, nbp):
    block = (1, n_tiles, NSA_TQ, nbp)
    return pl.pallas_call(
        _cmp_bias_table_kernel,
        grid=(NSA_HEADS,),
        in_specs=[pl.BlockSpec(memory_space=pltpu.SMEM)],
        out_specs=pl.BlockSpec(block, lambda h: (h // NSA_GROUP_SIZE, 0, h % NSA_GROUP_SIZE, 0)),
        out_shape=jax.ShapeDtypeStruct((NSA_KV_GROUPS, n_tiles, NSA_ROWS, nbp), F32),
        compiler_params=_cparams(("parallel",), 4 * n_tiles * NSA_TQ * nbp * 4 + (8 << 20)),
        name="cmp_bias_table",
    )(rel_bias)


def _bias_table(rel_bias, n_tiles, cols):
    return pl.pallas_call(
        _bias_table_kernel,
        grid=(NSA_HEADS, n_tiles),
        in_specs=[pl.BlockSpec(memory_space=pltpu.SMEM)],
        out_specs=pl.BlockSpec((1, 1, NSA_TQ, cols),
                               lambda h, t: (h // NSA_GROUP_SIZE, t, h % NSA_GROUP_SIZE, 0)),
        out_shape=jax.ShapeDtypeStruct((NSA_KV_GROUPS, n_tiles, NSA_ROWS, cols), F32),
        compiler_params=_cparams(("parallel", "parallel"), 8 << 20),
        name="bias_table",
    )(rel_bias)


def _normalise_rows(x_ref, g_ref, xn_ref, chunk):
    def body(c, carry):
        rows = pl.ds(pl.multiple_of(c * chunk, chunk), chunk)
        x = x_ref[rows, :]
        ms = jnp.mean(x * x, axis=-1, keepdims=True)
        xn_ref[rows, :] = (x * lax.rsqrt(ms + RMS_EPS) * g_ref[...]).astype(BF16)
        return carry
    lax.fori_loop(0, x_ref.shape[0] // chunk, body, 0)


def _proj_kernel(x_ref, g_ref, w_ref, cs_ref, o_ref, xn_ref, *, chunk):
    @pl.when(pl.program_id(1) == 0)
    def _():
        _normalise_rows(x_ref, g_ref, xn_ref, chunk)

    acc = jnp.dot(xn_ref[...], w_ref[...], preferred_element_type=F32)
    o_ref[...] = (acc * cs_ref[...]).astype(o_ref.dtype)


def _plain_proj_kernel(xn_ref, w_ref, o_ref):
    o_ref[...] = jnp.dot(xn_ref[...], w_ref[...], preferred_element_type=F32).astype(o_ref.dtype)


def _proj_chunk_major_kernel(xn_ref, w_ref, o_ref, acc_ref):
    acc = jnp.dot(xn_ref[...], w_ref[...], preferred_element_type=F32)
    n = acc.shape[1]
    slabs = n // V7X_LANES
    for s in range(slabs):
        acc_ref[s] = acc[:, s * V7X_LANES:(s + 1) * V7X_LANES]
    n_chunks = o_ref.shape[0]
    for l in range(CMP_STRIDE):
        for s in range(slabs):
            tokens = acc_ref[s, pl.ds(l, n_chunks, stride=CMP_STRIDE), :]
            o_ref[:, l * n + s * V7X_LANES:l * n + (s + 1) * V7X_LANES] = tokens.astype(o_ref.dtype)


def _proj_chunk_major(xn, w, tm, name):
    m, d = xn.shape
    n = w.shape[1]
    assert m % tm == 0 and tm % (CMP_STRIDE * V7X_BF16_SUBLANES) == 0 and n % V7X_LANES == 0
    vmem = 2 * tm * d * 2 + 2 * d * n * 2 + 2 * tm * n * 2 + 2 * tm * n * 4 + (4 << 20)
    return pl.pallas_call(
        _proj_chunk_major_kernel,
        grid=(m // tm,),
        in_specs=[pl.BlockSpec((tm, d), lambda i: (i, 0)),
                  pl.BlockSpec((d, n), lambda i: (0, 0))],
        out_specs=pl.BlockSpec((tm // CMP_STRIDE, CMP_STRIDE * n), lambda i: (i, 0)),
        out_shape=jax.ShapeDtypeStruct((m // CMP_STRIDE, CMP_STRIDE * n), BF16),
        scratch_shapes=[pltpu.VMEM((n // V7X_LANES, tm, V7X_LANES), F32)],
        compiler_params=_cparams(("parallel",), vmem),
        name=name,
    )(xn, w)


def _plain_proj(xn, w, out_dtype, tm, name):
    m, d = xn.shape
    n = w.shape[1]
    assert m % tm == 0
    osz = jnp.dtype(out_dtype).itemsize
    vmem = 2 * tm * d * 2 + 2 * d * n * 2 + 2 * tm * n * osz + tm * n * 4 + (4 << 20)
    return pl.pallas_call(
        _plain_proj_kernel,
        grid=(m // tm,),
        in_specs=[pl.BlockSpec((tm, d), lambda i: (i, 0)),
                  pl.BlockSpec((d, n), lambda i: (0, 0))],
        out_specs=pl.BlockSpec((tm, n), lambda i: (i, 0)),
        out_shape=jax.ShapeDtypeStruct((m, n), out_dtype),
        compiler_params=_cparams(("parallel",), vmem),
        name=name,
    )(xn, w)


def _rms_proj(x2d, g, w, col_scale, out_dtype, tm, tn, name):
    m, d = x2d.shape
    n = w.shape[1]
    assert m % tm == 0 and n % tn == 0 and col_scale.shape == (1, n)
    osz = jnp.dtype(out_dtype).itemsize
    vmem = 2 * tm * d * 4 + 2 * tm * d * 2 + 2 * d * tn * 2 + 2 * tm * tn * osz + tm * tn * 4 + (4 << 20)
    return pl.pallas_call(
        functools.partial(_proj_kernel, chunk=min(tm, 128)),
        grid=(m // tm, n // tn),
        in_specs=[pl.BlockSpec((tm, d), lambda i, j: (i, 0)),
                  pl.BlockSpec((1, d), lambda i, j: (0, 0)),
                  pl.BlockSpec((d, tn), lambda i, j: (0, j)),
                  pl.BlockSpec((1, tn), lambda i, j: (0, j))],
        out_specs=[pl.BlockSpec((tm, tn), lambda i, j: (i, j)),
                   pl.BlockSpec((tm, d), lambda i, j: (i, 0))],
        out_shape=[jax.ShapeDtypeStruct((m, n), out_dtype), jax.ShapeDtypeStruct((m, d), BF16)],
        compiler_params=_cparams(("parallel", "arbitrary"), vmem),
        name=name,
    )(x2d, g, w, col_scale)


def _sb_kernel(q_ref, k_ref, v_ref, o_ref, acc_ref, run_ref):
    i = pl.program_id(2)
    tq, tk = q_ref.shape[1], SB_TK
    n_diag = tq // tk
    row = lax.broadcasted_iota(jnp.int32, (tq, tk), 0)
    col = lax.broadcasted_iota(jnp.int32, (tq, tk), 1)
    below = row > col
    suffix = jnp.where(below[:tk], 1.0, 0.0).astype(BF16)

    def tile(j, row0, diag):
        start = pl.multiple_of(j * tk, tk)
        rows = slice(row0, tq)
        causal = below[:tq - row0]
        n_rows = tq - row0
        log_betas, rests, runs = [], [], []
        for hh in range(SB_HEADS_PER_STEP):
            lanes = slice(hh * HEAD_DIM, (hh + 1) * HEAD_DIM)
            z = lax.dot_general(q_ref[0, rows, lanes], k_ref[0, pl.ds(start, tk), lanes], _NT,
                                preferred_element_type=F32)
            sp = jnp.log2(1.0 + jnp.exp2(-jnp.abs(z)))
            neg_rest = jnp.maximum(z, 0.0) + sp
            log_beta = z - neg_rest
            if diag:
                neg_rest = jnp.where(causal, neg_rest, 0.0)
            run = run_ref[hh, rows]
            log_betas.append(log_beta - run)
            rests.append(neg_rest.astype(BF16))
            run_ref[hh, rows] = run + jnp.sum(neg_rest, axis=-1, keepdims=True)
        later_all = jnp.dot(jnp.concatenate(rests, axis=0), suffix, preferred_element_type=F32)
        for hh in range(SB_HEADS_PER_STEP):
            lanes = slice(hh * HEAD_DIM, (hh + 1) * HEAD_DIM)
            a = jnp.exp2(log_betas[hh] - later_all[hh * n_rows:(hh + 1) * n_rows])
            if diag:
                a = jnp.where(causal, a, 0.0)
            acc_ref[hh, rows] += jnp.dot(a.astype(BF16), v_ref[0, pl.ds(start, tk), lanes],
                                         preferred_element_type=F32)

    acc_ref[...] = jnp.zeros_like(acc_ref)
    run_ref[...] = jnp.zeros_like(run_ref)
    for c in reversed(range(n_diag)):
        tile(i * n_diag + c, c * tk, True)

    def walk(carry):
        jj, _ = carry
        tile(i * n_diag - 1 - jj, 0, False)
        return jj + 1, jnp.min(run_ref[...])

    def alive(carry):
        jj, least_run = carry
        return (jj < i * n_diag) & (least_run < SB_DEAD_LOG2)
    lax.while_loop(alive, walk, (jnp.int32(0), jnp.min(run_ref[...])))
    for hh in range(SB_HEADS_PER_STEP):
        o_ref[0, :, hh * HEAD_DIM:(hh + 1) * HEAD_DIM] = acc_ref[hh]


def _sb_attention(proj_a, batch, seq):
    tq = min(SB_TQ, seq)
    hps = SB_HEADS_PER_STEP
    width = hps * HEAD_DIM
    assert seq % tq == 0 and tq % SB_TK == 0 and SB_HEADS % hps == 0
    kblk = SB_WIDTH // width
    vmem = 2 * seq * width * 2 + 6 * tq * width * 4 + hps * 12 * tq * SB_TK * 4 + (4 << 20)
    whole_seq = pl.Buffered(1)
    return pl.pallas_call(
        _sb_kernel,
        grid=(batch, SB_HEADS // hps, seq // tq),
        in_specs=[pl.BlockSpec((1, tq, width), lambda b, h, i: (b, i, h)),
                  pl.BlockSpec((1, seq, width), lambda b, h, i: (b, 0, kblk + h), pipeline_mode=whole_seq),
                  pl.BlockSpec((1, seq, width), lambda b, h, i: (b, 0, 2 * kblk + h), pipeline_mode=whole_seq)],
        out_specs=pl.BlockSpec((1, tq, width), lambda b, h, i: (b, i, h)),
        out_shape=jax.ShapeDtypeStruct((batch, seq, SB_WIDTH), F32),
        scratch_shapes=[pltpu.VMEM((hps, tq, HEAD_DIM), F32), pltpu.VMEM((hps, tq, 1), F32)],
        compiler_params=_cparams(("parallel", "parallel", "arbitrary"), vmem),
        name="sb_attention",
    )(proj_a, proj_a, proj_a)


def _compress_kernel(cv_ref, pos_ref, w1_ref, w2_ref, ko_ref, vo_ref):
    nbp = cv_ref.shape[1]
    tok = 2 * NSA_KV_WIDTH
    for which, out_ref in ((0, ko_ref), (1, vo_ref)):
        for g in range(NSA_KV_GROUPS):
            c0 = (which * NSA_KV_GROUPS + g) * HEAD_DIM
            chunks = jnp.concatenate(
                [cv_ref[0, :, l * tok + c0:l * tok + c0 + HEAD_DIM] for l in range(CMP_STRIDE)],
                axis=1).astype(F32)
            first = jnp.dot((chunks + pos_ref[which, 0]).astype(BF16), w1_ref[which, 0],
                            preferred_element_type=F32)
            second = jnp.dot((chunks + pos_ref[which, 1]).astype(BF16), w1_ref[which, 1],
                             preferred_element_type=F32)
            hid = jax.nn.gelu(first + pltpu.roll(second, nbp - 1, 0)).astype(BF16)
            out_ref[0, g] = jnp.dot(hid, w2_ref[which], preferred_element_type=F32).astype(BF16)


def _compress(cv, pos, w1, w2, batch, nbp):
    tokw = CMP_STRIDE * 2 * NSA_KV_WIDTH
    half = CMP_STRIDE * HEAD_DIM
    vmem = 2 * nbp * tokw * 2 + 2 * 4 * half * CMP_HIDDEN * 2 + 8 * nbp * half * 4 + (8 << 20)
    out_sds = [jax.ShapeDtypeStruct((batch, NSA_KV_GROUPS, nbp, HEAD_DIM), BF16)] * 2
    out_spec = [pl.BlockSpec((1, NSA_KV_GROUPS, nbp, HEAD_DIM), lambda b: (b, 0, 0, 0))] * 2
    return pl.pallas_call(
        _compress_kernel,
        grid=(batch,),
        in_specs=[pl.BlockSpec((1, nbp, tokw), lambda b: (b, 0, 0)),
                  pl.BlockSpec((2, 2, 1, half), lambda b: (0, 0, 0, 0)),
                  pl.BlockSpec((2, 2, half, CMP_HIDDEN), lambda b: (0, 0, 0, 0)),
                  pl.BlockSpec((2, CMP_HIDDEN, HEAD_DIM), lambda b: (0, 0, 0))],
        out_specs=out_spec,
        out_shape=out_sds,
        compiler_params=_cparams(("parallel",), vmem),
        name="kv_compress",
    )(cv, pos, w1, w2)


def _nsa_kernel(q_ref, kc_ref, vc_ref, ks_ref, vs_ref, kw_ref, vw_ref, gate_ref, tbl_ref, tblc_ref,
                ovt_ref, expt_ref, o_ref, acc_ref, m_ref):
    i = pl.program_id(1)
    tq, tk, rows = NSA_TQ, NSA_TK, NSA_ROWS
    gw = NSA_GROUP_SIZE * HEAD_DIM
    seq = expt_ref.shape[0]
    nsel = seq // SEL_BLOCK
    t0 = i * tq
    ovt = ovt_ref[...]
    jrow = lax.broadcasted_iota(jnp.int32, (nsel, tq), 0)
    tcol = t0 + lax.broadcasted_iota(jnp.int32, (nsel, tq), 1)
    cur = tcol // SEL_BLOCK
    valid = jrow * SEL_BLOCK <= tcol
    forced = (jrow == 0) | (jrow == cur) | (jrow == cur - 1)
    slab_row = lax.broadcasted_iota(jnp.int32, (8, tq), 0)

    def compressed_and_selection(g):
        qb = q_ref[0, :, g * gw:(g + 1) * gw]
        q4 = jnp.concatenate([qb[:, r * HEAD_DIM:(r + 1) * HEAD_DIM] for r in range(NSA_GROUP_SIZE)], axis=0)
        s_c = lax.dot_general(q4, kc_ref[0, g], _NT, preferred_element_type=F32) + tblc_ref[g, 0]
        reps = s_c.shape[1] // V7X_LANES
        m_c = jnp.broadcast_to(jnp.max(s_c, axis=-1, keepdims=True), (rows, V7X_LANES))
        e_c = jnp.exp2(s_c - jnp.concatenate([m_c] * reps, axis=1))
        l_c = jnp.broadcast_to(jnp.sum(e_c, axis=-1, keepdims=True), (rows, V7X_LANES))
        inv = jnp.where(m_c > 0.5 * MASK_VALUE, 1.0 / l_c, 0.0)
        p_c = e_c * jnp.concatenate([inv] * reps, axis=1)
        o_c = jnp.dot(p_c.astype(BF16), vc_ref[0, g], preferred_element_type=F32)

        p_sum = p_c[0:tq]
        for r in range(1, NSA_GROUP_SIZE):
            p_sum = p_sum + p_c[r * tq:(r + 1) * tq]
        p_hi = p_sum.astype(BF16)
        p_lo = (p_sum - p_hi.astype(F32)).astype(BF16)
        imp = (lax.dot_general(ovt, p_hi, _NT, preferred_element_type=F32)
               + lax.dot_general(ovt, p_lo, _NT, preferred_element_type=F32))[:nsel]
        score = jnp.where(valid, jnp.where(forced, SEL_FORCE, imp), -SEL_FORCE)
        score3 = score.reshape(nsel // 8, 8, tq)
        slabs = [score3[n] for n in range(nsel // 8)]
        ranks = [jnp.zeros((8, tq), F32) for _ in slabs]
        for other in range(nsel):
            srow = slabs[other // 8][other % 8:other % 8 + 1, :]
            for n, slab in enumerate(slabs):
                if 8 * n > other:
                    before = srow >= slab
                elif 8 * n + 7 <= other:
                    before = srow > slab
                else:
                    before = (srow > slab) | ((srow >= slab) & (slab_row > other - 8 * n))
                ranks[n] = ranks[n] + jnp.where(before, 1.0, 0.0)
        rank = jnp.stack(ranks, axis=0).reshape(nsel, tq)
        drop = jnp.where(rank < float(min(SEL_TOP_N, nsel)), 0.0, SEL_MASK)
        if nsel < V7X_LANES:
            drop = jnp.concatenate([drop, jnp.zeros((V7X_LANES - nsel, tq), F32)], axis=0)
        drop_q = drop.T.astype(BF16)
        q4_sel = jnp.concatenate([q4, jnp.concatenate([drop_q] * NSA_GROUP_SIZE, axis=0)], axis=1)
        return q4, q4_sel, o_c

    per_group = [compressed_and_selection(g) for g in range(NSA_KV_GROUPS)]

    selected = [(g, 0) for g in range(NSA_KV_GROUPS)]
    both = selected + [(g, 1) for g in range(NSA_KV_GROUPS)]
    ones = jnp.ones((tk, HEAD_DIM), BF16)

    def flash(jt, chains, first):
        start = pl.multiple_of(jt * tk, tk)
        d0 = i - jt
        for g, branch in chains:
            q4, q4_sel, _ = per_group[g]
            lanes = slice(g * HEAD_DIM, (g + 1) * HEAD_DIM)
            if branch == 1:
                bias = tbl_ref[g, 0 if first else jnp.where(d0 == WIN_LAST_D0, BIAS_WIN_IDX, d0)]
                s = lax.dot_general(q4, kw_ref[0, pl.ds(start, tk), lanes], _NT, preferred_element_type=F32)
                v = vw_ref[0, pl.ds(start, tk), lanes]
            else:
                bias = tbl_ref[g, 0 if first else jnp.minimum(d0, BIAS_CONST_IDX)]
                k_sel = jnp.concatenate([ks_ref[0, pl.ds(start, tk), lanes], expt_ref[pl.ds(start, tk), :]], axis=1)
                s = lax.dot_general(q4_sel, k_sel, _NT, preferred_element_type=F32)
                v = vs_ref[0, pl.ds(start, tk), lanes]
            s = s + bias
            v_aug = jnp.concatenate([v, ones], axis=1)
            m_tile = jnp.broadcast_to(jnp.max(s, axis=-1, keepdims=True), (rows, V7X_LANES))
            if first:
                m_new = m_tile
                p = jnp.exp2(s - jnp.concatenate([m_new] * (tk // V7X_LANES), axis=1))
                acc_ref[g, branch] = jnp.dot(p.astype(BF16), v_aug, preferred_element_type=F32)
            else:
                m_old = m_ref[g, branch]
                m_new = jnp.maximum(m_old, m_tile)
                p = jnp.exp2(s - jnp.concatenate([m_new] * (tk // V7X_LANES), axis=1))
                alpha = jnp.exp2(m_old - m_new)
                acc_ref[g, branch] = (jnp.concatenate([alpha] * 2, axis=1) * acc_ref[g, branch]
                                      + jnp.dot(p.astype(BF16), v_aug, preferred_element_type=F32))
            m_ref[g, branch] = m_new

    flash(i, both, True)

    @pl.when(i >= 2)
    def _():
        flash(i - 1, both, False)
        flash(i - 2, both, False)

    @pl.when(i == 1)
    def _():
        flash(0, both, False)

    remaining = jnp.maximum(i - 2, 0)
    trips = remaining >> NSA_TRIP_SHIFT
    rest = remaining & (NSA_TILES_PER_TRIP - 1)

    def trip_body(p, carry):
        for u in range(NSA_TILES_PER_TRIP):
            flash(i - 3 - NSA_TILES_PER_TRIP * p - u, selected, False)
        return carry
    lax.fori_loop(0, trips, trip_body, 0)

    for n_left in range(1, NSA_TILES_PER_TRIP):
        @pl.when(rest == n_left)
        def _(n_left=n_left):
            for jt in reversed(range(n_left)):
                flash(jt, selected, False)

    gate = 1.0 / (1.0 + jnp.exp(-gate_ref[0]))
    for g in range(NSA_KV_GROUPS):
        o_c = per_group[g][2]
        for r in range(NSA_GROUP_SIZE):
            rs = slice(r * tq, (r + 1) * tq)
            o_s = acc_ref[g, 0, rs, 0:HEAD_DIM] / acc_ref[g, 0, rs, HEAD_DIM:2 * HEAD_DIM]
            o_w = acc_ref[g, 1, rs, 0:HEAD_DIM] / acc_ref[g, 1, rs, HEAD_DIM:2 * HEAD_DIM]
            c = (g * NSA_GROUP_SIZE + r) * N_GATES
            out = gate[:, c:c + 1] * o_c[rs] + gate[:, c + 1:c + 2] * o_s + gate[:, c + 2:c + 3] * o_w
            col = (g * NSA_GROUP_SIZE + r) * HEAD_DIM
            o_ref[0, :, col:col + HEAD_DIM] = out


def _nsa_attention(proj_a, proj_b, kcmp, vcmp, tbl, tblc, ovt, expt, batch, seq, cols):
    tq, tk, rows = NSA_TQ, NSA_TK, NSA_ROWS
    nbp = kcmp.shape[2]
    n_tiles = tbl.shape[1]
    groups = NSA_KV_GROUPS
    qblk = cols["n_q"] // NSA_WIDTH
    gblk = cols["gates"] // V7X_LANES
    assert cols["n_q"] % NSA_WIDTH == 0 and cols["gates"] % V7X_LANES == 0 and N_GATES * NSA_HEADS <= V7X_LANES

    def resident(shape, index_map):
        return pl.BlockSpec(shape, index_map, pipeline_mode=pl.Buffered(1))

    def kv_spec(name):
        assert cols[name] % NSA_KV_WIDTH == 0
        blk = cols[name] // NSA_KV_WIDTH
        return resident((1, seq, NSA_KV_WIDTH), lambda b, i: (b, 0, blk))

    cmp_spec = resident((1, groups, nbp, HEAD_DIM), lambda b, i: (b, 0, 0, 0))
    vmem = (4 * seq * NSA_KV_WIDTH * 2 + groups * n_tiles * rows * tk * 4 + 2 * groups * rows * nbp * 4
            + seq * V7X_LANES * 2 + 2 * groups * rows * (2 * HEAD_DIM + V7X_LANES) * 4
            + 4 * tq * (NSA_WIDTH + groups * V7X_LANES) * 4
            + 12 * rows * tk * 4 + (4 << 20))
    return pl.pallas_call(
        _nsa_kernel,
        grid=(batch, seq // tq),
        in_specs=[pl.BlockSpec((1, tq, NSA_WIDTH), lambda b, i: (b, i, qblk)),
                  cmp_spec, cmp_spec,
                  kv_spec("ks"), kv_spec("vs"), kv_spec("kw"), kv_spec("vw"),
                  pl.BlockSpec((1, tq, V7X_LANES), lambda b, i: (b, i, gblk)),
                  resident((groups, n_tiles, rows, tk), lambda b, i: (0, 0, 0, 0)),
                  pl.BlockSpec((groups, 1, rows, nbp), lambda b, i: (0, i, 0, 0)),
                  resident((V7X_LANES, nbp), lambda b, i: (0, 0)),
                  resident((seq, V7X_LANES), lambda b, i: (0, 0))],
        out_specs=pl.BlockSpec((1, tq, NSA_WIDTH), lambda b, i: (b, i, 0)),
        out_shape=jax.ShapeDtypeStruct((batch, seq, NSA_WIDTH), F32),
        scratch_shapes=[pltpu.VMEM((groups, 2, rows, 2 * HEAD_DIM), F32),
                        pltpu.VMEM((groups, 2, rows, V7X_LANES), F32)],
        compiler_params=_cparams(("parallel", "arbitrary"), vmem),
        name="nsa_attention",
    )(proj_a, kcmp, vcmp, proj_a, proj_a, proj_a, proj_a, proj_b, tbl, tblc, ovt, expt)


def _selection_constants(seq, nbp):
    nb = seq // CMP_STRIDE - 1
    nsel = seq // SEL_BLOCK
    assert nsel <= V7X_LANES and nbp >= nb
    ci = np.arange(nbp)[None, :] * CMP_STRIDE
    sj = np.arange(V7X_LANES)[:, None] * SEL_BLOCK
    ovt = ((ci < sj + SEL_BLOCK) & (ci + CMP_BLOCK > sj) & (np.arange(nbp)[None, :] < nb)
           & (np.arange(V7X_LANES)[:, None] < nsel))
    expt = (np.arange(seq)[:, None] // SEL_BLOCK) == np.arange(V7X_LANES)[None, :]
    return jnp.asarray(ovt, BF16), jnp.asarray(expt, BF16)


def _nsa_branch(proj_a3, proj_b3, proj_cv3, cmp_pos, cmp_w1, cmp_w2, rel_bias, cols):
    batch, seq, _ = proj_a3.shape
    nbp = seq // CMP_STRIDE
    half = CMP_STRIDE * HEAD_DIM
    pos = jnp.stack(cmp_pos).reshape(2, 2, 1, half)
    w1 = jnp.stack(cmp_w1).reshape(2, 2, half, CMP_HIDDEN).astype(BF16)
    w2 = jnp.stack(cmp_w2).astype(BF16)
    kcmp, vcmp = _compress(proj_cv3, pos, w1, w2, batch, nbp)

    tbl = _bias_table(rel_bias, BIAS_TILES, NSA_TK)
    tblc = _cmp_bias_table(rel_bias, seq // NSA_TQ, nbp)
    ovt, expt = _selection_constants(seq, nbp)
    return _nsa_attention(proj_a3, proj_b3, kcmp, vcmp, tbl, tblc, ovt, expt, batch, seq, cols)


def _out_kernel(osb_ref, zsb_ref, onsa_ref, znsa_ref, x_ref, gsb_ref, gnsa_ref, gfin_ref, wsb_ref, wnsa_ref,
                o_ref):
    def gated(o_r, z_r, g_r, rows):
        o = o_r[rows, :]
        y = o * lax.rsqrt(jnp.mean(o * o, axis=-1, keepdims=True) + RMS_EPS) * g_r[...]
        z = z_r[rows, :]
        return (y * (z * (1.0 / (1.0 + jnp.exp(-z))))).astype(BF16)

    for c in range(x_ref.shape[0] // OUT_CHUNK):
        rows = slice(c * OUT_CHUNK, (c + 1) * OUT_CHUNK)
        mixed = (jnp.dot(gated(osb_ref, zsb_ref, gsb_ref, rows), wsb_ref[...], preferred_element_type=F32)
                 + jnp.dot(gated(onsa_ref, znsa_ref, gnsa_ref, rows), wnsa_ref[...], preferred_element_type=F32))
        h = x_ref[rows, :] + mixed
        o_ref[rows, :] = h * lax.rsqrt(jnp.mean(h * h, axis=-1, keepdims=True) + RMS_EPS) * gfin_ref[...]


def _output_stage(o_sb, o_nsa, proj_b, x2d, g_sb, g_nsa, g_fin, w_out, tm):
    m, d = x2d.shape
    half = SB_WIDTH
    assert m % tm == 0 and tm % OUT_CHUNK == 0
    vmem = 2 * (4 * tm * half * 4 + 2 * tm * d * 4) + 2 * half * d * 2 + 6 * OUT_CHUNK * d * 4 + (4 << 20)
    row = lambda c: pl.BlockSpec((tm, half), lambda i: (i, c))
    vec = lambda n: pl.BlockSpec((1, n), lambda i: (0, 0))
    weights = pl.Buffered(1)
    return pl.pallas_call(
        _out_kernel,
        grid=(m // tm,),
        in_specs=[row(0), row(0), row(0), row(1),
                  pl.BlockSpec((tm, d), lambda i: (i, 0)),
                  vec(half), vec(half), vec(d),
                  pl.BlockSpec((half, d), lambda i: (0, 0), pipeline_mode=weights),
                  pl.BlockSpec((half, d), lambda i: (1, 0), pipeline_mode=weights)],
        out_specs=pl.BlockSpec((tm, d), lambda i: (i, 0)),
        out_shape=jax.ShapeDtypeStruct((m, d), F32),
        compiler_params=_cparams(("parallel",), vmem),
        name="output_stage",
    )(o_sb, proj_b, o_nsa, proj_b, x2d, g_sb, g_nsa, g_fin, w_out, w_out)


def kernel(x, norm_in, w_in, cmp_k_pos, cmp_k_w1, cmp_k_w2, cmp_v_pos, cmp_v_w1, cmp_v_w2,
           rel_bias, norm_sb, norm_nsa, w_out, norm_final):
    batch, seq, d_model = x.shape
    assert w_in.shape[0] == 1, "single-layer trunk: the final norm is fused into the output stage"
    m = batch * seq
    nbp = seq // CMP_STRIDE
    assert seq % NSA_TK == 0 and nbp % V7X_LANES == 0

    sizes = (SB_WIDTH,) * 4 + (NSA_WIDTH,) + (NSA_KV_WIDTH,) * 6 + (N_GATES * NSA_HEADS, NSA_WIDTH)
    off = np.concatenate([[0], np.cumsum(sizes)])
    (c_sbq, c_sbk, c_sbv, c_sbz, c_nq, c_kc, c_vc, c_ks, c_vs, c_kw, c_vw, c_gate, c_nz, c_end) = [int(o) for o in off]
    cols = {"n_q": 3 * SB_WIDTH, "ks": 3 * SB_WIDTH + NSA_WIDTH}
    cols["vs"] = cols["ks"] + NSA_KV_WIDTH
    cols["kw"] = cols["vs"] + NSA_KV_WIDTH
    cols["vw"] = cols["kw"] + NSA_KV_WIDTH
    cols["gates"] = 2 * SB_WIDTH

    w = w_in[0]
    w_a = jnp.concatenate([w[:, c_sbq:c_sbz], w[:, c_nq:c_kc], w[:, c_ks:c_gate]], axis=1).astype(BF16)
    w_cv = w[:, c_kc:c_ks].astype(BF16)
    gate_pad = jnp.zeros((d_model, V7X_LANES - N_GATES * NSA_HEADS), w.dtype)
    w_b = jnp.concatenate([w[:, c_sbz:c_nq], w[:, c_nz:c_end], w[:, c_gate:c_nz], gate_pad], axis=1).astype(BF16)
    g_in = norm_in[0].reshape(1, d_model)
    x2d = x.reshape(m, d_model)

    q_scale = np.ones((1, w_a.shape[1]), np.float32)
    q_scale[:, c_sbq:c_sbq + SB_WIDTH] = SCALE * LOG2E
    q_scale[:, cols["n_q"]:cols["n_q"] + NSA_WIDTH] = SCALE * LOG2E
    proj_a, xn = _rms_proj(x2d, g_in, w_a, jnp.asarray(q_scale), BF16, 1024, w_a.shape[1] // 4, "proj_attn")
    proj_cv = _proj_chunk_major(xn, w_cv, 1024, "proj_cmp")
    proj_b = _plain_proj(xn, w_b, F32, 512, "proj_gate")

    proj_a3 = proj_a.reshape(batch, seq, -1)
    o_sb = _sb_attention(proj_a3, batch, seq)
    o_nsa = _nsa_branch(proj_a3, proj_b.reshape(batch, seq, -1), proj_cv.reshape(batch, nbp, -1),
                        (cmp_k_pos[0], cmp_v_pos[0]), (cmp_k_w1[0], cmp_v_w1[0]), (cmp_k_w2[0], cmp_v_w2[0]),
                        rel_bias, cols)

    out = _output_stage(o_sb.reshape(m, SB_WIDTH), o_nsa.reshape(m, NSA_WIDTH), proj_b, x2d,
                        norm_sb[0].reshape(1, SB_WIDTH), norm_nsa[0].reshape(1, NSA_WIDTH),
                        norm_final.reshape(1, d_model), w_out[0].astype(BF16), OUT_TM)
    return out.reshape(batch, seq, d_model)
```

```python
import functools
import math

import numpy as np
import jax
import jax.numpy as jnp
from jax import lax
from jax.experimental import pallas as pl
from jax.experimental.pallas import tpu as pltpu

F32 = jnp.float32
BF16 = jnp.bfloat16

HEAD_DIM = 128
SB_HEADS = 8
NSA_HEADS = 8
NSA_KV_GROUPS = 2
NSA_GROUP_SIZE = NSA_HEADS // NSA_KV_GROUPS
SB_WIDTH = SB_HEADS * HEAD_DIM
NSA_WIDTH = NSA_HEADS * HEAD_DIM
NSA_KV_WIDTH = NSA_KV_GROUPS * HEAD_DIM
N_GATES = 3
CMP_STRIDE = 16
CMP_BLOCK = 2 * CMP_STRIDE
CMP_HIDDEN = 256
SEL_BLOCK = 64
SEL_TOP_N = 16
WINDOW = 512
REL_BUCKETS = 32
REL_MAX_EXACT = 16
REL_MAX_DISTANCE = 1024
RMS_EPS = 1e-6
SEL_FORCE = 1e9
MASK_VALUE = -1e30
SEL_MASK = -float(2 ** 100)
SCALE = 1.0 / math.sqrt(HEAD_DIM)
LOG2E = math.log2(math.e)

V7X_LANES = 128
V7X_MXU_DIM = 256
V7X_BF16_SUBLANES = 16
V7X_VMEM_BYTES = 64 * 1024 * 1024
VMEM_SPILL_BYTES = 8 * 1024 * 1024

SB_TQ = V7X_MXU_DIM
SB_TK = V7X_MXU_DIM
SB_HEADS_PER_STEP = 8
SB_DEAD_LOG2 = 160.0
NSA_TQ = V7X_MXU_DIM
NSA_TK = V7X_MXU_DIM
NSA_ROWS = NSA_GROUP_SIZE * NSA_TQ
NSA_TRIP_SHIFT = 2
NSA_TILES_PER_TRIP = 1 << NSA_TRIP_SHIFT
PROJ_TM = 4 * V7X_MXU_DIM
PROJ_COL_TILES = 4
OUT_TM = 2 * V7X_MXU_DIM
OUT_CHUNK = V7X_MXU_DIM

_NT = (((1,), (1,)), ((), ()))


def _bucket_thresholds():
    n = np.arange(0, 4 * REL_MAX_DISTANCE)
    nf = np.maximum(n, 1).astype(np.float64)
    large = REL_MAX_EXACT + (np.log(nf / REL_MAX_EXACT) / math.log(REL_MAX_DISTANCE / REL_MAX_EXACT)
                             * (REL_BUCKETS - REL_MAX_EXACT)).astype(np.int64)
    bucket = np.where(n < REL_MAX_EXACT, n, np.minimum(large, REL_BUCKETS - 1))
    assert np.all(np.diff(bucket) >= 0) and bucket[-1] == REL_BUCKETS - 1
    return [int(n[bucket >= k][0]) for k in range(1, REL_BUCKETS)]


BUCKET_THR = _bucket_thresholds()
LAST_THR = BUCKET_THR[-1]
BIAS_CONST_IDX = -(-(LAST_THR + NSA_TK - 1) // NSA_TQ)
BIAS_FAR = 1 << 20
WIN_LAST_D0 = 2
BIAS_WIN_IDX = BIAS_CONST_IDX + 1
BIAS_TILES = BIAS_CONST_IDX + 2
assert NSA_TQ == NSA_TK and NSA_TQ + NSA_TK - 1 < WINDOW <= WIN_LAST_D0 * NSA_TQ < BIAS_CONST_IDX * NSA_TQ


def _cparams(sem, block_bytes):
    limit = min(block_bytes + VMEM_SPILL_BYTES, V7X_VMEM_BYTES)
    return pltpu.CompilerParams(dimension_semantics=sem, vmem_limit_bytes=int(limit))


def _bias_table_kernel(rb_ref, o_ref):
    h = pl.program_id(0)
    tile = pl.program_id(1)
    rows, cols = o_ref.shape[2], o_ref.shape[3]
    base = jnp.where(tile == BIAS_CONST_IDX, BIAS_FAR,
                     jnp.where(tile == BIAS_WIN_IDX, WIN_LAST_D0 * NSA_TQ, tile * NSA_TQ))
    ti = lax.broadcasted_iota(jnp.int32, (rows, cols), 0)
    cj = lax.broadcasted_iota(jnp.int32, (rows, cols), 1)
    dist = base + ti - cj
    val = jnp.full((rows, cols), rb_ref[0, h], F32)
    for k in range(1, REL_BUCKETS):
        val = jnp.where(dist >= BUCKET_THR[k - 1], rb_ref[k, h], val)
    keep = ((dist >= 0) & (dist < WINDOW)) | ((tile != 0) & (tile != BIAS_WIN_IDX))
    o_ref[0, 0] = val * LOG2E + jnp.where(keep, 0.0, MASK_VALUE)


def _bias_lookup(rb_ref, h, dist):
    val = jnp.full(dist.shape, rb_ref[0, h], F32)
    for k in range(1, REL_BUCKETS):
        val = jnp.where(dist >= BUCKET_THR[k - 1], rb_ref[k, h], val)
    return val * LOG2E + jnp.where(dist >= 0, 0.0, MASK_VALUE)


def _cmp_bias_table_kernel(rb_ref, o_ref):
    h = pl.program_id(0)
    n_tiles, tq, nbp = o_ref.shape[1], o_ref.shape[2], o_ref.shape[3]
    step = tq // CMP_STRIDE
    assert n_tiles * step <= nbp
    ti = lax.broadcasted_iota(jnp.int32, (tq, 2 * nbp), 0)
    cj = lax.broadcasted_iota(jnp.int32, (tq, 2 * nbp), 1)
    master = _bias_lookup(rb_ref, h, ti - CMP_STRIDE * (cj - nbp) - (CMP_BLOCK - 1))
    for i in range(n_tiles):
        o_ref[0, i] = pltpu.roll(master, nbp + step * i, 1)[:, :nbp]


def _cmp_bias_table(rel_bias, n_tiles, nbp):
    block = (1, n_tiles, NSA_TQ, nbp)
    return pl.pallas_call(
        _cmp_bias_table_kernel,
        grid=(NSA_HEADS,),
        in_specs=[pl.BlockSpec(memory_space=pltpu.SMEM)],
        out_specs=pl.BlockSpec(block, lambda h: (h // NSA_GROUP_SIZE, 0, h % NSA_GROUP_SIZE, 0)),
        out_shape=jax.ShapeDtypeStruct((NSA_KV_GROUPS, n_tiles, NSA_ROWS, nbp), F32),
        compiler_params=_cparams(("parallel",), 4 * n_tiles * NSA_TQ * nbp * 4),
        name="cmp_bias_table",
    )(rel_bias)


def _bias_table(rel_bias, n_tiles, cols):
    return pl.pallas_call(
        _bias_table_kernel,
        grid=(NSA_HEADS, n_tiles),
        in_specs=[pl.BlockSpec(memory_space=pltpu.SMEM)],
        out_specs=pl.BlockSpec((1, 1, NSA_TQ, cols),
                               lambda h, t: (h // NSA_GROUP_SIZE, t, h % NSA_GROUP_SIZE, 0)),
        out_shape=jax.ShapeDtypeStruct((NSA_KV_GROUPS, n_tiles, NSA_ROWS, cols), F32),
        compiler_params=_cparams(("parallel", "parallel"), 2 * NSA_TQ * cols * 4),
        name="bias_table",
    )(rel_bias)


def _normalise_rows(x_ref, g_ref, xn_ref, chunk):
    def body(c, carry):
        rows = pl.ds(pl.multiple_of(c * chunk, chunk), chunk)
        x = x_ref[rows, :]
        ms = jnp.mean(x * x, axis=-1, keepdims=True)
        xn_ref[rows, :] = (x * lax.rsqrt(ms + RMS_EPS) * g_ref[...]).astype(BF16)
        return carry
    lax.fori_loop(0, x_ref.shape[0] // chunk, body, 0)


def _proj_kernel(x_ref, g_ref, w_ref, cs_ref, o_ref, xn_ref, *, chunk):
    @pl.when(pl.program_id(1) == 0)
    def _():
        _normalise_rows(x_ref, g_ref, xn_ref, chunk)

    acc = jnp.dot(xn_ref[...], w_ref[...], preferred_element_type=F32)
    o_ref[...] = (acc * cs_ref[...]).astype(o_ref.dtype)


def _plain_proj_kernel(xn_ref, w_ref, o_ref):
    o_ref[...] = jnp.dot(xn_ref[...], w_ref[...], preferred_element_type=F32).astype(o_ref.dtype)


def _proj_chunk_major_kernel(xn_ref, w_ref, o_ref, acc_ref):
    acc = jnp.dot(xn_ref[...], w_ref[...], preferred_element_type=F32)
    n = acc.shape[1]
    slabs = n // V7X_LANES
    for s in range(slabs):
        acc_ref[s] = acc[:, s * V7X_LANES:(s + 1) * V7X_LANES]
    n_chunks = o_ref.shape[0]
    for l in range(CMP_STRIDE):
        for s in range(slabs):
            tokens = acc_ref[s, pl.ds(l, n_chunks, stride=CMP_STRIDE), :]
            o_ref[:, l * n + s * V7X_LANES:l * n + (s + 1) * V7X_LANES] = tokens.astype(o_ref.dtype)


def _proj_chunk_major(xn, w, tm, name):
    m, d = xn.shape
    n = w.shape[1]
    assert m % tm == 0 and tm % (CMP_STRIDE * V7X_BF16_SUBLANES) == 0 and n % V7X_LANES == 0
    vmem = 2 * tm * d * 2 + 2 * d * n * 2 + 2 * tm * n * 2 + 2 * tm * n * 4
    return pl.pallas_call(
        _proj_chunk_major_kernel,
        grid=(m // tm,),
        in_specs=[pl.BlockSpec((tm, d), lambda i: (i, 0)),
                  pl.BlockSpec((d, n), lambda i: (0, 0))],
        out_specs=pl.BlockSpec((tm // CMP_STRIDE, CMP_STRIDE * n), lambda i: (i, 0)),
        out_shape=jax.ShapeDtypeStruct((m // CMP_STRIDE, CMP_STRIDE * n), BF16),
        scratch_shapes=[pltpu.VMEM((n // V7X_LANES, tm, V7X_LANES), F32)],
        compiler_params=_cparams(("parallel",), vmem),
        name=name,
    )(xn, w)


def _plain_proj(xn, w, out_dtype, tm, name):
    m, d = xn.shape
    n = w.shape[1]
    assert m % tm == 0
    osz = jnp.dtype(out_dtype).itemsize
    vmem = 2 * tm * d * 2 + 2 * d * n * 2 + 2 * tm * n * osz + tm * n * 4
    return pl.pallas_call(
        _plain_proj_kernel,
        grid=(m // tm,),
        in_specs=[pl.BlockSpec((tm, d), lambda i: (i, 0)),
                  pl.BlockSpec((d, n), lambda i: (0, 0))],
        out_specs=pl.BlockSpec((tm, n), lambda i: (i, 0)),
        out_shape=jax.ShapeDtypeStruct((m, n), out_dtype),
        compiler_params=_cparams(("parallel",), vmem),
        name=name,
    )(xn, w)


def _rms_proj(x2d, g, w, col_scale, out_dtype, tm, tn, name):
    m, d = x2d.shape
    n = w.shape[1]
    assert m % tm == 0 and n % tn == 0 and col_scale.shape == (1, n)
    osz = jnp.dtype(out_dtype).itemsize
    vmem = 2 * tm * d * 4 + 2 * tm * d * 2 + 2 * d * tn * 2 + 2 * tm * tn * osz + tm * tn * 4
    return pl.pallas_call(
        functools.partial(_proj_kernel, chunk=min(tm, 128)),
        grid=(m // tm, n // tn),
        in_specs=[pl.BlockSpec((tm, d), lambda i, j: (i, 0)),
                  pl.BlockSpec((1, d), lambda i, j: (0, 0)),
                  pl.BlockSpec((d, tn), lambda i, j: (0, j)),
                  pl.BlockSpec((1, tn), lambda i, j: (0, j))],
        out_specs=[pl.BlockSpec((tm, tn), lambda i, j: (i, j)),
                   pl.BlockSpec((tm, d), lambda i, j: (i, 0))],
        out_shape=[jax.ShapeDtypeStruct((m, n), out_dtype), jax.ShapeDtypeStruct((m, d), BF16)],
        compiler_params=_cparams(("parallel", "arbitrary"), vmem),
        name=name,
    )(x2d, g, w, col_scale)


def _sb_kernel(q_ref, k_ref, v_ref, o_ref, acc_ref, run_ref):
    i = pl.program_id(2)
    tq, tk = q_ref.shape[1], SB_TK
    n_diag = tq // tk
    row = lax.broadcasted_iota(jnp.int32, (tq, tk), 0)
    col = lax.broadcasted_iota(jnp.int32, (tq, tk), 1)
    below = row > col
    suffix = jnp.where(below[:tk], 1.0, 0.0).astype(BF16)

    def tile(j, row0, diag):
        start = pl.multiple_of(j * tk, tk)
        rows = slice(row0, tq)
        causal = below[:tq - row0]
        n_rows = tq - row0
        log_betas, rests, runs = [], [], []
        for hh in range(SB_HEADS_PER_STEP):
            lanes = slice(hh * HEAD_DIM, (hh + 1) * HEAD_DIM)
            z = lax.dot_general(q_ref[0, rows, lanes], k_ref[0, pl.ds(start, tk), lanes], _NT,
                                preferred_element_type=F32)
            sp = jnp.log2(1.0 + jnp.exp2(-jnp.abs(z)))
            neg_rest = jnp.maximum(z, 0.0) + sp
            log_beta = z - neg_rest
            if diag:
                neg_rest = jnp.where(causal, neg_rest, 0.0)
            run = run_ref[hh, rows]
            log_betas.append(log_beta - run)
            rests.append(neg_rest.astype(BF16))
            run_ref[hh, rows] = run + jnp.sum(neg_rest, axis=-1, keepdims=True)
        later_all = jnp.dot(jnp.concatenate(rests, axis=0), suffix, preferred_element_type=F32)
        for hh in range(SB_HEADS_PER_STEP):
            lanes = slice(hh * HEAD_DIM, (hh + 1) * HEAD_DIM)
            a = jnp.exp2(log_betas[hh] - later_all[hh * n_rows:(hh + 1) * n_rows])
            if diag:
                a = jnp.where(causal, a, 0.0)
            acc_ref[hh, rows] += jnp.dot(a.astype(BF16), v_ref[0, pl.ds(start, tk), lanes],
                                         preferred_element_type=F32)

    acc_ref[...] = jnp.zeros_like(acc_ref)
    run_ref[...] = jnp.zeros_like(run_ref)
    for c in reversed(range(n_diag)):
        tile(i * n_diag + c, c * tk, True)

    def walk(carry):
        jj, _ = carry
        tile(i * n_diag - 1 - jj, 0, False)
        return jj + 1, jnp.min(run_ref[...])

    def alive(carry):
        jj, least_run = carry
        return (jj < i * n_diag) & (least_run < SB_DEAD_LOG2)
    lax.while_loop(alive, walk, (jnp.int32(0), jnp.min(run_ref[...])))
    for hh in range(SB_HEADS_PER_STEP):
        o_ref[0, :, hh * HEAD_DIM:(hh + 1) * HEAD_DIM] = acc_ref[hh]


def _sb_attention(proj_a, batch, seq):
    tq = min(SB_TQ, seq)
    hps = SB_HEADS_PER_STEP
    width = hps * HEAD_DIM
    assert seq % tq == 0 and tq % SB_TK == 0 and SB_HEADS % hps == 0
    kblk = SB_WIDTH // width
    vmem = 2 * seq * width * 2 + 6 * tq * width * 4 + hps * 12 * tq * SB_TK * 4
    whole_seq = pl.Buffered(1)
    return pl.pallas_call(
        _sb_kernel,
        grid=(batch, SB_HEADS // hps, seq // tq),
        in_specs=[pl.BlockSpec((1, tq, width), lambda b, h, i: (b, i, h)),
                  pl.BlockSpec((1, seq, width), lambda b, h, i: (b, 0, kblk + h), pipeline_mode=whole_seq),
                  pl.BlockSpec((1, seq, width), lambda b, h, i: (b, 0, 2 * kblk + h), pipeline_mode=whole_seq)],
        out_specs=pl.BlockSpec((1, tq, width), lambda b, h, i: (b, i, h)),
        out_shape=jax.ShapeDtypeStruct((batch, seq, SB_WIDTH), F32),
        scratch_shapes=[pltpu.VMEM((hps, tq, HEAD_DIM), F32), pltpu.VMEM((hps, tq, 1), F32)],
        compiler_params=_cparams(("parallel", "parallel", "arbitrary"), vmem),
        name="sb_attention",
    )(proj_a, proj_a, proj_a)


def _compress_kernel(cv_ref, pos_ref, w1_ref, w2_ref, ko_ref, vo_ref):
    nbp = cv_ref.shape[1]
    tok = 2 * NSA_KV_WIDTH
    for which, out_ref in ((0, ko_ref), (1, vo_ref)):
        for g in range(NSA_KV_GROUPS):
            c0 = (which * NSA_KV_GROUPS + g) * HEAD_DIM
            chunks = jnp.concatenate(
                [cv_ref[0, :, l * tok + c0:l * tok + c0 + HEAD_DIM] for l in range(CMP_STRIDE)],
                axis=1).astype(F32)
            first = jnp.dot((chunks + pos_ref[which, 0]).astype(BF16), w1_ref[which, 0],
                            preferred_element_type=F32)
            second = jnp.dot((chunks + pos_ref[which, 1]).astype(BF16), w1_ref[which, 1],
                             preferred_element_type=F32)
            hid = jax.nn.gelu(first + pltpu.roll(second, nbp - 1, 0)).astype(BF16)
            out_ref[0, g] = jnp.dot(hid, w2_ref[which], preferred_element_type=F32).astype(BF16)


def _compress(cv, pos, w1, w2, batch, nbp):
    tokw = CMP_STRIDE * 2 * NSA_KV_WIDTH
    half = CMP_STRIDE * HEAD_DIM
    vmem = 2 * nbp * tokw * 2 + 2 * 4 * half * CMP_HIDDEN * 2 + 8 * nbp * half * 4
    out_sds = [jax.ShapeDtypeStruct((batch, NSA_KV_GROUPS, nbp, HEAD_DIM), BF16)] * 2
    out_spec = [pl.BlockSpec((1, NSA_KV_GROUPS, nbp, HEAD_DIM), lambda b: (b, 0, 0, 0))] * 2
    return pl.pallas_call(
        _compress_kernel,
        grid=(batch,),
        in_specs=[pl.BlockSpec((1, nbp, tokw), lambda b: (b, 0, 0)),
                  pl.BlockSpec((2, 2, 1, half), lambda b: (0, 0, 0, 0)),
                  pl.BlockSpec((2, 2, half, CMP_HIDDEN), lambda b: (0, 0, 0, 0)),
                  pl.BlockSpec((2, CMP_HIDDEN, HEAD_DIM), lambda b: (0, 0, 0))],
        out_specs=out_spec,
        out_shape=out_sds,
        compiler_params=_cparams(("parallel",), vmem),
        name="kv_compress",
    )(cv, pos, w1, w2)


def _nsa_kernel(q_ref, kc_ref, vc_ref, ks_ref, vs_ref, kw_ref, vw_ref, gate_ref, tbl_ref, tblc_ref,
                ovt_ref, expt_ref, o_ref, acc_ref, m_ref):
    i = pl.program_id(1)
    tq, tk, rows = NSA_TQ, NSA_TK, NSA_ROWS
    gw = NSA_GROUP_SIZE * HEAD_DIM
    seq = expt_ref.shape[0]
    nsel = seq // SEL_BLOCK
    t0 = i * tq
    ovt = ovt_ref[...]
    jrow = lax.broadcasted_iota(jnp.int32, (nsel, tq), 0)
    tcol = t0 + lax.broadcasted_iota(jnp.int32, (nsel, tq), 1)
    cur = tcol // SEL_BLOCK
    valid = jrow * SEL_BLOCK <= tcol
    forced = (jrow == 0) | (jrow == cur) | (jrow == cur - 1)
    slab_row = lax.broadcasted_iota(jnp.int32, (8, tq), 0)

    def compressed_and_selection(g):
        qb = q_ref[0, :, g * gw:(g + 1) * gw]
        q4 = jnp.concatenate([qb[:, r * HEAD_DIM:(r + 1) * HEAD_DIM] for r in range(NSA_GROUP_SIZE)], axis=0)
        s_c = lax.dot_general(q4, kc_ref[0, g], _NT, preferred_element_type=F32) + tblc_ref[g, 0]
        reps = s_c.shape[1] // V7X_LANES
        m_c = jnp.broadcast_to(jnp.max(s_c, axis=-1, keepdims=True), (rows, V7X_LANES))
        e_c = jnp.exp2(s_c - jnp.concatenate([m_c] * reps, axis=1))
        l_c = jnp.broadcast_to(jnp.sum(e_c, axis=-1, keepdims=True), (rows, V7X_LANES))
        inv = jnp.where(m_c > 0.5 * MASK_VALUE, 1.0 / l_c, 0.0)
        p_c = e_c * jnp.concatenate([inv] * reps, axis=1)
        o_c = jnp.dot(p_c.astype(BF16), vc_ref[0, g], preferred_element_type=F32)

        p_sum = p_c[0:tq]
        for r in range(1, NSA_GROUP_SIZE):
            p_sum = p_sum + p_c[r * tq:(r + 1) * tq]
        p_hi = p_sum.astype(BF16)
        p_lo = (p_sum - p_hi.astype(F32)).astype(BF16)
        imp = (lax.dot_general(ovt, p_hi, _NT, preferred_element_type=F32)
               + lax.dot_general(ovt, p_lo, _NT, preferred_element_type=F32))[:nsel]
        score = jnp.where(valid, jnp.where(forced, SEL_FORCE, imp), -SEL_FORCE)
        score3 = score.reshape(nsel // 8, 8, tq)
        slabs = [score3[n] for n in range(nsel // 8)]
        ranks = [jnp.zeros((8, tq), F32) for _ in slabs]
        for other in range(nsel):
            srow = slabs[other // 8][other % 8:other % 8 + 1, :]
            for n, slab in enumerate(slabs):
                if 8 * n > other:
                    before = srow >= slab
                elif 8 * n + 7 <= other:
                    before = srow > slab
                else:
                    before = (srow > slab) | ((srow >= slab) & (slab_row > other - 8 * n))
                ranks[n] = ranks[n] + jnp.where(before, 1.0, 0.0)
        rank = jnp.stack(ranks, axis=0).reshape(nsel, tq)
        drop = jnp.where(rank < float(min(SEL_TOP_N, nsel)), 0.0, SEL_MASK)
        if nsel < V7X_LANES:
            drop = jnp.concatenate([drop, jnp.zeros((V7X_LANES - nsel, tq), F32)], axis=0)
        drop_q = drop.T.astype(BF16)
        q4_sel = jnp.concatenate([q4, jnp.concatenate([drop_q] * NSA_GROUP_SIZE, axis=0)], axis=1)
        return q4, q4_sel, o_c

    per_group = [compressed_and_selection(g) for g in range(NSA_KV_GROUPS)]

    selected = [(g, 0) for g in range(NSA_KV_GROUPS)]
    both = selected + [(g, 1) for g in range(NSA_KV_GROUPS)]
    ones = jnp.ones((tk, HEAD_DIM), BF16)

    def flash(jt, chains, first):
        start = pl.multiple_of(jt * tk, tk)
        d0 = i - jt
        for g, branch in chains:
            q4, q4_sel, _ = per_group[g]
            lanes = slice(g * HEAD_DIM, (g + 1) * HEAD_DIM)
            if branch == 1:
                bias = tbl_ref[g, 0 if first else jnp.where(d0 == WIN_LAST_D0, BIAS_WIN_IDX, d0)]
                s = lax.dot_general(q4, kw_ref[0, pl.ds(start, tk), lanes], _NT, preferred_element_type=F32)
                v = vw_ref[0, pl.ds(start, tk), lanes]
            else:
                bias = tbl_ref[g, 0 if first else jnp.minimum(d0, BIAS_CONST_IDX)]
                k_sel = jnp.concatenate([ks_ref[0, pl.ds(start, tk), lanes], expt_ref[pl.ds(start, tk), :]], axis=1)
                s = lax.dot_general(q4_sel, k_sel, _NT, preferred_element_type=F32)
                v = vs_ref[0, pl.ds(start, tk), lanes]
            s = s + bias
            v_aug = jnp.concatenate([v, ones], axis=1)
            m_tile = jnp.broadcast_to(jnp.max(s, axis=-1, keepdims=True), (rows, V7X_LANES))
            if first:
                m_new = m_tile
                p = jnp.exp2(s - jnp.concatenate([m_new] * (tk // V7X_LANES), axis=1))
                acc_ref[g, branch] = jnp.dot(p.astype(BF16), v_aug, preferred_element_type=F32)
            else:
                m_old = m_ref[g, branch]
                m_new = jnp.maximum(m_old, m_tile)
                p = jnp.exp2(s - jnp.concatenate([m_new] * (tk // V7X_LANES), axis=1))
                alpha = jnp.exp2(m_old - m_new)
                acc_ref[g, branch] = (jnp.concatenate([alpha] * 2, axis=1) * acc_ref[g, branch]
                                      + jnp.dot(p.astype(BF16), v_aug, preferred_element_type=F32))
            m_ref[g, branch] = m_new

    flash(i, both, True)

    @pl.when(i >= 2)
    def _():
        flash(i - 1, both, False)
        flash(i - 2, both, False)

    @pl.when(i == 1)
    def _():
        flash(0, both, False)

    remaining = jnp.maximum(i - 2, 0)
    trips = remaining >> NSA_TRIP_SHIFT
    rest = remaining & (NSA_TILES_PER_TRIP - 1)

    def trip_body(p, carry):
        for u in range(NSA_TILES_PER_TRIP):
            flash(i - 3 - NSA_TILES_PER_TRIP * p - u, selected, False)
        return carry
    lax.fori_loop(0, trips, trip_body, 0)

    for n_left in range(1, NSA_TILES_PER_TRIP):
        @pl.when(rest == n_left)
        def _(n_left=n_left):
            for jt in reversed(range(n_left)):
                flash(jt, selected, False)

    gate = 1.0 / (1.0 + jnp.exp(-gate_ref[0]))
    for g in range(NSA_KV_GROUPS):
        o_c = per_group[g][2]
        for r in range(NSA_GROUP_SIZE):
            rs = slice(r * tq, (r + 1) * tq)
            o_s = acc_ref[g, 0, rs, 0:HEAD_DIM] / acc_ref[g, 0, rs, HEAD_DIM:2 * HEAD_DIM]
            o_w = acc_ref[g, 1, rs, 0:HEAD_DIM] / acc_ref[g, 1, rs, HEAD_DIM:2 * HEAD_DIM]
            c = (g * NSA_GROUP_SIZE + r) * N_GATES
            out = gate[:, c:c + 1] * o_c[rs] + gate[:, c + 1:c + 2] * o_s + gate[:, c + 2:c + 3] * o_w
            col = (g * NSA_GROUP_SIZE + r) * HEAD_DIM
            o_ref[0, :, col:col + HEAD_DIM] = out


def _nsa_attention(proj_a, proj_b, kcmp, vcmp, tbl, tblc, ovt, expt, batch, seq, cols):
    tq, tk, rows = NSA_TQ, NSA_TK, NSA_ROWS
    nbp = kcmp.shape[2]
    n_tiles = tbl.shape[1]
    groups = NSA_KV_GROUPS
    qblk = cols["n_q"] // NSA_WIDTH
    gblk = cols["gates"] // V7X_LANES
    assert cols["n_q"] % NSA_WIDTH == 0 and cols["gates"] % V7X_LANES == 0 and N_GATES * NSA_HEADS <= V7X_LANES

    def resident(shape, index_map):
        return pl.BlockSpec(shape, index_map, pipeline_mode=pl.Buffered(1))

    def kv_spec(name):
        assert cols[name] % NSA_KV_WIDTH == 0
        blk = cols[name] // NSA_KV_WIDTH
        return resident((1, seq, NSA_KV_WIDTH), lambda b, i: (b, 0, blk))

    cmp_spec = resident((1, groups, nbp, HEAD_DIM), lambda b, i: (b, 0, 0, 0))
    vmem = (4 * seq * NSA_KV_WIDTH * 2 + groups * n_tiles * rows * tk * 4 + 2 * groups * rows * nbp * 4
            + seq * V7X_LANES * 2 + 2 * groups * rows * (2 * HEAD_DIM + V7X_LANES) * 4
            + 4 * tq * (NSA_WIDTH + groups * V7X_LANES) * 4
            + 12 * rows * tk * 4)
    return pl.pallas_call(
        _nsa_kernel,
        grid=(batch, seq // tq),
        in_specs=[pl.BlockSpec((1, tq, NSA_WIDTH), lambda b, i: (b, i, qblk)),
                  cmp_spec, cmp_spec,
                  kv_spec("ks"), kv_spec("vs"), kv_spec("kw"), kv_spec("vw"),
                  pl.BlockSpec((1, tq, V7X_LANES), lambda b, i: (b, i, gblk)),
                  resident((groups, n_tiles, rows, tk), lambda b, i: (0, 0, 0, 0)),
                  pl.BlockSpec((groups, 1, rows, nbp), lambda b, i: (0, i, 0, 0)),
                  resident((V7X_LANES, nbp), lambda b, i: (0, 0)),
                  resident((seq, V7X_LANES), lambda b, i: (0, 0))],
        out_specs=pl.BlockSpec((1, tq, NSA_WIDTH), lambda b, i: (b, i, 0)),
        out_shape=jax.ShapeDtypeStruct((batch, seq, NSA_WIDTH), F32),
        scratch_shapes=[pltpu.VMEM((groups, 2, rows, 2 * HEAD_DIM), F32),
                        pltpu.VMEM((groups, 2, rows, V7X_LANES), F32)],
        compiler_params=_cparams(("parallel", "arbitrary"), vmem),
        name="nsa_attention",
    )(proj_a, kcmp, vcmp, proj_a, proj_a, proj_a, proj_a, proj_b, tbl, tblc, ovt, expt)


def _selection_constants(seq, nbp):
    nb = seq // CMP_STRIDE - 1
    nsel = seq // SEL_BLOCK
    assert nsel <= V7X_LANES and nbp >= nb
    ci = np.arange(nbp)[None, :] * CMP_STRIDE
    sj = np.arange(V7X_LANES)[:, None] * SEL_BLOCK
    ovt = ((ci < sj + SEL_BLOCK) & (ci + CMP_BLOCK > sj) & (np.arange(nbp)[None, :] < nb)
           & (np.arange(V7X_LANES)[:, None] < nsel))
    expt = (np.arange(seq)[:, None] // SEL_BLOCK) == np.arange(V7X_LANES)[None, :]
    return jnp.asarray(ovt, BF16), jnp.asarray(expt, BF16)


def _nsa_branch(proj_a3, proj_b3, proj_cv3, cmp_pos, cmp_w1, cmp_w2, rel_bias, cols):
    batch, seq, _ = proj_a3.shape
    nbp = seq // CMP_STRIDE
    half = CMP_STRIDE * HEAD_DIM
    pos = jnp.stack(cmp_pos).reshape(2, 2, 1, half)
    w1 = jnp.stack(cmp_w1).reshape(2, 2, half, CMP_HIDDEN).astype(BF16)
    w2 = jnp.stack(cmp_w2).astype(BF16)
    kcmp, vcmp = _compress(proj_cv3, pos, w1, w2, batch, nbp)

    tbl = _bias_table(rel_bias, BIAS_TILES, NSA_TK)
    tblc = _cmp_bias_table(rel_bias, seq // NSA_TQ, nbp)
    ovt, expt = _selection_constants(seq, nbp)
    return _nsa_attention(proj_a3, proj_b3, kcmp, vcmp, tbl, tblc, ovt, expt, batch, seq, cols)


def _out_kernel(osb_ref, zsb_ref, onsa_ref, znsa_ref, x_ref, gsb_ref, gnsa_ref, gfin_ref, wsb_ref, wnsa_ref,
                o_ref):
    def gated(o_r, z_r, g_r, rows):
        o = o_r[rows, :]
        y = o * lax.rsqrt(jnp.mean(o * o, axis=-1, keepdims=True) + RMS_EPS) * g_r[...]
        z = z_r[rows, :]
        return (y * (z * (1.0 / (1.0 + jnp.exp(-z))))).astype(BF16)

    for c in range(x_ref.shape[0] // OUT_CHUNK):
        rows = slice(c * OUT_CHUNK, (c + 1) * OUT_CHUNK)
        mixed = (jnp.dot(gated(osb_ref, zsb_ref, gsb_ref, rows), wsb_ref[...], preferred_element_type=F32)
                 + jnp.dot(gated(onsa_ref, znsa_ref, gnsa_ref, rows), wnsa_ref[...], preferred_element_type=F32))
        h = x_ref[rows, :] + mixed
        o_ref[rows, :] = h * lax.rsqrt(jnp.mean(h * h, axis=-1, keepdims=True) + RMS_EPS) * gfin_ref[...]


def _output_stage(o_sb, o_nsa, proj_b, x2d, g_sb, g_nsa, g_fin, w_out, tm):
    m, d = x2d.shape
    half = SB_WIDTH
    assert m % tm == 0 and tm % OUT_CHUNK == 0
    vmem = 2 * (4 * tm * half * 4 + 2 * tm * d * 4) + 2 * half * d * 2 + 6 * OUT_CHUNK * d * 4
    row = lambda c: pl.BlockSpec((tm, half), lambda i: (i, c))
    vec = lambda n: pl.BlockSpec((1, n), lambda i: (0, 0))
    weights = pl.Buffered(1)
    return pl.pallas_call(
        _out_kernel,
        grid=(m // tm,),
        in_specs=[row(0), row(0), row(0), row(1),
                  pl.BlockSpec((tm, d), lambda i: (i, 0)),
                  vec(half), vec(half), vec(d),
                  pl.BlockSpec((half, d), lambda i: (0, 0), pipeline_mode=weights),
                  pl.BlockSpec((half, d), lambda i: (1, 0), pipeline_mode=weights)],
        out_specs=pl.BlockSpec((tm, d), lambda i: (i, 0)),
        out_shape=jax.ShapeDtypeStruct((m, d), F32),
        compiler_params=_cparams(("parallel",), vmem),
        name="output_stage",
    )(o_sb, proj_b, o_nsa, proj_b, x2d, g_sb, g_nsa, g_fin, w_out, w_out)


def kernel(x, norm_in, w_in, cmp_k_pos, cmp_k_w1, cmp_k_w2, cmp_v_pos, cmp_v_w1, cmp_v_w2,
           rel_bias, norm_sb, norm_nsa, w_out, norm_final):
    batch, seq, d_model = x.shape
    assert w_in.shape[0] == 1, "single-layer trunk: the final norm is fused into the output stage"
    m = batch * seq
    nbp = seq // CMP_STRIDE
    assert seq % NSA_TK == 0 and nbp % V7X_LANES == 0

    sizes = (SB_WIDTH,) * 4 + (NSA_WIDTH,) + (NSA_KV_WIDTH,) * 6 + (N_GATES * NSA_HEADS, NSA_WIDTH)
    off = np.concatenate([[0], np.cumsum(sizes)])
    (c_sbq, c_sbk, c_sbv, c_sbz, c_nq, c_kc, c_vc, c_ks, c_vs, c_kw, c_vw, c_gate, c_nz, c_end) = [int(o) for o in off]
    cols = {"n_q": 3 * SB_WIDTH, "ks": 3 * SB_WIDTH + NSA_WIDTH}
    cols["vs"] = cols["ks"] + NSA_KV_WIDTH
    cols["kw"] = cols["vs"] + NSA_KV_WIDTH
    cols["vw"] = cols["kw"] + NSA_KV_WIDTH
    cols["gates"] = 2 * SB_WIDTH

    w = w_in[0]
    w_a = jnp.concatenate([w[:, c_sbq:c_sbz], w[:, c_nq:c_kc], w[:, c_ks:c_gate]], axis=1).astype(BF16)
    w_cv = w[:, c_kc:c_ks].astype(BF16)
    gate_pad = jnp.zeros((d_model, V7X_LANES - N_GATES * NSA_HEADS), w.dtype)
    w_b = jnp.concatenate([w[:, c_sbz:c_nq], w[:, c_nz:c_end], w[:, c_gate:c_nz], gate_pad], axis=1).astype(BF16)
    g_in = norm_in[0].reshape(1, d_model)
    x2d = x.reshape(m, d_model)

    q_scale = np.ones((1, w_a.shape[1]), np.float32)
    q_scale[:, c_sbq:c_sbq + SB_WIDTH] = SCALE * LOG2E
    q_scale[:, cols["n_q"]:cols["n_q"] + NSA_WIDTH] = SCALE * LOG2E
    proj_a, xn = _rms_proj(x2d, g_in, w_a, jnp.asarray(q_scale), BF16, PROJ_TM, w_a.shape[1] // PROJ_COL_TILES,
                           "proj_attn")
    proj_cv = _proj_chunk_major(xn, w_cv, PROJ_TM, "proj_cmp")
    proj_b = _plain_proj(xn, w_b, F32, PROJ_TM // 2, "proj_gate")

    proj_a3 = proj_a.reshape(batch, seq, -1)
    o_sb = _sb_attention(proj_a3, batch, seq)
    o_nsa = _nsa_branch(proj_a3, proj_b.reshape(batch, seq, -1), proj_cv.reshape(batch, nbp, -1),
                        (cmp_k_pos[0], cmp_v_pos[0]), (cmp_k_w1[0], cmp_v_w1[0]), (cmp_k_w2[0], cmp_v_w2[0]),
                        rel_bias, cols)

    out = _output_stage(o_sb.reshape(m, SB_WIDTH), o_nsa.reshape(m, NSA_WIDTH), proj_b, x2d,
                        norm_sb[0].reshape(1, SB_WIDTH), norm_nsa[0].reshape(1, NSA_WIDTH),
                        norm_final.reshape(1, d_model), w_out[0].astype(BF16), OUT_TM)
    return out.reshape(batch, seq, d_model)
```

```python
import functools
import math

import numpy as np
import jax
import jax.numpy as jnp
from jax import lax
from jax.experimental import pallas as pl
from jax.experimental.pallas import tpu as pltpu

F32 = jnp.float32
BF16 = jnp.bfloat16

HEAD_DIM = 128
SB_HEADS = 8
NSA_HEADS = 8
NSA_KV_GROUPS = 2
NSA_GROUP_SIZE = NSA_HEADS // NSA_KV_GROUPS
SB_WIDTH = SB_HEADS * HEAD_DIM
NSA_WIDTH = NSA_HEADS * HEAD_DIM
NSA_KV_WIDTH = NSA_KV_GROUPS * HEAD_DIM
N_GATES = 3
CMP_STRIDE = 16
CMP_BLOCK = 2 * CMP_STRIDE
CMP_HIDDEN = 256
SEL_BLOCK = 64
SEL_TOP_N = 16
WINDOW = 512
REL_BUCKETS = 32
REL_MAX_EXACT = 16
REL_MAX_DISTANCE = 1024
RMS_EPS = 1e-6
SEL_FORCE = 1e9
MASK_VALUE = -1e30
SEL_MASK = -float(2 ** 100)
SCALE = 1.0 / math.sqrt(HEAD_DIM)
LOG2E = math.log2(math.e)

V7X_LANES = 128
V7X_MXU_DIM = 256
V7X_BF16_SUBLANES = 16
V7X_VMEM_BYTES = 64 * 1024 * 1024
VMEM_SPILL_BYTES = 8 * 1024 * 1024

SB_TQ = V7X_MXU_DIM
SB_TK = V7X_MXU_DIM
SB_HEADS_PER_STEP = 8
SB_DEAD_LOG2 = 160.0
NSA_TQ = V7X_MXU_DIM
NSA_TK = V7X_MXU_DIM
NSA_ROWS = NSA_GROUP_SIZE * NSA_TQ
NSA_TRIP_SHIFT = 2
NSA_TILES_PER_TRIP = 1 << NSA_TRIP_SHIFT
PROJ_TM = 4 * V7X_MXU_DIM
PROJ_COL_TILES = 4
OUT_TM = 2 * V7X_MXU_DIM
OUT_CHUNK = V7X_MXU_DIM

_NT = (((1,), (1,)), ((), ()))


def _bucket_thresholds():
    n = np.arange(0, 4 * REL_MAX_DISTANCE)
    nf = np.maximum(n, 1).astype(np.float64)
    large = REL_MAX_EXACT + (np.log(nf / REL_MAX_EXACT) / math.log(REL_MAX_DISTANCE / REL_MAX_EXACT)
                             * (REL_BUCKETS - REL_MAX_EXACT)).astype(np.int64)
    bucket = np.where(n < REL_MAX_EXACT, n, np.minimum(large, REL_BUCKETS - 1))
    assert np.all(np.diff(bucket) >= 0) and bucket[-1] == REL_BUCKETS - 1
    return [int(n[bucket >= k][0]) for k in range(1, REL_BUCKETS)]


BUCKET_THR = _bucket_thresholds()
LAST_THR = BUCKET_THR[-1]
BIAS_CONST_IDX = -(-(LAST_THR + NSA_TK - 1) // NSA_TQ)
BIAS_FAR = 1 << 20
WIN_LAST_D0 = 2
BIAS_WIN_IDX = BIAS_CONST_IDX + 1
BIAS_TILES = BIAS_CONST_IDX + 2
assert NSA_TQ == NSA_TK and NSA_TQ + NSA_TK - 1 < WINDOW <= WIN_LAST_D0 * NSA_TQ < BIAS_CONST_IDX * NSA_TQ


def _cparams(sem, block_bytes):
    limit = min(block_bytes + VMEM_SPILL_BYTES, V7X_VMEM_BYTES)
    return pltpu.CompilerParams(dimension_semantics=sem, vmem_limit_bytes=int(limit))


def _bias_table_kernel(rb_ref, o_ref):
    h = pl.program_id(0)
    tile = pl.program_id(1)
    rows, cols = o_ref.shape[2], o_ref.shape[3]
    base = jnp.where(tile == BIAS_CONST_IDX, BIAS_FAR,
                     jnp.where(tile == BIAS_WIN_IDX, WIN_LAST_D0 * NSA_TQ, tile * NSA_TQ))
    ti = lax.broadcasted_iota(jnp.int32, (rows, cols), 0)
    cj = lax.broadcasted_iota(jnp.int32, (rows, cols), 1)
    dist = base + ti - cj
    val = jnp.full((rows, cols), rb_ref[0, h], F32)
    for k in range(1, REL_BUCKETS):
        val = jnp.where(dist >= BUCKET_THR[k - 1], rb_ref[k, h], val)
    keep = ((dist >= 0) & (dist < WINDOW)) | ((tile != 0) & (tile != BIAS_WIN_IDX))
    o_ref[0, 0] = val * LOG2E + jnp.where(keep, 0.0, MASK_VALUE)


def _bias_lookup(rb_ref, h, dist):
    val = jnp.full(dist.shape, rb_ref[0, h], F32)
    for k in range(1, REL_BUCKETS):
        val = jnp.where(dist >= BUCKET_THR[k - 1], rb_ref[k, h], val)
    return val * LOG2E + jnp.where(dist >= 0, 0.0, MASK_VALUE)


def _cmp_bias_table_kernel(rb_ref, o_ref):
    h = pl.program_id(0)
    n_tiles, tq, nbp = o_ref.shape[1], o_ref.shape[2], o_ref.shape[3]
    step = tq // CMP_STRIDE
    assert n_tiles * step <= nbp
    ti = lax.broadcasted_iota(jnp.int32, (tq, 2 * nbp), 0)
    cj = lax.broadcasted_iota(jnp.int32, (tq, 2 * nbp), 1)
    master = _bias_lookup(rb_ref, h, ti - CMP_STRIDE * (cj - nbp) - (CMP_BLOCK - 1))
    for i in range(n_tiles):
        o_ref[0, i] = pltpu.roll(master, nbp + step * i, 1)[:, :nbp]


def _cmp_bias_table(rel_bias, n_tiles, nbp):
    block = (1, n_tiles, NSA_TQ, nbp)
    return pl.pallas_call(
        _cmp_bias_table_kernel,
        grid=(NSA_HEADS,),
        in_specs=[pl.BlockSpec(memory_space=pltpu.SMEM)],
        out_specs=pl.BlockSpec(block, lambda h: (h // NSA_GROUP_SIZE, 0, h % NSA_GROUP_SIZE, 0)),
        out_shape=jax.ShapeDtypeStruct((NSA_KV_GROUPS, n_tiles, NSA_ROWS, nbp), F32),
        compiler_params=_cparams(("parallel",), 4 * n_tiles * NSA_TQ * nbp * 4),
        name="cmp_bias_table",
    )(rel_bias)


def _bias_table(rel_bias, n_tiles, cols):
    return pl.pallas_call(
        _bias_table_kernel,
        grid=(NSA_HEADS, n_tiles),
        in_specs=[pl.BlockSpec(memory_space=pltpu.SMEM)],
        out_specs=pl.BlockSpec((1, 1, NSA_TQ, cols),
                               lambda h, t: (h // NSA_GROUP_SIZE, t, h % NSA_GROUP_SIZE, 0)),
        out_shape=jax.ShapeDtypeStruct((NSA_KV_GROUPS, n_tiles, NSA_ROWS, cols), F32),
        compiler_params=_cparams(("parallel", "parallel"), 2 * NSA_TQ * cols * 4),
        name="bias_table",
    )(rel_bias)


def _normalise_rows(x_ref, g_ref, xn_ref, chunk):
    def body(c, carry):
        rows = pl.ds(pl.multiple_of(c * chunk, chunk), chunk)
        x = x_ref[rows, :]
        ms = jnp.mean(x * x, axis=-1, keepdims=True)
        xn_ref[rows, :] = (x * lax.rsqrt(ms + RMS_EPS) * g_ref[...]).astype(BF16)
        return carry
    lax.fori_loop(0, x_ref.shape[0] // chunk, body, 0)


def _proj_kernel(x_ref, g_ref, w_ref, cs_ref, o_ref, xn_ref, *, chunk):
    @pl.when(pl.program_id(1) == 0)
    def _():
        _normalise_rows(x_ref, g_ref, xn_ref, chunk)

    acc = jnp.dot(xn_ref[...], w_ref[...], preferred_element_type=F32)
    o_ref[...] = (acc * cs_ref[...]).astype(o_ref.dtype)


def _plain_proj_kernel(xn_ref, w_ref, o_ref):
    o_ref[...] = jnp.dot(xn_ref[...], w_ref[...], preferred_element_type=F32).astype(o_ref.dtype)


def _proj_chunk_major_kernel(xn_ref, w_ref, o_ref, acc_ref):
    acc = jnp.dot(xn_ref[...], w_ref[...], preferred_element_type=F32)
    n = acc.shape[1]
    slabs = n // V7X_LANES
    for s in range(slabs):
        acc_ref[s] = acc[:, s * V7X_LANES:(s + 1) * V7X_LANES]
    n_chunks = o_ref.shape[0]
    for l in range(CMP_STRIDE):
        for s in range(slabs):
            tokens = acc_ref[s, pl.ds(l, n_chunks, stride=CMP_STRIDE), :]
            o_ref[:, l * n + s * V7X_LANES:l * n + (s + 1) * V7X_LANES] = tokens.astype(o_ref.dtype)


def _proj_chunk_major(xn, w, tm, name):
    m, d = xn.shape
    n = w.shape[1]
    assert m % tm == 0 and tm % (CMP_STRIDE * V7X_BF16_SUBLANES) == 0 and n % V7X_LANES == 0
    vmem = 2 * tm * d * 2 + 2 * d * n * 2 + 2 * tm * n * 2 + 2 * tm * n * 4
    return pl.pallas_call(
        _proj_chunk_major_kernel,
        grid=(m // tm,),
        in_specs=[pl.BlockSpec((tm, d), lambda i: (i, 0)),
                  pl.BlockSpec((d, n), lambda i: (0, 0))],
        out_specs=pl.BlockSpec((tm // CMP_STRIDE, CMP_STRIDE * n), lambda i: (i, 0)),
        out_shape=jax.ShapeDtypeStruct((m // CMP_STRIDE, CMP_STRIDE * n), BF16),
        scratch_shapes=[pltpu.VMEM((n // V7X_LANES, tm, V7X_LANES), F32)],
        compiler_params=_cparams(("parallel",), vmem),
        name=name,
    )(xn, w)


def _plain_proj(xn, w, out_dtype, tm, name):
    m, d = xn.shape
    n = w.shape[1]
    assert m % tm == 0
    osz = jnp.dtype(out_dtype).itemsize
    vmem = 2 * tm * d * 2 + 2 * d * n * 2 + 2 * tm * n * osz + tm * n * 4
    return pl.pallas_call(
        _plain_proj_kernel,
        grid=(m // tm,),
        in_specs=[pl.BlockSpec((tm, d), lambda i: (i, 0)),
                  pl.BlockSpec((d, n), lambda i: (0, 0))],
        out_specs=pl.BlockSpec((tm, n), lambda i: (i, 0)),
        out_shape=jax.ShapeDtypeStruct((m, n), out_dtype),
        compiler_params=_cparams(("parallel",), vmem),
        name=name,
    )(xn, w)


def _rms_proj(x2d, g, w, col_scale, out_dtype, tm, tn, name):
    m, d = x2d.shape
    n = w.shape[1]
    assert m % tm == 0 and n % tn == 0 and col_scale.shape == (1, n)
    osz = jnp.dtype(out_dtype).itemsize
    vmem = 2 * tm * d * 4 + 2 * tm * d * 2 + 2 * d * tn * 2 + 2 * tm * tn * osz + tm * tn * 4
    return pl.pallas_call(
        functools.partial(_proj_kernel, chunk=min(tm, 128)),
        grid=(m // tm, n // tn),
        in_specs=[pl.BlockSpec((tm, d), lambda i, j: (i, 0)),
                  pl.BlockSpec((1, d), lambda i, j: (0, 0)),
                  pl.BlockSpec((d, tn), lambda i, j: (0, j)),
                  pl.BlockSpec((1, tn), lambda i, j: (0, j))],
        out_specs=[pl.BlockSpec((tm, tn), lambda i, j: (i, j)),
                   pl.BlockSpec((tm, d), lambda i, j: (i, 0))],
        out_shape=[jax.ShapeDtypeStruct((m, n), out_dtype), jax.ShapeDtypeStruct((m, d), BF16)],
        compiler_params=_cparams(("parallel", "arbitrary"), vmem),
        name=name,
    )(x2d, g, w, col_scale)


def _sb_kernel(q_ref, k_ref, v_ref, o_ref, acc_ref, run_ref):
    i = pl.program_id(2)
    tq, tk = q_ref.shape[1], SB_TK
    n_diag = tq // tk
    row = lax.broadcasted_iota(jnp.int32, (tq, tk), 0)
    col = lax.broadcasted_iota(jnp.int32, (tq, tk), 1)
    below = row > col
    suffix = jnp.where(below[:tk], 1.0, 0.0).astype(BF16)

    def tile(j, row0, diag):
        start = pl.multiple_of(j * tk, tk)
        rows = slice(row0, tq)
        causal = below[:tq - row0]
        n_rows = tq - row0
        log_betas, rests, runs = [], [], []
        for hh in range(SB_HEADS_PER_STEP):
            lanes = slice(hh * HEAD_DIM, (hh + 1) * HEAD_DIM)
            z = lax.dot_general(q_ref[0, rows, lanes], k_ref[0, pl.ds(start, tk), lanes], _NT,
                                preferred_element_type=F32)
            sp = jnp.log2(1.0 + jnp.exp2(-jnp.abs(z)))
            neg_rest = jnp.maximum(z, 0.0) + sp
            log_beta = z - neg_rest
            if diag:
                neg_rest = jnp.where(causal, neg_rest, 0.0)
            run = run_ref[hh, rows]
            log_betas.append(log_beta - run)
            rests.append(neg_rest.astype(BF16))
            run_ref[hh, rows] = run + jnp.sum(neg_rest, axis=-1, keepdims=True)
        later_all = jnp.dot(jnp.concatenate(rests, axis=0), suffix, preferred_element_type=F32)
        for hh in range(SB_HEADS_PER_STEP):
            lanes = slice(hh * HEAD_DIM, (hh + 1) * HEAD_DIM)
            a = jnp.exp2(log_betas[hh] - later_all[hh * n_rows:(hh + 1) * n_rows])
            if diag:
                a = jnp.where(causal, a, 0.0)
            acc_ref[hh, rows] += jnp.dot(a.astype(BF16), v_ref[0, pl.ds(start, tk), lanes],
                                         preferred_element_type=F32)

    acc_ref[...] = jnp.zeros_like(acc_ref)
    run_ref[...] = jnp.zeros_like(run_ref)
    for c in reversed(range(n_diag)):
        tile(i * n_diag + c, c * tk, True)

    def walk(carry):
        jj, _ = carry
        tile(i * n_diag - 1 - jj, 0, False)
        return jj + 1, jnp.min(run_ref[...])

    def alive(carry):
        jj, least_run = carry
        return (jj < i * n_diag) & (least_run < SB_DEAD_LOG2)
    lax.while_loop(alive, walk, (jnp.int32(0), jnp.min(run_ref[...])))
    for hh in range(SB_HEADS_PER_STEP):
        o_ref[0, :, hh * HEAD_DIM:(hh + 1) * HEAD_DIM] = acc_ref[hh]


def _sb_attention(proj_a, batch, seq):
    tq = min(SB_TQ, seq)
    hps = SB_HEADS_PER_STEP
    width = hps * HEAD_DIM
    assert seq % tq == 0 and tq % SB_TK == 0 and SB_HEADS % hps == 0
    kblk = SB_WIDTH // width
    vmem = 2 * seq * width * 2 + 6 * tq * width * 4 + hps * 12 * tq * SB_TK * 4
    whole_seq = pl.Buffered(1)
    return pl.pallas_call(
        _sb_kernel,
        grid=(batch, SB_HEADS // hps, seq // tq),
        in_specs=[pl.BlockSpec((1, tq, width), lambda b, h, i: (b, i, h)),
                  pl.BlockSpec((1, seq, width), lambda b, h, i: (b, 0, kblk + h), pipeline_mode=whole_seq),
                  pl.BlockSpec((1, seq, width), lambda b, h, i: (b, 0, 2 * kblk + h), pipeline_mode=whole_seq)],
        out_specs=pl.BlockSpec((1, tq, width), lambda b, h, i: (b, i, h)),
        out_shape=jax.ShapeDtypeStruct((batch, seq, SB_WIDTH), F32),
        scratch_shapes=[pltpu.VMEM((hps, tq, HEAD_DIM), F32), pltpu.VMEM((hps, tq, 1), F32)],
        compiler_params=_cparams(("parallel", "parallel", "arbitrary"), vmem),
        name="sb_attention",
    )(proj_a, proj_a, proj_a)


def _compress_kernel(cv_ref, pos_ref, w1_ref, w2_ref, ko_ref, vo_ref):
    nbp = cv_ref.shape[1]
    tok = 2 * NSA_KV_WIDTH
    for which, out_ref in ((0, ko_ref), (1, vo_ref)):
        for g in range(NSA_KV_GROUPS):
            c0 = (which * NSA_KV_GROUPS + g) * HEAD_DIM
            chunks = jnp.concatenate(
                [cv_ref[0, :, l * tok + c0:l * tok + c0 + HEAD_DIM] for l in range(CMP_STRIDE)],
                axis=1).astype(F32)
            first = jnp.dot((chunks + pos_ref[which, 0]).astype(BF16), w1_ref[which, 0],
                            preferred_element_type=F32)
            second = jnp.dot((chunks + pos_ref[which, 1]).astype(BF16), w1_ref[which, 1],
                             preferred_element_type=F32)
            hid = jax.nn.gelu(first + pltpu.roll(second, nbp - 1, 0)).astype(BF16)
            out_ref[0, g] = jnp.dot(hid, w2_ref[which], preferred_element_type=F32).astype(BF16)


def _compress(cv, pos, w1, w2, batch, nbp):
    tokw = CMP_STRIDE * 2 * NSA_KV_WIDTH
    half = CMP_STRIDE * HEAD_DIM
    vmem = 2 * nbp * tokw * 2 + 2 * 4 * half * CMP_HIDDEN * 2 + 8 * nbp * half * 4
    out_sds = [jax.ShapeDtypeStruct((batch, NSA_KV_GROUPS, nbp, HEAD_DIM), BF16)] * 2
    out_spec = [pl.BlockSpec((1, NSA_KV_GROUPS, nbp, HEAD_DIM), lambda b: (b, 0, 0, 0))] * 2
    return pl.pallas_call(
        _compress_kernel,
        grid=(batch,),
        in_specs=[pl.BlockSpec((1, nbp, tokw), lambda b: (b, 0, 0)),
                  pl.BlockSpec((2, 2, 1, half), lambda b: (0, 0, 0, 0)),
                  pl.BlockSpec((2, 2, half, CMP_HIDDEN), lambda b: (0, 0, 0, 0)),
                  pl.BlockSpec((2, CMP_HIDDEN, HEAD_DIM), lambda b: (0, 0, 0))],
        out_specs=out_spec,
        out_shape=out_sds,
        compiler_params=_cparams(("parallel",), vmem),
        name="kv_compress",
    )(cv, pos, w1, w2)


def _nsa_kernel(q_ref, kc_ref, vc_ref, ks_ref, vs_ref, kw_ref, vw_ref, gate_ref, tbl_ref, tblc_ref,
                ovt_ref, expt_ref, o_ref, acc_ref, m_ref):
    i = pl.program_id(1)
    tq, tk, rows = NSA_TQ, NSA_TK, NSA_ROWS
    gw = NSA_GROUP_SIZE * HEAD_DIM
    seq = expt_ref.shape[0]
    nsel = seq // SEL_BLOCK
    t0 = i * tq
    ovt = ovt_ref[...]
    jrow = lax.broadcasted_iota(jnp.int32, (nsel, tq), 0)
    tcol = t0 + lax.broadcasted_iota(jnp.int32, (nsel, tq), 1)
    cur = tcol // SEL_BLOCK
    valid = jrow * SEL_BLOCK <= tcol
    forced = (jrow == 0) | (jrow == cur) | (jrow == cur - 1)
    slab_row = lax.broadcasted_iota(jnp.int32, (8, tq), 0)

    def compressed_and_selection(g):
        qb = q_ref[0, :, g * gw:(g + 1) * gw]
        q4 = jnp.concatenate([qb[:, r * HEAD_DIM:(r + 1) * HEAD_DIM] for r in range(NSA_GROUP_SIZE)], axis=0)
        s_c = lax.dot_general(q4, kc_ref[0, g], _NT, preferred_element_type=F32) + tblc_ref[g, 0]
        reps = s_c.shape[1] // V7X_LANES
        m_c = jnp.broadcast_to(jnp.max(s_c, axis=-1, keepdims=True), (rows, V7X_LANES))
        e_c = jnp.exp2(s_c - jnp.concatenate([m_c] * reps, axis=1))
        l_c = jnp.broadcast_to(jnp.sum(e_c, axis=-1, keepdims=True), (rows, V7X_LANES))
        inv = jnp.where(m_c > 0.5 * MASK_VALUE, 1.0 / l_c, 0.0)
        p_c = e_c * jnp.concatenate([inv] * reps, axis=1)
        o_c = jnp.dot(p_c.astype(BF16), vc_ref[0, g], preferred_element_type=F32)

        p_sum = p_c[0:tq]
        for r in range(1, NSA_GROUP_SIZE):
            p_sum = p_sum + p_c[r * tq:(r + 1) * tq]
        p_hi = p_sum.astype(BF16)
        p_lo = (p_sum - p_hi.astype(F32)).astype(BF16)
        imp = (lax.dot_general(ovt, p_hi, _NT, preferred_element_type=F32)
               + lax.dot_general(ovt, p_lo, _NT, preferred_element_type=F32))[:nsel]
        score = jnp.where(valid, jnp.where(forced, SEL_FORCE, imp), -SEL_FORCE)
        score3 = score.reshape(nsel // 8, 8, tq)
        slabs = [score3[n] for n in range(nsel // 8)]
        ranks = [jnp.zeros((8, tq), F32) for _ in slabs]
        for other in range(nsel):
            srow = slabs[other // 8][other % 8:other % 8 + 1, :]
            for n, slab in enumerate(slabs):
                if 8 * n > other:
                    before = srow >= slab
                elif 8 * n + 7 <= other:
                    before = srow > slab
                else:
                    before = (srow > slab) | ((srow >= slab) & (slab_row > other - 8 * n))
                ranks[n] = ranks[n] + jnp.where(before, 1.0, 0.0)
        rank = jnp.stack(ranks, axis=0).reshape(nsel, tq)
        drop = jnp.where(rank < float(min(SEL_TOP_N, nsel)), 0.0, SEL_MASK)
        if nsel < V7X_LANES:
            drop = jnp.concatenate([drop, jnp.zeros((V7X_LANES - nsel, tq), F32)], axis=0)
        drop_q = drop.T.astype(BF16)
        q4_sel = jnp.concatenate([q4, jnp.concatenate([drop_q] * NSA_GROUP_SIZE, axis=0)], axis=1)
        return q4, q4_sel, o_c

    per_group = [compressed_and_selection(g) for g in range(NSA_KV_GROUPS)]

    selected = [(g, 0) for g in range(NSA_KV_GROUPS)]
    both = selected + [(g, 1) for g in range(NSA_KV_GROUPS)]
    ones = jnp.ones((tk, HEAD_DIM), BF16)

    def flash(jt, chains, first):
        start = pl.multiple_of(jt * tk, tk)
        d0 = i - jt
        for g, branch in chains:
            q4, q4_sel, _ = per_group[g]
            lanes = slice(g * HEAD_DIM, (g + 1) * HEAD_DIM)
            if branch == 1:
                bias = tbl_ref[g, 0 if first else jnp.where(d0 == WIN_LAST_D0, BIAS_WIN_IDX, d0)]
                s = lax.dot_general(q4, kw_ref[0, pl.ds(start, tk), lanes], _NT, preferred_element_type=F32)
                v = vw_ref[0, pl.ds(start, tk), lanes]
            else:
                bias = tbl_ref[g, 0 if first else jnp.minimum(d0, BIAS_CONST_IDX)]
                k_sel = jnp.concatenate([ks_ref[0, pl.ds(start, tk), lanes], expt_ref[pl.ds(start, tk), :]], axis=1)
                s = lax.dot_general(q4_sel, k_sel, _NT, preferred_element_type=F32)
                v = vs_ref[0, pl.ds(start, tk), lanes]
            s = s + bias
            v_aug = jnp.concatenate([v, ones], axis=1)
            m_tile = jnp.broadcast_to(jnp.max(s, axis=-1, keepdims=True), (rows, V7X_LANES))
            if first:
                m_new = m_tile
                p = jnp.exp2(s - jnp.concatenate([m_new] * (tk // V7X_LANES), axis=1))
                acc_ref[g, branch] = jnp.dot(p.astype(BF16), v_aug, preferred_element_type=F32)
            else:
                m_old = m_ref[g, branch]
                m_new = jnp.maximum(m_old, m_tile)
                p = jnp.exp2(s - jnp.concatenate([m_new] * (tk // V7X_LANES), axis=1))
                alpha = jnp.exp2(m_old - m_new)
                acc_ref[g, branch] = (jnp.concatenate([alpha] * 2, axis=1) * acc_ref[g, branch]
                                      + jnp.dot(p.astype(BF16), v_aug, preferred_element_type=F32))
            m_ref[g, branch] = m_new

    flash(i, both, True)

    remaining = jnp.maximum(i - 2, 0)
    trips = remaining >> NSA_TRIP_SHIFT
    rest = remaining & (NSA_TILES_PER_TRIP - 1)

    for n_left in range(NSA_TILES_PER_TRIP):
        @pl.when((i >= 2) & (rest == n_left))
        def _(n_left=n_left):
            flash(i - 1, both, False)
            flash(i - 2, both, False)
            for jt in reversed(range(n_left)):
                flash(jt, selected, False)

    @pl.when(i == 1)
    def _():
        flash(0, both, False)

    def trip_body(p, carry):
        for u in range(NSA_TILES_PER_TRIP):
            flash(i - 3 - NSA_TILES_PER_TRIP * p - u, selected, False)
        return carry
    lax.fori_loop(0, trips, trip_body, 0)

    gate = 1.0 / (1.0 + jnp.exp(-gate_ref[0]))
    for g in range(NSA_KV_GROUPS):
        o_c = per_group[g][2]
        for r in range(NSA_GROUP_SIZE):
            rs = slice(r * tq, (r + 1) * tq)
            o_s = acc_ref[g, 0, rs, 0:HEAD_DIM] / acc_ref[g, 0, rs, HEAD_DIM:2 * HEAD_DIM]
            o_w = acc_ref[g, 1, rs, 0:HEAD_DIM] / acc_ref[g, 1, rs, HEAD_DIM:2 * HEAD_DIM]
            c = (g * NSA_GROUP_SIZE + r) * N_GATES
            out = gate[:, c:c + 1] * o_c[rs] + gate[:, c + 1:c + 2] * o_s + gate[:, c + 2:c + 3] * o_w
            col = (g * NSA_GROUP_SIZE + r) * HEAD_DIM
            o_ref[0, :, col:col + HEAD_DIM] = out


def _nsa_attention(proj_a, proj_b, kcmp, vcmp, tbl, tblc, ovt, expt, batch, seq, cols):
    tq, tk, rows = NSA_TQ, NSA_TK, NSA_ROWS
    nbp = kcmp.shape[2]
    n_tiles = tbl.shape[1]
    groups = NSA_KV_GROUPS
    qblk = cols["n_q"] // NSA_WIDTH
    gblk = cols["gates"] // V7X_LANES
    assert cols["n_q"] % NSA_WIDTH == 0 and cols["gates"] % V7X_LANES == 0 and N_GATES * NSA_HEADS <= V7X_LANES

    def resident(shape, index_map):
        return pl.BlockSpec(shape, index_map, pipeline_mode=pl.Buffered(1))

    def kv_spec(name):
        assert cols[name] % NSA_KV_WIDTH == 0
        blk = cols[name] // NSA_KV_WIDTH
        return resident((1, seq, NSA_KV_WIDTH), lambda b, i: (b, 0, blk))

    cmp_spec = resident((1, groups, nbp, HEAD_DIM), lambda b, i: (b, 0, 0, 0))
    vmem = (4 * seq * NSA_KV_WIDTH * 2 + groups * n_tiles * rows * tk * 4 + 2 * groups * rows * nbp * 4
            + seq * V7X_LANES * 2 + 2 * groups * rows * (2 * HEAD_DIM + V7X_LANES) * 4
            + 4 * tq * (NSA_WIDTH + groups * V7X_LANES) * 4
            + 12 * rows * tk * 4)
    return pl.pallas_call(
        _nsa_kernel,
        grid=(batch, seq // tq),
        in_specs=[pl.BlockSpec((1, tq, NSA_WIDTH), lambda b, i: (b, i, qblk)),
                  cmp_spec, cmp_spec,
                  kv_spec("ks"), kv_spec("vs"), kv_spec("kw"), kv_spec("vw"),
                  pl.BlockSpec((1, tq, V7X_LANES), lambda b, i: (b, i, gblk)),
                  resident((groups, n_tiles, rows, tk), lambda b, i: (0, 0, 0, 0)),
                  pl.BlockSpec((groups, 1, rows, nbp), lambda b, i: (0, i, 0, 0)),
                  resident((V7X_LANES, nbp), lambda b, i: (0, 0)),
                  resident((seq, V7X_LANES), lambda b, i: (0, 0))],
        out_specs=pl.BlockSpec((1, tq, NSA_WIDTH), lambda b, i: (b, i, 0)),
        out_shape=jax.ShapeDtypeStruct((batch, seq, NSA_WIDTH), F32),
        scratch_shapes=[pltpu.VMEM((groups, 2, rows, 2 * HEAD_DIM), F32),
                        pltpu.VMEM((groups, 2, rows, V7X_LANES), F32)],
        compiler_params=_cparams(("parallel", "arbitrary"), vmem),
        name="nsa_attention",
    )(proj_a, kcmp, vcmp, proj_a, proj_a, proj_a, proj_a, proj_b, tbl, tblc, ovt, expt)


def _selection_constants(seq, nbp):
    nb = seq // CMP_STRIDE - 1
    nsel = seq // SEL_BLOCK
    assert nsel <= V7X_LANES and nbp >= nb
    ci = np.arange(nbp)[None, :] * CMP_STRIDE
    sj = np.arange(V7X_LANES)[:, None] * SEL_BLOCK
    ovt = ((ci < sj + SEL_BLOCK) & (ci + CMP_BLOCK > sj) & (np.arange(nbp)[None, :] < nb)
           & (np.arange(V7X_LANES)[:, None] < nsel))
    expt = (np.arange(seq)[:, None] // SEL_BLOCK) == np.arange(V7X_LANES)[None, :]
    return jnp.asarray(ovt, BF16), jnp.asarray(expt, BF16)


def _nsa_branch(proj_a3, proj_b3, proj_cv3, cmp_pos, cmp_w1, cmp_w2, rel_bias, cols):
    batch, seq, _ = proj_a3.shape
    nbp = seq // CMP_STRIDE
    half = CMP_STRIDE * HEAD_DIM
    pos = jnp.stack(cmp_pos).reshape(2, 2, 1, half)
    w1 = jnp.stack(cmp_w1).reshape(2, 2, half, CMP_HIDDEN).astype(BF16)
    w2 = jnp.stack(cmp_w2).astype(BF16)
    kcmp, vcmp = _compress(proj_cv3, pos, w1, w2, batch, nbp)

    tbl = _bias_table(rel_bias, BIAS_TILES, NSA_TK)
    tblc = _cmp_bias_table(rel_bias, seq // NSA_TQ, nbp)
    ovt, expt = _selection_constants(seq, nbp)
    return _nsa_attention(proj_a3, proj_b3, kcmp, vcmp, tbl, tblc, ovt, expt, batch, seq, cols)


def _out_kernel(osb_ref, zsb_ref, onsa_ref, znsa_ref, x_ref, gsb_ref, gnsa_ref, gfin_ref, wsb_ref, wnsa_ref,
                o_ref):
    def gated(o_r, z_r, g_r, rows):
        o = o_r[rows, :]
        y = o * lax.rsqrt(jnp.mean(o * o, axis=-1, keepdims=True) + RMS_EPS) * g_r[...]
        z = z_r[rows, :]
        return (y * (z * (1.0 / (1.0 + jnp.exp(-z))))).astype(BF16)

    for c in range(x_ref.shape[0] // OUT_CHUNK):
        rows = slice(c * OUT_CHUNK, (c + 1) * OUT_CHUNK)
        mixed = (jnp.dot(gated(osb_ref, zsb_ref, gsb_ref, rows), wsb_ref[...], preferred_element_type=F32)
                 + jnp.dot(gated(onsa_ref, znsa_ref, gnsa_ref, rows), wnsa_ref[...], preferred_element_type=F32))
        h = x_ref[rows, :] + mixed
        o_ref[rows, :] = h * lax.rsqrt(jnp.mean(h * h, axis=-1, keepdims=True) + RMS_EPS) * gfin_ref[...]


def _output_stage(o_sb, o_nsa, proj_b, x2d, g_sb, g_nsa, g_fin, w_out, tm):
    m, d = x2d.shape
    half = SB_WIDTH
    assert m % tm == 0 and tm % OUT_CHUNK == 0
    vmem = 2 * (4 * tm * half * 4 + 2 * tm * d * 4) + 2 * half * d * 2 + 6 * OUT_CHUNK * d * 4
    row = lambda c: pl.BlockSpec((tm, half), lambda i: (i, c))
    vec = lambda n: pl.BlockSpec((1, n), lambda i: (0, 0))
    weights = pl.Buffered(1)
    return pl.pallas_call(
        _out_kernel,
        grid=(m // tm,),
        in_specs=[row(0), row(0), row(0), row(1),
                  pl.BlockSpec((tm, d), lambda i: (i, 0)),
                  vec(half), vec(half), vec(d),
                  pl.BlockSpec((half, d), lambda i: (0, 0), pipeline_mode=weights),
                  pl.BlockSpec((half, d), lambda i: (1, 0), pipeline_mode=weights)],
        out_specs=pl.BlockSpec((tm, d), lambda i: (i, 0)),
        out_shape=jax.ShapeDtypeStruct((m, d), F32),
        compiler_params=_cparams(("parallel",), vmem),
        name="output_stage",
    )(o_sb, proj_b, o_nsa, proj_b, x2d, g_sb, g_nsa, g_fin, w_out, w_out)


def kernel(x, norm_in, w_in, cmp_k_pos, cmp_k_w1, cmp_k_w2, cmp_v_pos, cmp_v_w1, cmp_v_w2,
           rel_bias, norm_sb, norm_nsa, w_out, norm_final):
    batch, seq, d_model = x.shape
    assert w_in.shape[0] == 1, "single-layer trunk: the final norm is fused into the output stage"
    m = batch * seq
    nbp = seq // CMP_STRIDE
    assert seq % NSA_TK == 0 and nbp % V7X_LANES == 0

    sizes = (SB_WIDTH,) * 4 + (NSA_WIDTH,) + (NSA_KV_WIDTH,) * 6 + (N_GATES * NSA_HEADS, NSA_WIDTH)
    off = np.concatenate([[0], np.cumsum(sizes)])
    (c_sbq, c_sbk, c_sbv, c_sbz, c_nq, c_kc, c_vc, c_ks, c_vs, c_kw, c_vw, c_gate, c_nz, c_end) = [int(o) for o in off]
    cols = {"n_q": 3 * SB_WIDTH, "ks": 3 * SB_WIDTH + NSA_WIDTH}
    cols["vs"] = cols["ks"] + NSA_KV_WIDTH
    cols["kw"] = cols["vs"] + NSA_KV_WIDTH
    cols["vw"] = cols["kw"] + NSA_KV_WIDTH
    cols["gates"] = 2 * SB_WIDTH

    w = w_in[0]
    w_a = jnp.concatenate([w[:, c_sbq:c_sbz], w[:, c_nq:c_kc], w[:, c_ks:c_gate]], axis=1).astype(BF16)
    w_cv = w[:, c_kc:c_ks].astype(BF16)
    gate_pad = jnp.zeros((d_model, V7X_LANES - N_GATES * NSA_HEADS), w.dtype)
    w_b = jnp.concatenate([w[:, c_sbz:c_nq], w[:, c_nz:c_end], w[:, c_gate:c_nz], gate_pad], axis=1).astype(BF16)
    g_in = norm_in[0].reshape(1, d_model)
    x2d = x.reshape(m, d_model)

    q_scale = np.ones((1, w_a.shape[1]), np.float32)
    q_scale[:, c_sbq:c_sbq + SB_WIDTH] = SCALE * LOG2E
    q_scale[:, cols["n_q"]:cols["n_q"] + NSA_WIDTH] = SCALE * LOG2E
    proj_a, xn = _rms_proj(x2d, g_in, w_a, jnp.asarray(q_scale), BF16, PROJ_TM, w_a.shape[1] // PROJ_COL_TILES,
                           "proj_attn")
    proj_cv = _proj_chunk_major(xn, w_cv, PROJ_TM, "proj_cmp")
    proj_b = _plain_proj(xn, w_b, F32, PROJ_TM // 2, "proj_gate")

    proj_a3 = proj_a.reshape(batch, seq, -1)
    o_sb = _sb_attention(proj_a3, batch, seq)
    o_nsa = _nsa_branch(proj_a3, proj_b.reshape(batch, seq, -1), proj_cv.reshape(batch, nbp, -1),
                        (cmp_k_pos[0], cmp_v_pos[0]), (cmp_k_w1[0], cmp_v_w1[0]), (cmp_k_w2[0], cmp_v_w2[0]),
                        rel_bias, cols)

    out = _output_stage(o_sb.reshape(m, SB_WIDTH), o_nsa.reshape(m, NSA_WIDTH), proj_b, x2d,
                        norm_sb[0].reshape(1, SB_WIDTH), norm_nsa[0].reshape(1, NSA_WIDTH),
                        norm_final.reshape(1, d_model), w_out[0].astype(BF16), OUT_TM)
    return out.reshape(batch, seq, d_model)
```

```python
import functools
import math

import numpy as np
import jax
import jax.numpy as jnp
from jax import lax
from jax.experimental import pallas as pl
from jax.experimental.pallas import tpu as pltpu

F32 = jnp.float32
BF16 = jnp.bfloat16

HEAD_DIM = 128
SB_HEADS = 8
NSA_HEADS = 8
NSA_KV_GROUPS = 2
NSA_GROUP_SIZE = NSA_HEADS // NSA_KV_GROUPS
SB_WIDTH = SB_HEADS * HEAD_DIM
NSA_WIDTH = NSA_HEADS * HEAD_DIM
NSA_KV_WIDTH = NSA_KV_GROUPS * HEAD_DIM
N_GATES = 3
CMP_STRIDE = 16
CMP_BLOCK = 2 * CMP_STRIDE
CMP_HIDDEN = 256
SEL_BLOCK = 64
SEL_TOP_N = 16
WINDOW = 512
REL_BUCKETS = 32
REL_MAX_EXACT = 16
REL_MAX_DISTANCE = 1024
RMS_EPS = 1e-6
SEL_FORCE = 1e9
MASK_VALUE = -1e30
SEL_MASK = -float(2 ** 100)
SCALE = 1.0 / math.sqrt(HEAD_DIM)
LOG2E = math.log2(math.e)

V7X_LANES = 128
V7X_MXU_DIM = 256
V7X_BF16_SUBLANES = 16
V7X_VMEM_BYTES = 64 * 1024 * 1024
VMEM_SPILL_BYTES = 8 * 1024 * 1024

SB_TQ = V7X_MXU_DIM
SB_TK = V7X_MXU_DIM
SB_HEADS_PER_STEP = 8
SB_DEAD_LOG2 = 160.0
NSA_TQ = V7X_MXU_DIM
NSA_TK = V7X_MXU_DIM
NSA_ROWS = NSA_GROUP_SIZE * NSA_TQ
NSA_TRIP_SHIFT = 2
NSA_TILES_PER_TRIP = 1 << NSA_TRIP_SHIFT
PROJ_TM = 4 * V7X_MXU_DIM
PROJ_COL_TILES = 4
OUT_TM = 2 * V7X_MXU_DIM
OUT_CHUNK = V7X_MXU_DIM

_NT = (((1,), (1,)), ((), ()))


def _bucket_thresholds():
    n = np.arange(0, 4 * REL_MAX_DISTANCE)
    nf = np.maximum(n, 1).astype(np.float64)
    large = REL_MAX_EXACT + (np.log(nf / REL_MAX_EXACT) / math.log(REL_MAX_DISTANCE / REL_MAX_EXACT)
                             * (REL_BUCKETS - REL_MAX_EXACT)).astype(np.int64)
    bucket = np.where(n < REL_MAX_EXACT, n, np.minimum(large, REL_BUCKETS - 1))
    assert np.all(np.diff(bucket) >= 0) and bucket[-1] == REL_BUCKETS - 1
    return [int(n[bucket >= k][0]) for k in range(1, REL_BUCKETS)]


BUCKET_THR = _bucket_thresholds()
LAST_THR = BUCKET_THR[-1]
BIAS_CONST_IDX = -(-(LAST_THR + NSA_TK - 1) // NSA_TQ)
BIAS_FAR = 1 << 20
WIN_LAST_D0 = 2
BIAS_WIN_IDX = BIAS_CONST_IDX + 1
BIAS_TILES = BIAS_CONST_IDX + 2
assert NSA_TQ == NSA_TK and NSA_TQ + NSA_TK - 1 < WINDOW <= WIN_LAST_D0 * NSA_TQ < BIAS_CONST_IDX * NSA_TQ


def _cparams(sem, block_bytes):
    limit = min(block_bytes + VMEM_SPILL_BYTES, V7X_VMEM_BYTES)
    return pltpu.CompilerParams(dimension_semantics=sem, vmem_limit_bytes=int(limit))


def _bias_table_kernel(rb_ref, o_ref):
    h = pl.program_id(0)
    tile = pl.program_id(1)
    rows, cols = o_ref.shape[2], o_ref.shape[3]
    base = jnp.where(tile == BIAS_CONST_IDX, BIAS_FAR,
                     jnp.where(tile == BIAS_WIN_IDX, WIN_LAST_D0 * NSA_TQ, tile * NSA_TQ))
    ti = lax.broadcasted_iota(jnp.int32, (rows, cols), 0)
    cj = lax.broadcasted_iota(jnp.int32, (rows, cols), 1)
    dist = base + ti - cj
    val = jnp.full((rows, cols), rb_ref[0, h], F32)
    for k in range(1, REL_BUCKETS):
        val = jnp.where(dist >= BUCKET_THR[k - 1], rb_ref[k, h], val)
    keep = ((dist >= 0) & (dist < WINDOW)) | ((tile != 0) & (tile != BIAS_WIN_IDX))
    o_ref[0, 0] = val * LOG2E + jnp.where(keep, 0.0, MASK_VALUE)


def _bias_lookup(rb_ref, h, dist):
    val = jnp.full(dist.shape, rb_ref[0, h], F32)
    for k in range(1, REL_BUCKETS):
        val = jnp.where(dist >= BUCKET_THR[k - 1], rb_ref[k, h], val)
    return val * LOG2E + jnp.where(dist >= 0, 0.0, MASK_VALUE)


def _cmp_bias_table_kernel(rb_ref, o_ref):
    h = pl.program_id(0)
    n_tiles, tq, nbp = o_ref.shape[1], o_ref.shape[2], o_ref.shape[3]
    step = tq // CMP_STRIDE
    assert n_tiles * step <= nbp
    ti = lax.broadcasted_iota(jnp.int32, (tq, 2 * nbp), 0)
    cj = lax.broadcasted_iota(jnp.int32, (tq, 2 * nbp), 1)
    master = _bias_lookup(rb_ref, h, ti - CMP_STRIDE * (cj - nbp) - (CMP_BLOCK - 1))
    for i in range(n_tiles):
        o_ref[0, i] = pltpu.roll(master, nbp + step * i, 1)[:, :nbp]


def _cmp_bias_table(rel_bias, n_tiles, nbp):
    block = (1, n_tiles, NSA_TQ, nbp)
    return pl.pallas_call(
        _cmp_bias_table_kernel,
        grid=(NSA_HEADS,),
        in_specs=[pl.BlockSpec(memory_space=pltpu.SMEM)],
        out_specs=pl.BlockSpec(block, lambda h: (h // NSA_GROUP_SIZE, 0, h % NSA_GROUP_SIZE, 0)),
        out_shape=jax.ShapeDtypeStruct((NSA_KV_GROUPS, n_tiles, NSA_ROWS, nbp), F32),
        compiler_params=_cparams(("parallel",), 4 * n_tiles * NSA_TQ * nbp * 4),
        name="cmp_bias_table",
    )(rel_bias)


def _bias_table(rel_bias, n_tiles, cols):
    return pl.pallas_call(
        _bias_table_kernel,
        grid=(NSA_HEADS, n_tiles),
        in_specs=[pl.BlockSpec(memory_space=pltpu.SMEM)],
        out_specs=pl.BlockSpec((1, 1, NSA_TQ, cols),
                               lambda h, t: (h // NSA_GROUP_SIZE, t, h % NSA_GROUP_SIZE, 0)),
        out_shape=jax.ShapeDtypeStruct((NSA_KV_GROUPS, n_tiles, NSA_ROWS, cols), F32),
        compiler_params=_cparams(("parallel", "parallel"), 2 * NSA_TQ * cols * 4),
        name="bias_table",
    )(rel_bias)


def _normalise_rows(x_ref, g_ref, xn_ref, chunk):
    def body(c, carry):
        rows = pl.ds(pl.multiple_of(c * chunk, chunk), chunk)
        x = x_ref[rows, :]
        ms = jnp.mean(x * x, axis=-1, keepdims=True)
        xn_ref[rows, :] = (x * lax.rsqrt(ms + RMS_EPS) * g_ref[...]).astype(BF16)
        return carry
    lax.fori_loop(0, x_ref.shape[0] // chunk, body, 0)


def _proj_kernel(x_ref, g_ref, w_ref, cs_ref, o_ref, xn_ref, *, chunk):
    @pl.when(pl.program_id(1) == 0)
    def _():
        _normalise_rows(x_ref, g_ref, xn_ref, chunk)

    acc = jnp.dot(xn_ref[...], w_ref[...], preferred_element_type=F32)
    o_ref[...] = (acc * cs_ref[...]).astype(o_ref.dtype)


def _plain_proj_kernel(xn_ref, w_ref, o_ref):
    o_ref[...] = jnp.dot(xn_ref[...], w_ref[...], preferred_element_type=F32).astype(o_ref.dtype)


def _proj_chunk_major_kernel(xn_ref, w_ref, o_ref, acc_ref):
    acc = jnp.dot(xn_ref[...], w_ref[...], preferred_element_type=F32)
    n = acc.shape[1]
    slabs = n // V7X_LANES
    for s in range(slabs):
        acc_ref[s] = acc[:, s * V7X_LANES:(s + 1) * V7X_LANES]
    n_chunks = o_ref.shape[0]
    for l in range(CMP_STRIDE):
        for s in range(slabs):
            tokens = acc_ref[s, pl.ds(l, n_chunks, stride=CMP_STRIDE), :]
            o_ref[:, l * n + s * V7X_LANES:l * n + (s + 1) * V7X_LANES] = tokens.astype(o_ref.dtype)


def _proj_chunk_major(xn, w, tm, name):
    m, d = xn.shape
    n = w.shape[1]
    assert m % tm == 0 and tm % (CMP_STRIDE * V7X_BF16_SUBLANES) == 0 and n % V7X_LANES == 0
    vmem = 2 * tm * d * 2 + 2 * d * n * 2 + 2 * tm * n * 2 + 2 * tm * n * 4
    return pl.pallas_call(
        _proj_chunk_major_kernel,
        grid=(m // tm,),
        in_specs=[pl.BlockSpec((tm, d), lambda i: (i, 0)),
                  pl.BlockSpec((d, n), lambda i: (0, 0))],
        out_specs=pl.BlockSpec((tm // CMP_STRIDE, CMP_STRIDE * n), lambda i: (i, 0)),
        out_shape=jax.ShapeDtypeStruct((m // CMP_STRIDE, CMP_STRIDE * n), BF16),
        scratch_shapes=[pltpu.VMEM((n // V7X_LANES, tm, V7X_LANES), F32)],
        compiler_params=_cparams(("parallel",), vmem),
        name=name,
    )(xn, w)


def _plain_proj(xn, w, out_dtype, tm, name):
    m, d = xn.shape
    n = w.shape[1]
    assert m % tm == 0
    osz = jnp.dtype(out_dtype).itemsize
    vmem = 2 * tm * d * 2 + 2 * d * n * 2 + 2 * tm * n * osz + tm * n * 4
    return pl.pallas_call(
        _plain_proj_kernel,
        grid=(m // tm,),
        in_specs=[pl.BlockSpec((tm, d), lambda i: (i, 0)),
                  pl.BlockSpec((d, n), lambda i: (0, 0))],
        out_specs=pl.BlockSpec((tm, n), lambda i: (i, 0)),
        out_shape=jax.ShapeDtypeStruct((m, n), out_dtype),
        compiler_params=_cparams(("parallel",), vmem),
        name=name,
    )(xn, w)


def _rms_proj(x2d, g, w, col_scale, out_dtype, tm, tn, name):
    m, d = x2d.shape
    n = w.shape[1]
    assert m % tm == 0 and n % tn == 0 and col_scale.shape == (1, n)
    osz = jnp.dtype(out_dtype).itemsize
    vmem = 2 * tm * d * 4 + 2 * tm * d * 2 + 2 * d * tn * 2 + 2 * tm * tn * osz + tm * tn * 4
    return pl.pallas_call(
        functools.partial(_proj_kernel, chunk=min(tm, 128)),
        grid=(m // tm, n // tn),
        in_specs=[pl.BlockSpec((tm, d), lambda i, j: (i, 0)),
                  pl.BlockSpec((1, d), lambda i, j: (0, 0)),
                  pl.BlockSpec((d, tn), lambda i, j: (0, j)),
                  pl.BlockSpec((1, tn), lambda i, j: (0, j))],
        out_specs=[pl.BlockSpec((tm, tn), lambda i, j: (i, j)),
                   pl.BlockSpec((tm, d), lambda i, j: (i, 0))],
        out_shape=[jax.ShapeDtypeStruct((m, n), out_dtype), jax.ShapeDtypeStruct((m, d), BF16)],
        compiler_params=_cparams(("parallel", "arbitrary"), vmem),
        name=name,
    )(x2d, g, w, col_scale)


def _sb_kernel(q_ref, k_ref, v_ref, o_ref, acc_ref, run_ref):
    i = pl.program_id(2)
    tq, tk = q_ref.shape[1], SB_TK
    n_diag = tq // tk
    row = lax.broadcasted_iota(jnp.int32, (tq, tk), 0)
    col = lax.broadcasted_iota(jnp.int32, (tq, tk), 1)
    below = row > col
    suffix = jnp.where(below[:tk], 1.0, 0.0).astype(BF16)

    def tile(j, row0, diag):
        start = pl.multiple_of(j * tk, tk)
        rows = slice(row0, tq)
        causal = below[:tq - row0]
        n_rows = tq - row0
        log_betas, rests, runs = [], [], []
        for hh in range(SB_HEADS_PER_STEP):
            lanes = slice(hh * HEAD_DIM, (hh + 1) * HEAD_DIM)
            z = lax.dot_general(q_ref[0, rows, lanes], k_ref[0, pl.ds(start, tk), lanes], _NT,
                                preferred_element_type=F32)
            sp = jnp.log2(1.0 + jnp.exp2(-jnp.abs(z)))
            neg_rest = jnp.maximum(z, 0.0) + sp
            log_beta = z - neg_rest
            if diag:
                neg_rest = jnp.where(causal, neg_rest, 0.0)
            run = run_ref[hh, rows]
            log_betas.append(log_beta - run)
            rests.append(neg_rest.astype(BF16))
            run_ref[hh, rows] = run + jnp.sum(neg_rest, axis=-1, keepdims=True)
        later_all = jnp.dot(jnp.concatenate(rests, axis=0), suffix, preferred_element_type=F32)
        for hh in range(SB_HEADS_PER_STEP):
            lanes = slice(hh * HEAD_DIM, (hh + 1) * HEAD_DIM)
            a = jnp.exp2(log_betas[hh] - later_all[hh * n_rows:(hh + 1) * n_rows])
            if diag:
                a = jnp.where(causal, a, 0.0)
            acc_ref[hh, rows] += jnp.dot(a.astype(BF16), v_ref[0, pl.ds(start, tk), lanes],
                                         preferred_element_type=F32)

    acc_ref[...] = jnp.zeros_like(acc_ref)
    run_ref[...] = jnp.zeros_like(run_ref)
    def diagonal_tiles():
        for c in reversed(range(n_diag)):
            tile(i * n_diag + c, c * tk, True)

    @pl.when(i == 0)
    def _():
        diagonal_tiles()

    @pl.when(i > 0)
    def _():
        diagonal_tiles()
        tile(i * n_diag - 1, 0, False)

    def walk(carry):
        jj, _ = carry
        tile(i * n_diag - 1 - jj, 0, False)
        return jj + 1, jnp.min(run_ref[...])

    def alive(carry):
        jj, least_run = carry
        return (jj < i * n_diag) & (least_run < SB_DEAD_LOG2)
    lax.while_loop(alive, walk, (jnp.int32(1), jnp.min(run_ref[...])))
    for hh in range(SB_HEADS_PER_STEP):
        o_ref[0, :, hh * HEAD_DIM:(hh + 1) * HEAD_DIM] = acc_ref[hh]


def _sb_attention(proj_a, batch, seq):
    tq = min(SB_TQ, seq)
    hps = SB_HEADS_PER_STEP
    width = hps * HEAD_DIM
    assert seq % tq == 0 and tq % SB_TK == 0 and SB_HEADS % hps == 0
    kblk = SB_WIDTH // width
    vmem = 2 * seq * width * 2 + 6 * tq * width * 4 + hps * 12 * tq * SB_TK * 4
    whole_seq = pl.Buffered(1)
    return pl.pallas_call(
        _sb_kernel,
        grid=(batch, SB_HEADS // hps, seq // tq),
        in_specs=[pl.BlockSpec((1, tq, width), lambda b, h, i: (b, i, h)),
                  pl.BlockSpec((1, seq, width), lambda b, h, i: (b, 0, kblk + h), pipeline_mode=whole_seq),
                  pl.BlockSpec((1, seq, width), lambda b, h, i: (b, 0, 2 * kblk + h), pipeline_mode=whole_seq)],
        out_specs=pl.BlockSpec((1, tq, width), lambda b, h, i: (b, i, h)),
        out_shape=jax.ShapeDtypeStruct((batch, seq, SB_WIDTH), F32),
        scratch_shapes=[pltpu.VMEM((hps, tq, HEAD_DIM), F32), pltpu.VMEM((hps, tq, 1), F32)],
        compiler_params=_cparams(("parallel", "parallel", "arbitrary"), vmem),
        name="sb_attention",
    )(proj_a, proj_a, proj_a)


def _compress_kernel(cv_ref, pos_ref, w1_ref, w2_ref, ko_ref, vo_ref):
    nbp = cv_ref.shape[1]
    tok = 2 * NSA_KV_WIDTH
    for which, out_ref in ((0, ko_ref), (1, vo_ref)):
        for g in range(NSA_KV_GROUPS):
            c0 = (which * NSA_KV_GROUPS + g) * HEAD_DIM
            chunks = jnp.concatenate(
                [cv_ref[0, :, l * tok + c0:l * tok + c0 + HEAD_DIM] for l in range(CMP_STRIDE)],
                axis=1).astype(F32)
            first = jnp.dot((chunks + pos_ref[which, 0]).astype(BF16), w1_ref[which, 0],
                            preferred_element_type=F32)
            second = jnp.dot((chunks + pos_ref[which, 1]).astype(BF16), w1_ref[which, 1],
                             preferred_element_type=F32)
            hid = jax.nn.gelu(first + pltpu.roll(second, nbp - 1, 0)).astype(BF16)
            out_ref[0, g] = jnp.dot(hid, w2_ref[which], preferred_element_type=F32).astype(BF16)


def _compress(cv, pos, w1, w2, batch, nbp):
    tokw = CMP_STRIDE * 2 * NSA_KV_WIDTH
    half = CMP_STRIDE * HEAD_DIM
    vmem = 2 * nbp * tokw * 2 + 2 * 4 * half * CMP_HIDDEN * 2 + 8 * nbp * half * 4
    out_sds = [jax.ShapeDtypeStruct((batch, NSA_KV_GROUPS, nbp, HEAD_DIM), BF16)] * 2
    out_spec = [pl.BlockSpec((1, NSA_KV_GROUPS, nbp, HEAD_DIM), lambda b: (b, 0, 0, 0))] * 2
    return pl.pallas_call(
        _compress_kernel,
        grid=(batch,),
        in_specs=[pl.BlockSpec((1, nbp, tokw), lambda b: (b, 0, 0)),
                  pl.BlockSpec((2, 2, 1, half), lambda b: (0, 0, 0, 0)),
                  pl.BlockSpec((2, 2, half, CMP_HIDDEN), lambda b: (0, 0, 0, 0)),
                  pl.BlockSpec((2, CMP_HIDDEN, HEAD_DIM), lambda b: (0, 0, 0))],
        out_specs=out_spec,
        out_shape=out_sds,
        compiler_params=_cparams(("parallel",), vmem),
        name="kv_compress",
    )(cv, pos, w1, w2)


def _nsa_kernel(q_ref, kc_ref, vc_ref, ks_ref, vs_ref, kw_ref, vw_ref, gate_ref, tbl_ref, tblc_ref,
                ovt_ref, expt_ref, o_ref, acc_ref, m_ref):
    i = pl.program_id(1)
    tq, tk, rows = NSA_TQ, NSA_TK, NSA_ROWS
    gw = NSA_GROUP_SIZE * HEAD_DIM
    seq = expt_ref.shape[0]
    nsel = seq // SEL_BLOCK
    t0 = i * tq
    ovt = ovt_ref[...]
    jrow = lax.broadcasted_iota(jnp.int32, (nsel, tq), 0)
    tcol = t0 + lax.broadcasted_iota(jnp.int32, (nsel, tq), 1)
    cur = tcol // SEL_BLOCK
    valid = jrow * SEL_BLOCK <= tcol
    forced = (jrow == 0) | (jrow == cur) | (jrow == cur - 1)
    slab_row = lax.broadcasted_iota(jnp.int32, (8, tq), 0)

    def compressed_and_selection(g):
        qb = q_ref[0, :, g * gw:(g + 1) * gw]
        q4 = jnp.concatenate([qb[:, r * HEAD_DIM:(r + 1) * HEAD_DIM] for r in range(NSA_GROUP_SIZE)], axis=0)
        s_c = lax.dot_general(q4, kc_ref[0, g], _NT, preferred_element_type=F32) + tblc_ref[g, 0]
        reps = s_c.shape[1] // V7X_LANES
        m_c = jnp.broadcast_to(jnp.max(s_c, axis=-1, keepdims=True), (rows, V7X_LANES))
        e_c = jnp.exp2(s_c - jnp.concatenate([m_c] * reps, axis=1))
        l_c = jnp.broadcast_to(jnp.sum(e_c, axis=-1, keepdims=True), (rows, V7X_LANES))
        inv = jnp.where(m_c > 0.5 * MASK_VALUE, 1.0 / l_c, 0.0)
        p_c = e_c * jnp.concatenate([inv] * reps, axis=1)
        o_c = jnp.dot(p_c.astype(BF16), vc_ref[0, g], preferred_element_type=F32)

        p_sum = p_c[0:tq]
        for r in range(1, NSA_GROUP_SIZE):
            p_sum = p_sum + p_c[r * tq:(r + 1) * tq]
        p_hi = p_sum.astype(BF16)
        p_lo = (p_sum - p_hi.astype(F32)).astype(BF16)
        imp = (lax.dot_general(ovt, p_hi, _NT, preferred_element_type=F32)
               + lax.dot_general(ovt, p_lo, _NT, preferred_element_type=F32))[:nsel]
        score = jnp.where(valid, jnp.where(forced, SEL_FORCE, imp), -SEL_FORCE)
        score3 = score.reshape(nsel // 8, 8, tq)
        slabs = [score3[n] for n in range(nsel // 8)]
        ranks = [jnp.zeros((8, tq), F32) for _ in slabs]
        for other in range(nsel):
            srow = slabs[other // 8][other % 8:other % 8 + 1, :]
            for n, slab in enumerate(slabs):
                if 8 * n > other:
                    before = srow >= slab
                elif 8 * n + 7 <= other:
                    before = srow > slab
                else:
                    before = (srow > slab) | ((srow >= slab) & (slab_row > other - 8 * n))
                ranks[n] = ranks[n] + jnp.where(before, 1.0, 0.0)
        rank = jnp.stack(ranks, axis=0).reshape(nsel, tq)
        drop = jnp.where(rank < float(min(SEL_TOP_N, nsel)), 0.0, SEL_MASK)
        if nsel < V7X_LANES:
            drop = jnp.concatenate([drop, jnp.zeros((V7X_LANES - nsel, tq), F32)], axis=0)
        drop_q = drop.T.astype(BF16)
        q4_sel = jnp.concatenate([q4, jnp.concatenate([drop_q] * NSA_GROUP_SIZE, axis=0)], axis=1)
        return q4, q4_sel, o_c

    per_group = [compressed_and_selection(g) for g in range(NSA_KV_GROUPS)]

    selected = [(g, 0) for g in range(NSA_KV_GROUPS)]
    both = selected + [(g, 1) for g in range(NSA_KV_GROUPS)]
    ones = jnp.ones((tk, HEAD_DIM), BF16)

    def flash(jt, chains, first):
        start = pl.multiple_of(jt * tk, tk)
        d0 = i - jt
        for g, branch in chains:
            q4, q4_sel, _ = per_group[g]
            lanes = slice(g * HEAD_DIM, (g + 1) * HEAD_DIM)
            if branch == 1:
                bias = tbl_ref[g, 0 if first else jnp.where(d0 == WIN_LAST_D0, BIAS_WIN_IDX, d0)]
                s = lax.dot_general(q4, kw_ref[0, pl.ds(start, tk), lanes], _NT, preferred_element_type=F32)
                v = vw_ref[0, pl.ds(start, tk), lanes]
            else:
                bias = tbl_ref[g, 0 if first else jnp.minimum(d0, BIAS_CONST_IDX)]
                k_sel = jnp.concatenate([ks_ref[0, pl.ds(start, tk), lanes], expt_ref[pl.ds(start, tk), :]], axis=1)
                s = lax.dot_general(q4_sel, k_sel, _NT, preferred_element_type=F32)
                v = vs_ref[0, pl.ds(start, tk), lanes]
            s = s + bias
            v_aug = jnp.concatenate([v, ones], axis=1)
            m_tile = jnp.broadcast_to(jnp.max(s, axis=-1, keepdims=True), (rows, V7X_LANES))
            if first:
                m_new = m_tile
                p = jnp.exp2(s - jnp.concatenate([m_new] * (tk // V7X_LANES), axis=1))
                acc_ref[g, branch] = jnp.dot(p.astype(BF16), v_aug, preferred_element_type=F32)
            else:
                m_old = m_ref[g, branch]
                m_new = jnp.maximum(m_old, m_tile)
                p = jnp.exp2(s - jnp.concatenate([m_new] * (tk // V7X_LANES), axis=1))
                alpha = jnp.exp2(m_old - m_new)
                acc_ref[g, branch] = (jnp.concatenate([alpha] * 2, axis=1) * acc_ref[g, branch]
                                      + jnp.dot(p.astype(BF16), v_aug, preferred_element_type=F32))
            m_ref[g, branch] = m_new

    flash(i, both, True)

    remaining = jnp.maximum(i - 2, 0)
    trips = remaining >> NSA_TRIP_SHIFT
    rest = remaining & (NSA_TILES_PER_TRIP - 1)

    for n_left in range(NSA_TILES_PER_TRIP):
        @pl.when((i >= 2) & (rest == n_left))
        def _(n_left=n_left):
            flash(i - 1, both, False)
            flash(i - 2, both, False)
            for jt in reversed(range(n_left)):
                flash(jt, selected, False)

    @pl.when(i == 1)
    def _():
        flash(0, both, False)

    def trip_body(p, carry):
        for u in range(NSA_TILES_PER_TRIP):
            flash(i - 3 - NSA_TILES_PER_TRIP * p - u, selected, False)
        return carry
    lax.fori_loop(0, trips, trip_body, 0)

    gate = 1.0 / (1.0 + jnp.exp(-gate_ref[0]))
    for g in range(NSA_KV_GROUPS):
        o_c = per_group[g][2]
        for r in range(NSA_GROUP_SIZE):
            rs = slice(r * tq, (r + 1) * tq)
            o_s = acc_ref[g, 0, rs, 0:HEAD_DIM] / acc_ref[g, 0, rs, HEAD_DIM:2 * HEAD_DIM]
            o_w = acc_ref[g, 1, rs, 0:HEAD_DIM] / acc_ref[g, 1, rs, HEAD_DIM:2 * HEAD_DIM]
            c = (g * NSA_GROUP_SIZE + r) * N_GATES
            out = gate[:, c:c + 1] * o_c[rs] + gate[:, c + 1:c + 2] * o_s + gate[:, c + 2:c + 3] * o_w
            col = (g * NSA_GROUP_SIZE + r) * HEAD_DIM
            o_ref[0, :, col:col + HEAD_DIM] = out


def _nsa_attention(proj_a, proj_b, kcmp, vcmp, tbl, tblc, ovt, expt, batch, seq, cols):
    tq, tk, rows = NSA_TQ, NSA_TK, NSA_ROWS
    nbp = kcmp.shape[2]
    n_tiles = tbl.shape[1]
    groups = NSA_KV_GROUPS
    qblk = cols["n_q"] // NSA_WIDTH
    gblk = cols["gates"] // V7X_LANES
    assert cols["n_q"] % NSA_WIDTH == 0 and cols["gates"] % V7X_LANES == 0 and N_GATES * NSA_HEADS <= V7X_LANES

    def resident(shape, index_map):
        return pl.BlockSpec(shape, index_map, pipeline_mode=pl.Buffered(1))

    def kv_spec(name):
        assert cols[name] % NSA_KV_WIDTH == 0
        blk = cols[name] // NSA_KV_WIDTH
        return resident((1, seq, NSA_KV_WIDTH), lambda b, i: (b, 0, blk))

    cmp_spec = resident((1, groups, nbp, HEAD_DIM), lambda b, i: (b, 0, 0, 0))
    vmem = (4 * seq * NSA_KV_WIDTH * 2 + groups * n_tiles * rows * tk * 4 + 2 * groups * rows * nbp * 4
            + seq * V7X_LANES * 2 + 2 * groups * rows * (2 * HEAD_DIM + V7X_LANES) * 4
            + 4 * tq * (NSA_WIDTH + groups * V7X_LANES) * 4
            + 12 * rows * tk * 4)
    return pl.pallas_call(
        _nsa_kernel,
        grid=(batch, seq // tq),
        in_specs=[pl.BlockSpec((1, tq, NSA_WIDTH), lambda b, i: (b, i, qblk)),
                  cmp_spec, cmp_spec,
                  kv_spec("ks"), kv_spec("vs"), kv_spec("kw"), kv_spec("vw"),
                  pl.BlockSpec((1, tq, V7X_LANES), lambda b, i: (b, i, gblk)),
                  resident((groups, n_tiles, rows, tk), lambda b, i: (0, 0, 0, 0)),
                  pl.BlockSpec((groups, 1, rows, nbp), lambda b, i: (0, i, 0, 0)),
                  resident((V7X_LANES, nbp), lambda b, i: (0, 0)),
                  resident((seq, V7X_LANES), lambda b, i: (0, 0))],
        out_specs=pl.BlockSpec((1, tq, NSA_WIDTH), lambda b, i: (b, i, 0)),
        out_shape=jax.ShapeDtypeStruct((batch, seq, NSA_WIDTH), F32),
        scratch_shapes=[pltpu.VMEM((groups, 2, rows, 2 * HEAD_DIM), F32),
                        pltpu.VMEM((groups, 2, rows, V7X_LANES), F32)],
        compiler_params=_cparams(("parallel", "arbitrary"), vmem),
        name="nsa_attention",
    )(proj_a, kcmp, vcmp, proj_a, proj_a, proj_a, proj_a, proj_b, tbl, tblc, ovt, expt)


def _selection_constants(seq, nbp):
    nb = seq // CMP_STRIDE - 1
    nsel = seq // SEL_BLOCK
    assert nsel <= V7X_LANES and nbp >= nb
    ci = np.arange(nbp)[None, :] * CMP_STRIDE
    sj = np.arange(V7X_LANES)[:, None] * SEL_BLOCK
    ovt = ((ci < sj + SEL_BLOCK) & (ci + CMP_BLOCK > sj) & (np.arange(nbp)[None, :] < nb)
           & (np.arange(V7X_LANES)[:, None] < nsel))
    expt = (np.arange(seq)[:, None] // SEL_BLOCK) == np.arange(V7X_LANES)[None, :]
    return jnp.asarray(ovt, BF16), jnp.asarray(expt, BF16)


def _nsa_branch(proj_a3, proj_b3, proj_cv3, cmp_pos, cmp_w1, cmp_w2, rel_bias, cols):
    batch, seq, _ = proj_a3.shape
    nbp = seq // CMP_STRIDE
    half = CMP_STRIDE * HEAD_DIM
    pos = jnp.stack(cmp_pos).reshape(2, 2, 1, half)
    w1 = jnp.stack(cmp_w1).reshape(2, 2, half, CMP_HIDDEN).astype(BF16)
    w2 = jnp.stack(cmp_w2).astype(BF16)
    kcmp, vcmp = _compress(proj_cv3, pos, w1, w2, batch, nbp)

    tbl = _bias_table(rel_bias, BIAS_TILES, NSA_TK)
    tblc = _cmp_bias_table(rel_bias, seq // NSA_TQ, nbp)
    ovt, expt = _selection_constants(seq, nbp)
    return _nsa_attention(proj_a3, proj_b3, kcmp, vcmp, tbl, tblc, ovt, expt, batch, seq, cols)


def _out_kernel(osb_ref, zsb_ref, onsa_ref, znsa_ref, x_ref, gsb_ref, gnsa_ref, gfin_ref, wsb_ref, wnsa_ref,
                o_ref):
    def gated(o_r, z_r, g_r, rows):
        o = o_r[rows, :]
        y = o * lax.rsqrt(jnp.mean(o * o, axis=-1, keepdims=True) + RMS_EPS) * g_r[...]
        z = z_r[rows, :]
        return (y * (z * (1.0 / (1.0 + jnp.exp(-z))))).astype(BF16)

    for c in range(x_ref.shape[0] // OUT_CHUNK):
        rows = slice(c * OUT_CHUNK, (c + 1) * OUT_CHUNK)
        mixed = (jnp.dot(gated(osb_ref, zsb_ref, gsb_ref, rows), wsb_ref[...], preferred_element_type=F32)
                 + jnp.dot(gated(onsa_ref, znsa_ref, gnsa_ref, rows), wnsa_ref[...], preferred_element_type=F32))
        h = x_ref[rows, :] + mixed
        o_ref[rows, :] = h * lax.rsqrt(jnp.mean(h * h, axis=-1, keepdims=True) + RMS_EPS) * gfin_ref[...]


def _output_stage(o_sb, o_nsa, proj_b, x2d, g_sb, g_nsa, g_fin, w_out, tm):
    m, d = x2d.shape
    half = SB_WIDTH
    assert m % tm == 0 and tm % OUT_CHUNK == 0
    vmem = 2 * (4 * tm * half * 4 + 2 * tm * d * 4) + 2 * half * d * 2 + 6 * OUT_CHUNK * d * 4
    row = lambda c: pl.BlockSpec((tm, half), lambda i: (i, c))
    vec = lambda n: pl.BlockSpec((1, n), lambda i: (0, 0))
    weights = pl.Buffered(1)
    return pl.pallas_call(
        _out_kernel,
        grid=(m // tm,),
        in_specs=[row(0), row(0), row(0), row(1),
                  pl.BlockSpec((tm, d), lambda i: (i, 0)),
                  vec(half), vec(half), vec(d),
                  pl.BlockSpec((half, d), lambda i: (0, 0), pipeline_mode=weights),
                  pl.BlockSpec((half, d), lambda i: (1, 0), pipeline_mode=weights)],
        out_specs=pl.BlockSpec((tm, d), lambda i: (i, 0)),
        out_shape=jax.ShapeDtypeStruct((m, d), F32),
        compiler_params=_cparams(("parallel",), vmem),
        name="output_stage",
    )(o_sb, proj_b, o_nsa, proj_b, x2d, g_sb, g_nsa, g_fin, w_out, w_out)


def kernel(x, norm_in, w_in, cmp_k_pos, cmp_k_w1, cmp_k_w2, cmp_v_pos, cmp_v_w1, cmp_v_w2,
           rel_bias, norm_sb, norm_nsa, w_out, norm_final):
    batch, seq, d_model = x.shape
    assert w_in.shape[0] == 1, "single-layer trunk: the final norm is fused into the output stage"
    m = batch * seq
    nbp = seq // CMP_STRIDE
    assert seq % NSA_TK == 0 and nbp % V7X_LANES == 0

    sizes = (SB_WIDTH,) * 4 + (NSA_WIDTH,) + (NSA_KV_WIDTH,) * 6 + (N_GATES * NSA_HEADS, NSA_WIDTH)
    off = np.concatenate([[0], np.cumsum(sizes)])
    (c_sbq, c_sbk, c_sbv, c_sbz, c_nq, c_kc, c_vc, c_ks, c_vs, c_kw, c_vw, c_gate, c_nz, c_end) = [int(o) for o in off]
    cols = {"n_q": 3 * SB_WIDTH, "ks": 3 * SB_WIDTH + NSA_WIDTH}
    cols["vs"] = cols["ks"] + NSA_KV_WIDTH
    cols["kw"] = cols["vs"] + NSA_KV_WIDTH
    cols["vw"] = cols["kw"] + NSA_KV_WIDTH
    cols["gates"] = 2 * SB_WIDTH

    w = w_in[0]
    w_a = jnp.concatenate([w[:, c_sbq:c_sbz], w[:, c_nq:c_kc], w[:, c_ks:c_gate]], axis=1).astype(BF16)
    w_cv = w[:, c_kc:c_ks].astype(BF16)
    gate_pad = jnp.zeros((d_model, V7X_LANES - N_GATES * NSA_HEADS), w.dtype)
    w_b = jnp.concatenate([w[:, c_sbz:c_nq], w[:, c_nz:c_end], w[:, c_gate:c_nz], gate_pad], axis=1).astype(BF16)
    g_in = norm_in[0].reshape(1, d_model)
    x2d = x.reshape(m, d_model)

    q_scale = np.ones((1, w_a.shape[1]), np.float32)
    q_scale[:, c_sbq:c_sbq + SB_WIDTH] = SCALE * LOG2E
    q_scale[:, cols["n_q"]:cols["n_q"] + NSA_WIDTH] = SCALE * LOG2E
    proj_a, xn = _rms_proj(x2d, g_in, w_a, jnp.asarray(q_scale), BF16, PROJ_TM, w_a.shape[1] // PROJ_COL_TILES,
                           "proj_attn")
    proj_cv = _proj_chunk_major(xn, w_cv, PROJ_TM, "proj_cmp")
    proj_b = _plain_proj(xn, w_b, F32, PROJ_TM // 2, "proj_gate")

    proj_a3 = proj_a.reshape(batch, seq, -1)
    o_sb = _sb_attention(proj_a3, batch, seq)
    o_nsa = _nsa_branch(proj_a3, proj_b.reshape(batch, seq, -1), proj_cv.reshape(batch, nbp, -1),
                        (cmp_k_pos[0], cmp_v_pos[0]), (cmp_k_w1[0], cmp_v_w1[0]), (cmp_k_w2[0], cmp_v_w2[0]),
                        rel_bias, cols)

    out = _output_stage(o_sb.reshape(m, SB_WIDTH), o_nsa.reshape(m, NSA_WIDTH), proj_b, x2d,
                        norm_sb[0].reshape(1, SB_WIDTH), norm_nsa[0].reshape(1, NSA_WIDTH),
                        norm_final.reshape(1, d_model), w_out[0].astype(BF16), OUT_TM)
    return out.reshape(batch, seq, d_model)
```
